```python
import math
import jax, jax.numpy as jnp
from jax import lax
import numpy as np

D_MODEL = 1024
BATCH = 4
SEQ = 8192
DEPTH = 4

GRID_W = 64
CTX_LEN = 256
HEAD_DIM = 64
MIX_WIDTH = D_MODEL
N_GROUPS = 4
GROUP_WIDTH = MIX_WIDTH // N_GROUPS
DIFF_HEADS = GROUP_WIDTH // HEAD_DIM
DIFF_QK_DIM = HEAD_DIM // 2
SWA_HEADS = GROUP_WIDTH // HEAD_DIM
SWA_KV_HEADS = 2
SWA_GROUP = SWA_HEADS // SWA_KV_HEADS
SWA_WINDOW = 128
SWA_BLOCK = 128
NA_HEADS = GROUP_WIDTH // HEAD_DIM
NA_ROWS_MAX = 8
NA_COLS = 16
SCONV_CH = GROUP_WIDTH
SCONV_WIDTH = 3
D_FF = 2816
FFN_CONV_WIDTH = 3
Q_BLOCK = 128
ROPE_THETA = 10000.0
EPS = 1e-6
NEG_INF = -1e30

A_Q = DIFF_HEADS * 2 * DIFF_QK_DIM
A_K = A_Q
A_V = DIFF_HEADS * HEAD_DIM
A_COLS = A_Q + A_K + A_V
B_Q = SWA_HEADS * HEAD_DIM
B_KV = SWA_KV_HEADS * HEAD_DIM
B_COLS = B_Q + 2 * B_KV
C_W = NA_HEADS * HEAD_DIM
C_COLS = 3 * C_W
D_COLS = 3 * SCONV_CH
IN_COLS = A_COLS + B_COLS + C_COLS + D_COLS

kernel_name = "hybrid_parallel_heads_diffusion_trunk"


def rmsnorm(x, w):
    xf = x.astype(jnp.float32)
    y = xf * lax.rsqrt(jnp.mean(xf * xf, axis=-1, keepdims=True) + EPS)
    return (y * w.astype(jnp.float32)).astype(x.dtype)


def modulate(x, shift, scale):
    return x * (1.0 + scale) + shift


def dwconv3(x, w):
    xp = jnp.pad(x, ((0, 0), (1, 1), (0, 0)))
    return xp[:, :-2] * w[0] + xp[:, 1:-1] * w[1] + xp[:, 2:] * w[2]


def axial_rope_angles(n_tokens, dim):
    t = jnp.arange(n_tokens)
    rows = (t // GRID_W).astype(jnp.float32)
    cols = (t % GRID_W).astype(jnp.float32)
    quarter = dim // 4
    freqs = ROPE_THETA ** (-(jnp.arange(quarter, dtype=jnp.float32) / quarter))
    return rows[:, None] * freqs[None], cols[:, None] * freqs[None]


def _rot_half(x, ang):
    cos = jnp.cos(ang).astype(x.dtype)
    sin = jnp.sin(ang).astype(x.dtype)
    x1, x2 = jnp.split(x, 2, axis=-1)
    return jnp.concatenate([x1 * cos - x2 * sin, x2 * cos + x1 * sin], axis=-1)


def apply_axial_rope(x, ang_row, ang_col):
    half = x.shape[-1] // 2
    return jnp.concatenate([_rot_half(x[..., :half], ang_row), _rot_half(x[..., half:], ang_col)], axis=-1)


def merge_heads(o):
    b, h, s, d = o.shape
    return o.transpose(0, 2, 1, 3).reshape(b, s, h * d)


def diff_attention(h_l, h_c, lam_q1, lam_k1, lam_q2, lam_k2, subln_w, layer_idx, ang_row, ang_col, need_ctx):
    f32 = jnp.float32
    lambda_init = 0.8 - 0.6 * math.exp(-0.3 * layer_idx)
    lam = (jnp.exp(jnp.sum(lam_q1.astype(f32) * lam_k1.astype(f32)))
           - jnp.exp(jnp.sum(lam_q2.astype(f32) * lam_k2.astype(f32))) + lambda_init)
    scale = DIFF_QK_DIM ** -0.5

    def qkv(h):
        b, s, _ = h.shape
        q = h[..., :A_Q].reshape(b, s, DIFF_HEADS, 2, DIFF_QK_DIM).transpose(0, 2, 3, 1, 4)
        k = h[..., A_Q:A_Q + A_K].reshape(b, s, DIFF_HEADS, 2, DIFF_QK_DIM).transpose(0, 2, 3, 1, 4)
        v = h[..., A_Q + A_K:].reshape(b, s, DIFF_HEADS, HEAD_DIM).transpose(0, 2, 1, 3)
        return q, k, v

    def attend(q, k, v):
        s = jnp.einsum('bhiqd,bhikd->bhiqk', q, k).astype(f32) * scale
        p = jax.nn.softmax(s, axis=-1)
        w = (p[:, :, 0] - lam * p[:, :, 1]).astype(v.dtype)
        o = jnp.einsum('bhqk,bhkd->bhqd', w, v)
        return rmsnorm(o, subln_w) * (1.0 - lambda_init)

    ql, kl, vl = qkv(h_l)
    qc, kc, vc = qkv(h_c)
    ql = apply_axial_rope(ql, ang_row, ang_col)
    kl = apply_axial_rope(kl, ang_row, ang_col)
    k_all = jnp.concatenate([kl, kc], axis=3)
    v_all = jnp.concatenate([vl, vc], axis=2)
    b, h, _, s, d = ql.shape
    nb = s // Q_BLOCK
    qb = ql.reshape(b, h, 2, nb, Q_BLOCK, d).transpose(3, 0, 1, 2, 4, 5)
    ob = lax.map(lambda q: attend(q, k_all, v_all), qb)
    o_l = merge_heads(ob.transpose(1, 2, 0, 3, 4).reshape(b, h, s, HEAD_DIM))
    o_c = merge_heads(attend(qc, kc, vc)) if need_ctx else None
    return o_l, o_c


def window_gqa(h_l, h_c, sink, ang_row, ang_col, need_ctx):
    f32 = jnp.float32
    scale = HEAD_DIM ** -0.5
    sink_f = sink.astype(f32).reshape(SWA_KV_HEADS, SWA_GROUP)

    def qkv(h):
        b, s, _ = h.shape
        q = h[..., :B_Q].reshape(b, s, SWA_KV_HEADS, SWA_GROUP, HEAD_DIM).transpose(0, 2, 3, 1, 4)
        k = h[..., B_Q:B_Q + B_KV].reshape(b, s, SWA_KV_HEADS, HEAD_DIM).transpose(0, 2, 1, 3)
        v = h[..., B_Q + B_KV:].reshape(b, s, SWA_KV_HEADS, HEAD_DIM).transpose(0, 2, 1, 3)
        return q, k, v

    ql, kl, vl = qkv(h_l)
    qc, kc, vc = qkv(h_c)
    ql = apply_axial_rope(ql, ang_row, ang_col)
    kl = apply_axial_rope(kl, ang_row, ang_col)
    b, _, _, s, d = ql.shape
    n_ctx = kc.shape[2]
    nb = s // SWA_BLOCK
    pad = ((0, 0), (0, 0), (SWA_BLOCK, SWA_BLOCK), (0, 0))
    kp = jnp.pad(kl, pad).reshape(b, SWA_KV_HEADS, nb + 2, SWA_BLOCK, d)
    vp = jnp.pad(vl, pad).reshape(b, SWA_KV_HEADS, nb + 2, SWA_BLOCK, d)
    kband = jnp.concatenate([kp[:, :, 0:nb], kp[:, :, 1:nb + 1], kp[:, :, 2:nb + 2]], axis=3)
    vband = jnp.concatenate([vp[:, :, 0:nb], vp[:, :, 1:nb + 1], vp[:, :, 2:nb + 2]], axis=3)
    qb = ql.reshape(b, SWA_KV_HEADS, SWA_GROUP, nb, SWA_BLOCK, d)
    s_loc = jnp.einsum('bkgnqd,bknjd->bkgnqj', qb, kband).astype(f32) * scale
    blk = jnp.arange(nb)[:, None, None]
    qpos = blk * SWA_BLOCK + jnp.arange(SWA_BLOCK)[None, :, None]
    kpos = blk * SWA_BLOCK - SWA_BLOCK + jnp.arange(3 * SWA_BLOCK)[None, None, :]
    band = (jnp.abs(kpos - qpos) <= SWA_WINDOW) & (kpos >= 0) & (kpos < s)
    s_loc = jnp.where(band, s_loc, NEG_INF)
    s_ctx = jnp.einsum('bkgnqd,bkld->bkgnql', qb, kc).astype(f32) * scale
    s_snk = jnp.broadcast_to(sink_f[None, :, :, None, None, None], s_loc.shape[:-1] + (1,))
    p = jax.nn.softmax(jnp.concatenate([s_loc, s_ctx, s_snk], axis=-1), axis=-1)
    nk = 3 * SWA_BLOCK
    o = (jnp.einsum('bkgnqj,bknjd->bkgnqd', p[..., :nk].astype(vl.dtype), vband)
         + jnp.einsum('bkgnql,bkld->bkgnqd', p[..., nk:nk + n_ctx].astype(vl.dtype), vc))
    o_l = merge_heads(o.reshape(b, SWA_HEADS, s, d))
    o_c = None
    if need_ctx:
        sc = jnp.einsum('bkgqd,bkld->bkgql', qc, kc).astype(f32) * scale
        sc_snk = jnp.broadcast_to(sink_f[None, :, :, None, None], sc.shape[:-1] + (1,))
        pc = jax.nn.softmax(jnp.concatenate([sc, sc_snk], axis=-1), axis=-1)
        oc = jnp.einsum('bkgql,bkld->bkgqd', pc[..., :n_ctx].astype(vc.dtype), vc)
        o_c = merge_heads(oc.reshape(b, SWA_HEADS, n_ctx, d))
    return o_l, o_c


def neighborhood_attention(h_l, h_c, rpb, need_ctx):
    f32 = jnp.float32
    scale = HEAD_DIM ** -0.5

    def qkv(h):
        b, s, _ = h.shape
        q = h[..., :C_W].reshape(b, s, NA_HEADS, HEAD_DIM).transpose(0, 2, 1, 3)
        k = h[..., C_W:2 * C_W].reshape(b, s, NA_HEADS, HEAD_DIM).transpose(0, 2, 1, 3)
        v = h[..., 2 * C_W:].reshape(b, s, NA_HEADS, HEAD_DIM).transpose(0, 2, 1, 3)
        return q, k, v

    ql, kl, vl = qkv(h_l)
    qc, kc, vc = qkv(h_c)
    b, h, s, d = ql.shape
    n_ctx = kc.shape[2]
    rows = s // GRID_W
    kr = min(NA_ROWS_MAX, rows)
    qg = ql.reshape(b, h, rows, GRID_W, d)
    kg = kl.reshape(b, h, rows, GRID_W, d)
    vg = vl.reshape(b, h, rows, GRID_W, d)
    r = jnp.arange(rows)
    row_start = jnp.clip(r - kr // 2, 0, rows - kr)
    key_rows = row_start[:, None] + jnp.arange(kr)[None, :]
    krows = jnp.take(kg, key_rows, axis=2)
    vrows = jnp.take(vg, key_rows, axis=2)
    s_loc = jnp.einsum('bhrqd,bhrjcd->bhrqjc', qg, krows).astype(f32) * scale
    cq = jnp.arange(GRID_W)
    col_start = jnp.clip(cq - NA_COLS // 2, 0, GRID_W - NA_COLS)
    ck = jnp.arange(GRID_W)
    col_mask = (ck[None, :] >= col_start[:, None]) & (ck[None, :] < col_start[:, None] + NA_COLS)
    dr = key_rows - r[:, None]
    dc = jnp.clip(ck[None, :] - cq[:, None], -(NA_COLS - 1), NA_COLS - 1)
    bias = rpb[:, (dr + NA_ROWS_MAX - 1)[:, None, :, None], (dc + NA_COLS - 1)[None, :, None, :]]
    s_loc = jnp.where(col_mask[None, None, None, :, None, :], s_loc + bias[None].astype(f32), NEG_INF)
    s_loc = s_loc.reshape(b, h, rows, GRID_W, kr * GRID_W)
    s_ctx = jnp.einsum('bhrqd,bhld->bhrql', qg, kc).astype(f32) * scale
    p = jax.nn.softmax(jnp.concatenate([s_loc, s_ctx], axis=-1), axis=-1)
    nk = kr * GRID_W
    p_loc = p[..., :nk].reshape(b, h, rows, GRID_W, kr, GRID_W).astype(vl.dtype)
    o = (jnp.einsum('bhrqjc,bhrjcd->bhrqd', p_loc, vrows)
         + jnp.einsum('bhrql,bhld->bhrqd', p[..., nk:].astype(vl.dtype), vc))
    o_l = merge_heads(o.reshape(b, h, s, d))
    o_c = None
    if need_ctx:
        sc = jnp.einsum('bhqd,bhkd->bhqk', qc, kc).astype(f32) * scale
        pc = jax.nn.softmax(sc, axis=-1).astype(vc.dtype)
        o_c = merge_heads(jnp.einsum('bhqk,bhkd->bhqd', pc, vc))
    return o_l, o_c


def short_conv_mixer(h_l, h_c, conv_w, need_ctx):
    def run(h):
        bg = h[..., :SCONV_CH]
        cg = h[..., SCONV_CH:2 * SCONV_CH]
        xi = h[..., 2 * SCONV_CH:]
        return bg * dwconv3(cg * xi, conv_w)
    o_l = run(h_l)
    o_c = run(h_c) if need_ctx else None
    return o_l, o_c


def conv_ffn(h, w_up, conv_w, conv_b, w_down):
    u = dwconv3(h @ w_up, conv_w) + conv_b
    g, v = jnp.split(u, 2, axis=-1)
    return (jax.nn.silu(g) * v) @ w_down


def setup_inputs(seed: int = 0) -> dict:
    key = jax.random.key(seed)
    ks = jax.random.split(key, 24)
    f32 = jnp.float32
    nrm = lambda k, shape, s: jax.random.normal(k, shape, f32) * s
    return {
        "x": nrm(ks[0], (BATCH, SEQ, D_MODEL), 1.0),
        "c": nrm(ks[1], (BATCH, D_MODEL), 1.0),
        "ctx": nrm(ks[2], (BATCH, CTX_LEN, D_MODEL), 1.0),
        "c_ctx": nrm(ks[3], (D_MODEL,), 1.0),
        "norm_mix_w": 1.0 + nrm(ks[4], (DEPTH, D_MODEL), 0.01),
        "norm_ffn_w": 1.0 + nrm(ks[5], (DEPTH, D_MODEL), 0.01),
        "w_mod": nrm(ks[6], (DEPTH, D_MODEL, 6 * D_MODEL), 0.5 * D_MODEL ** -0.5),
        "b_mod": nrm(ks[7], (DEPTH, 6 * D_MODEL), 0.02),
        "w_in": nrm(ks[8], (DEPTH, D_MODEL, IN_COLS), D_MODEL ** -0.5),
        "w_out": nrm(ks[9], (DEPTH, MIX_WIDTH, D_MODEL), MIX_WIDTH ** -0.5),
        "diff_lambda_q1": nrm(ks[10], (DEPTH, DIFF_QK_DIM), 0.1),
        "diff_lambda_k1": nrm(ks[11], (DEPTH, DIFF_QK_DIM), 0.1),
        "diff_lambda_q2": nrm(ks[12], (DEPTH, DIFF_QK_DIM), 0.1),
        "diff_lambda_k2": nrm(ks[13], (DEPTH, DIFF_QK_DIM), 0.1),
        "diff_subln_w": 1.0 + nrm(ks[14], (DEPTH, HEAD_DIM), 0.01),
        "swa_sink": nrm(ks[15], (DEPTH, SWA_HEADS), 0.5),
        "na_rpb": nrm(ks[16], (DEPTH, NA_HEADS, 2 * NA_ROWS_MAX - 1, 2 * NA_COLS - 1), 0.02),
        "sconv_w": nrm(ks[17], (DEPTH, SCONV_WIDTH, SCONV_CH), SCONV_WIDTH ** -0.5),
        "ffn_w_up": nrm(ks[18], (DEPTH, D_MODEL, 2 * D_FF), D_MODEL ** -0.5),
        "ffn_conv_w": nrm(ks[19], (DEPTH, FFN_CONV_WIDTH, 2 * D_FF), FFN_CONV_WIDTH ** -0.5),
        "ffn_conv_b": nrm(ks[20], (DEPTH, 2 * D_FF), 0.02),
        "ffn_w_down": nrm(ks[21], (DEPTH, D_FF, D_MODEL), D_FF ** -0.5),
        "final_norm_w": 1.0 + nrm(ks[22], (D_MODEL,), 0.01),
    }


def reference(x, c, ctx, c_ctx, norm_mix_w, norm_ffn_w, w_mod, b_mod, w_in, w_out,
              diff_lambda_q1, diff_lambda_k1, diff_lambda_q2, diff_lambda_k2, diff_subln_w,
              swa_sink, na_rpb, sconv_w, ffn_w_up, ffn_conv_w, ffn_conv_b, ffn_w_down, final_norm_w):
    s = x.shape[1]
    ang_a_row, ang_a_col = axial_rope_angles(s, DIFF_QK_DIM)
    ang_b_row, ang_b_col = axial_rope_angles(s, HEAD_DIM)
    c_act = jax.nn.silu(c)
    cc_act = jax.nn.silu(c_ctx)
    o1 = A_COLS
    o2 = o1 + B_COLS
    o3 = o2 + C_COLS
    xl, xc = x, ctx
    for l in range(DEPTH):
        need_ctx = l < DEPTH - 1
        mod_l = (c_act @ w_mod[l] + b_mod[l])[:, None, :]
        mod_c = (cc_act @ w_mod[l] + b_mod[l])[None, None, :]
        sh1, sc1, g1, sh2, sc2, g2 = jnp.split(mod_l, 6, axis=-1)
        csh1, csc1, cg1, csh2, csc2, cg2 = jnp.split(mod_c, 6, axis=-1)
        hl = modulate(rmsnorm(xl, norm_mix_w[l]), sh1, sc1) @ w_in[l]
        hc = modulate(rmsnorm(xc, norm_mix_w[l]), csh1, csc1) @ w_in[l]
        oa_l, oa_c = diff_attention(hl[..., :o1], hc[..., :o1], diff_lambda_q1[l], diff_lambda_k1[l],
                                    diff_lambda_q2[l], diff_lambda_k2[l], diff_subln_w[l], l,
                                    ang_a_row, ang_a_col, need_ctx)
        ob_l, ob_c = window_gqa(hl[..., o1:o2], hc[..., o1:o2], swa_sink[l], ang_b_row, ang_b_col, need_ctx)
        oc_l, oc_c = neighborhood_attention(hl[..., o2:o3], hc[..., o2:o3], na_rpb[l], need_ctx)
        od_l, od_c = short_conv_mixer(hl[..., o3:], hc[..., o3:], sconv_w[l], need_ctx)
        yl = jnp.concatenate([oa_l, ob_l, oc_l, od_l], axis=-1) @ w_out[l]
        xl = xl + g1 * yl
        fl = modulate(rmsnorm(xl, norm_ffn_w[l]), sh2, sc2)
        xl = xl + g2 * conv_ffn(fl, ffn_w_up[l], ffn_conv_w[l], ffn_conv_b[l], ffn_w_down[l])
        if need_ctx:
            yc = jnp.concatenate([oa_c, ob_c, oc_c, od_c], axis=-1) @ w_out[l]
            xc = xc + cg1 * yc
            fc = modulate(rmsnorm(xc, norm_ffn_w[l]), csh2, csc2)
            xc = xc + cg2 * conv_ffn(fc, ffn_w_up[l], ffn_conv_w[l], ffn_conv_b[l], ffn_w_down[l])
    return rmsnorm(xl, final_norm_w)
```

```python
import functools
import math

import numpy as np
import jax
import jax.numpy as jnp
from jax import lax
from jax.experimental import pallas as pl
from jax.experimental.pallas import tpu as pltpu

D_MODEL = 1024
DEPTH = 4
GRID_W = 64
CTX_LEN = 256
HEAD_DIM = 64
GROUP_WIDTH = 256
DIFF_HEADS = 4
DIFF_QK_DIM = 32
SWA_HEADS = 4
SWA_KV_HEADS = 2
SWA_WINDOW = 128
NA_HEADS = 4
NA_ROWS = 8
NA_COLS = 16
D_FF = 2816
IN_COLS = 2816
ROPE_THETA = 10000.0
EPS = 1e-6
NEG_INF = -1e30
LOG2E = math.log2(math.e)

TILE = 256
HALO_F32 = 8
HALO_BF16 = 16
LANES = 128
VMEM_LIMIT = 56 * 1024 * 1024

F32 = jnp.float32
BF16 = jnp.bfloat16


def _params(n_grid):
    return pltpu.CompilerParams(
        dimension_semantics=("arbitrary",) * n_grid, vmem_limit_bytes=VMEM_LIMIT)


def _resident(shape):
    nd = len(shape)
    return pl.BlockSpec(shape, lambda *_: (0,) * nd, pipeline_mode=pl.Buffered(1))


def _mod_spec(n_batch):
    return pl.BlockSpec((1, 1, 6 * D_MODEL), lambda b, j: (jnp.where(j == 0, n_batch, b), 0, 0))


def _rmsnorm_mod(x, w, shift, scale):
    ms = jnp.mean(x * x, axis=-1, keepdims=True)
    return (x * lax.rsqrt(ms + EPS) * w) * (1.0 + scale) + shift


def _dot(a, b):
    return jnp.dot(a, b, preferred_element_type=F32)


def _dot_nt(a, b):
    return lax.dot_general(a, b, (((1,), (1,)), ((), ())), preferred_element_type=F32)


MOD_BLOCK = 1536


def _mod_kernel(c_ref, w_ref, b_ref, o_ref):
    c = c_ref[...]
    a = c * (1.0 / (1.0 + jnp.exp(-c)))
    o_ref[0] = _dot(a.astype(BF16), w_ref[0].astype(BF16)) + b_ref[0]


def _mod_call(cin, w_mod, b_mod):
    depth = w_mod.shape[0]
    n = 6 * D_MODEL
    return pl.pallas_call(
        _mod_kernel,
        grid=(depth, n // MOD_BLOCK),
        in_specs=[
            pl.BlockSpec((8, D_MODEL), lambda l, k: (0, 0)),
            pl.BlockSpec((1, D_MODEL, MOD_BLOCK), lambda l, k: (l, 0, k)),
            pl.BlockSpec((1, 1, MOD_BLOCK), lambda l, k: (l, 0, k)),
        ],
        out_specs=pl.BlockSpec((1, 8, MOD_BLOCK), lambda l, k: (l, 0, k)),
        out_shape=jax.ShapeDtypeStruct((depth, 8, n), F32),
        compiler_params=_params(2),
        name="mod_proj",
    )(cin, w_mod, b_mod.reshape(depth, 1, n))


def _rope(x, cos, sin, half):
    lane = lax.broadcasted_iota(jnp.int32, x.shape, 1)
    first = (lane % (2 * half)) < half
    partner = jnp.where(first, pltpu.roll(x, LANES - half, 1), pltpu.roll(x, half, 1))
    return x * cos + partner * sin


def _in_proj_kernel(x_ref, mod_ref, nw_ref, w_ref, cosa_ref, sina_ref, cosb_ref, sinb_ref,
                    qa_ref, ka_ref, vat_ref, bq_ref, bkv_ref, cq_ref, ckv_ref, hd_ref):
    mod = mod_ref[0]
    shift = mod[:, 0:D_MODEL]
    scale = mod[:, D_MODEL:2 * D_MODEL]
    z = _rmsnorm_mod(x_ref[0], nw_ref[...], shift, scale)
    h = _dot(z.astype(BF16), w_ref[...])

    def sec(k, n=1):
        return h[:, k * LANES:(k + n) * LANES]

    cosa, sina = cosa_ref[...], sina_ref[...]
    cosb, sinb = cosb_ref[...], sinb_ref[...]
    qa_scale = DIFF_QK_DIM ** -0.5 * LOG2E
    q_scale = HEAD_DIM ** -0.5
    ha = DIFF_QK_DIM // 4
    hb = HEAD_DIM // 4
    for t in range(2):
        qa_ref[0, :, t * LANES:(t + 1) * LANES] = (
            _rope(sec(t), cosa, sina, ha) * qa_scale).astype(BF16)
        ka_ref[0, :, t * LANES:(t + 1) * LANES] = _rope(sec(2 + t), cosa, sina, ha).astype(BF16)
    vat_ref[0, 0] = sec(4, 2).T.astype(BF16)
    for t in range(2):
        bq_ref[0, :, t * LANES:(t + 1) * LANES] = (
            _rope(sec(6 + t), cosb, sinb, hb) * q_scale).astype(BF16)
    bkv_ref[0, :, 0:LANES] = _rope(sec(8), cosb, sinb, hb).astype(BF16)
    bkv_ref[0, :, LANES:2 * LANES] = sec(9).astype(BF16)
    cq_ref[0] = (sec(10, 2) * q_scale).astype(BF16)
    ckv_ref[0] = sec(12, 4).astype(BF16)
    hd_ref[0] = sec(16, 6).astype(BF16)


def _in_proj_call(xall, mod, norm_w, w_in, tabs):
    nb, rows, _ = xall.shape
    nt = rows // TILE
    tile = lambda w: pl.BlockSpec((1, TILE, w), lambda b, j: (b, j, 0))
    tab = pl.BlockSpec((TILE, LANES), lambda b, j: (j, 0))
    sds = lambda w: jax.ShapeDtypeStruct((nb, rows, w), BF16)
    return pl.pallas_call(
        _in_proj_kernel,
        grid=(nb, nt),
        in_specs=[
            pl.BlockSpec((1, TILE, D_MODEL), lambda b, j: (b, j, 0)),
            _mod_spec(nb),
            _resident((1, D_MODEL)),
            _resident((D_MODEL, IN_COLS)),
            tab, tab, tab, tab,
        ],
        out_specs=[
            tile(256), tile(256),
            pl.BlockSpec((1, 1, 256, TILE), lambda b, j: (b, j, 0, 0)),
            tile(256), tile(256), tile(256), tile(512), tile(768),
        ],
        out_shape=[
            sds(256), sds(256),
            jax.ShapeDtypeStruct((nb, nt, 256, TILE), BF16),
            sds(256), sds(256), sds(256), sds(512), sds(768),
        ],
        compiler_params=_params(2),
        name="in_proj",
    )(xall, mod, norm_w, w_in, *tabs)


QA_TILE = 128
N_GROUPS_A = 2 * DIFF_HEADS
PV_ROWS = HEAD_DIM + HALO_BF16


def _diff_attn_kernel(li_ref, lv_ref, q_ref, k_ref, vt_ref, sw_ref, o_ref, m_ref, acc_ref):
    i = pl.program_id(1)
    q = q_ref[0]
    lane = lax.broadcasted_iota(jnp.int32, q.shape, 1) // DIFF_QK_DIM
    qs = jnp.concatenate(
        [jnp.where(lane == g, q, jnp.zeros_like(q)) for g in range(N_GROUPS_A)], axis=0)
    m_ref[...] = jnp.full(m_ref.shape, NEG_INF, F32)
    acc_ref[...] = jnp.zeros(acc_ref.shape, F32)
    ones = jnp.ones((HALO_BF16, TILE), BF16)
    n_cols = 2 * QA_TILE

    def step(t, carry):
        k = k_ref[0, pl.ds(pl.multiple_of(t * TILE, TILE), TILE), :]
        s = _dot_nt(k, qs)
        m_old = m_ref[...]
        m_new = jnp.maximum(m_old, jnp.max(s, axis=0, keepdims=True))
        alpha = jnp.exp2(m_old - m_new)
        p = jnp.exp2(s - m_new).astype(BF16)
        m_ref[...] = m_new
        vt = vt_ref[0, t]
        for h in range(DIFF_HEADS):
            v_ext = jnp.concatenate([vt[h * HEAD_DIM:(h + 1) * HEAD_DIM], ones], axis=0)
            cols = slice(h * n_cols, (h + 1) * n_cols)
            acc_ref[h] = acc_ref[h] * alpha[:, cols] + _dot(v_ext, p[:, cols])
        return carry

    n_kv = jnp.where(i < CTX_LEN // QA_TILE, 1, k_ref.shape[1] // TILE)
    lax.fori_loop(0, n_kv, step, 0)

    lv = lv_ref[...]
    lam = (jnp.exp(jnp.sum(lv[0:1] * lv[1:2], axis=1, keepdims=True))
           - jnp.exp(jnp.sum(lv[2:3] * lv[3:4], axis=1, keepdims=True)) + li_ref[0])
    out_scale = li_ref[1]
    outs = []
    for h in range(DIFF_HEADS):
        a = acc_ref[h]
        o1 = a[0:HEAD_DIM, 0:QA_TILE] / a[HEAD_DIM:HEAD_DIM + 1, 0:QA_TILE]
        o2 = a[0:HEAD_DIM, QA_TILE:n_cols] / a[HEAD_DIM:HEAD_DIM + 1, QA_TILE:n_cols]
        o = o1 - lam * o2
        ms = jnp.mean(o * o, axis=0, keepdims=True)
        outs.append(o * lax.rsqrt(ms + EPS) * sw_ref[...] * out_scale)
    o_ref[0] = jnp.concatenate(outs, axis=0).T.astype(BF16)


def _diff_attn_call(lam_init, lam_vecs, qa, ka, vat, subw):
    nb, rows, _ = qa.shape
    nt = rows // TILE
    return pl.pallas_call(
        _diff_attn_kernel,
        grid=(nb, rows // QA_TILE),
        in_specs=[
            pl.BlockSpec(memory_space=pltpu.SMEM),
            _resident((4, DIFF_QK_DIM)),
            pl.BlockSpec((1, QA_TILE, 256), lambda b, i: (b, i, 0)),
            pl.BlockSpec((1, rows, 256), lambda b, i: (b, 0, 0)),
            pl.BlockSpec((1, nt, 256, TILE), lambda b, i: (b, 0, 0, 0)),
            _resident((HEAD_DIM, QA_TILE)),
        ],
        out_specs=pl.BlockSpec((1, QA_TILE, 256), lambda b, i: (b, i, 0)),
        out_shape=jax.ShapeDtypeStruct((nb, rows, 256), BF16),
        scratch_shapes=[
            pltpu.VMEM((1, N_GROUPS_A * QA_TILE), F32),
            pltpu.VMEM((DIFF_HEADS, PV_ROWS, 2 * QA_TILE), F32),
        ],
        compiler_params=_params(2),
        name="diff_attn",
    )(lam_init, lam_vecs, qa, ka, vat, subw)


def _softmax_pv(s_parts, v_parts, extra_logit=None):
    m = s_parts[0].max(axis=-1, keepdims=True)
    for s in s_parts[1:]:
        m = jnp.maximum(m, s.max(axis=-1, keepdims=True))
    if extra_logit is not None:
        m = jnp.maximum(m, extra_logit)
    l = jnp.exp(extra_logit - m) if extra_logit is not None else 0.0
    o = None
    for s, v in zip(s_parts, v_parts):
        p = jnp.exp(s - m)
        l = l + p.sum(axis=-1, keepdims=True)
        pv = _dot(p.astype(BF16), v)
        o = pv if o is None else o + pv
    return o / l


def _half_stack(q):
    lane = lax.broadcasted_iota(jnp.int32, q.shape, 1) // HEAD_DIM
    zero = jnp.zeros_like(q)
    return jnp.concatenate([jnp.where(lane == 0, q, zero), jnp.where(lane == 1, q, zero)], axis=0)


def _half_merge(o):
    rows = o.shape[0] // 2
    lane = lax.broadcasted_iota(jnp.int32, (rows, LANES), 1) // HEAD_DIM
    return jnp.where(lane == 0, o[:rows], o[rows:])


def _local_mix_kernel(sink_ref, bq_ref, bkvp_ref, bkvc_ref, bkvn_ref, bkvx_ref,
                      cq_ref, ckvp_ref, ckvc_ref, ckvn_ref, ckvx_ref, ctab_ref,
                      dp_ref, dc_ref, dn_ref, cw_ref, o_ref):
    j = pl.program_id(1)
    n_tiles = pl.num_programs(1)
    n_lat = (n_tiles - 1) * TILE
    half = SWA_WINDOW

    k_loc = jnp.concatenate([bkvp_ref[0, :, 0:LANES], bkvc_ref[0, :, 0:LANES],
                             bkvn_ref[0, :, 0:LANES]], axis=0)
    v_loc = jnp.concatenate([bkvp_ref[0, :, LANES:], bkvc_ref[0, :, LANES:],
                             bkvn_ref[0, :, LANES:]], axis=0)
    k_ctx = bkvx_ref[0, :, 0:LANES]
    v_ctx = bkvx_ref[0, :, LANES:]
    n_loc = TILE + 2 * half
    rq = lax.broadcasted_iota(jnp.int32, (TILE, n_loc), 0)
    kk = lax.broadcasted_iota(jnp.int32, (TILE, n_loc), 1)
    kpos = (j - 1) * TILE - half + kk
    band = (jnp.abs(kk - half - rq) <= SWA_WINDOW) & (kpos >= 0) & (kpos < n_lat) & (j > 0)
    band2 = jnp.concatenate([band, band], axis=0)
    for p in range(2):
        qs = _half_stack(bq_ref[0, :, p * LANES:(p + 1) * LANES])
        s_loc = jnp.where(band2, _dot_nt(qs, k_loc), NEG_INF)
        s_ctx = _dot_nt(qs, k_ctx)
        sink = jnp.concatenate([jnp.full((TILE, 1), sink_ref[p], F32),
                                jnp.full((TILE, 1), sink_ref[2 + p], F32)], axis=0)
        o = _softmax_pv([s_loc, s_ctx], [v_loc, v_ctx], sink)
        o_ref[0, :, p * LANES:(p + 1) * LANES] = _half_merge(o).astype(BF16)

    n_locc = 3 * TILE
    rows_per_tile = TILE // GRID_W
    n_rows = n_lat // GRID_W
    rqc = lax.broadcasted_iota(jnp.int32, (TILE, n_locc), 0) // GRID_W
    krc = lax.broadcasted_iota(jnp.int32, (TILE, n_locc), 1) // GRID_W
    r = (j - 1) * rows_per_tile + rqc
    kr = (j - 2) * rows_per_tile + krc
    rs = jnp.clip(r - NA_ROWS // 2, 0, n_rows - NA_ROWS)
    row_ok = (kr >= rs) & (kr < rs + NA_ROWS) & (j > 0)
    for p in range(2):
        ksl = slice(p * LANES, (p + 1) * LANES)
        vsl = slice(2 * LANES + p * LANES, 2 * LANES + (p + 1) * LANES)
        k_loc = jnp.concatenate([ckvp_ref[0, :, ksl], ckvc_ref[0, :, ksl], ckvn_ref[0, :, ksl]], axis=0)
        v_loc = jnp.concatenate([ckvp_ref[0, :, vsl], ckvc_ref[0, :, vsl], ckvn_ref[0, :, vsl]], axis=0)
        qs = _half_stack(cq_ref[0, :, ksl])
        s_raw = _dot_nt(qs, k_loc)
        s_loc = jnp.concatenate(
            [jnp.where(row_ok, s_raw[k * TILE:(k + 1) * TILE] + ctab_ref[2 * p + k], NEG_INF)
             for k in range(2)], axis=0)
        s_ctx = _dot_nt(qs, ckvx_ref[0, :, ksl])
        o = _softmax_pv([s_loc, s_ctx], [v_loc, ckvx_ref[0, :, vsl]])
        o_ref[0, :, (2 + p) * LANES:(3 + p) * LANES] = _half_merge(o).astype(BF16)

    has_prev = (j > 1).astype(F32)
    has_next = ((j > 0) & (j < n_tiles - 1)).astype(F32)
    dc = dc_ref[0].astype(F32)
    dp = dp_ref[0].astype(F32)[HALO_BF16 - HALO_F32:]
    dn = dn_ref[0].astype(F32)[:HALO_F32]
    gw = GROUP_WIDTH
    z = jnp.concatenate([dp[:, gw:2 * gw] * dp[:, 2 * gw:] * has_prev,
                         dc[:, gw:2 * gw] * dc[:, 2 * gw:],
                         dn[:, gw:2 * gw] * dn[:, 2 * gw:] * has_next], axis=0)
    cw = cw_ref[...]
    c0 = HALO_F32
    conv = (z[c0 - 1:c0 - 1 + TILE] * cw[0:1] + z[c0:c0 + TILE] * cw[1:2]
            + z[c0 + 1:c0 + 1 + TILE] * cw[2:3])
    o_ref[0, :, 4 * LANES:6 * LANES] = (dc[:, 0:gw] * conv).astype(BF16)


def _local_mix_call(sink, bq, bkv, cq, ckv, ctab, hd, conv_w):
    nb, rows, _ = bq.shape
    nt = rows // TILE
    hb = TILE // SWA_WINDOW
    h16 = TILE // HALO_BF16
    cur = lambda w: pl.BlockSpec((1, TILE, w), lambda b, j: (b, j, 0))
    prev = lambda w: pl.BlockSpec((1, TILE, w), lambda b, j: (b, jnp.maximum(j - 1, 0), 0))
    nxt = lambda w: pl.BlockSpec((1, TILE, w), lambda b, j: (b, jnp.minimum(j + 1, nt - 1), 0))
    ctx = lambda w: pl.BlockSpec((1, TILE, w), lambda b, j: (b, 0, 0))
    return pl.pallas_call(
        _local_mix_kernel,
        grid=(nb, nt),
        in_specs=[
            pl.BlockSpec(memory_space=pltpu.SMEM),
            cur(256),
            pl.BlockSpec((1, SWA_WINDOW, 256), lambda b, j: (b, jnp.maximum(j * hb - 1, 0), 0)),
            cur(256),
            pl.BlockSpec((1, SWA_WINDOW, 256), lambda b, j: (b, jnp.minimum((j + 1) * hb, nt * hb - 1), 0)),
            ctx(256),
            cur(256), prev(512), cur(512), nxt(512), ctx(512),
            _resident((NA_HEADS, TILE, 3 * TILE)),
            pl.BlockSpec((1, HALO_BF16, 768), lambda b, j: (b, jnp.maximum(j * h16 - 1, 0), 0)),
            cur(768),
            pl.BlockSpec((1, HALO_BF16, 768), lambda b, j: (b, jnp.minimum((j + 1) * h16, nt * h16 - 1), 0)),
            _resident((3, GROUP_WIDTH)),
        ],
        out_specs=pl.BlockSpec((1, TILE, 768), lambda b, j: (b, j, 0)),
        out_shape=jax.ShapeDtypeStruct((nb, rows, 768), BF16),
        compiler_params=_params(2),
        name="local_mix",
    )(sink, bq, bkv, bkv, bkv, bkv, cq, ckv, ckv, ckv, ckv, ctab, hd, hd, hd, conv_w)


FFN_CHUNK = 1408
EXT = TILE + 2 * HALO_F32


def _halo_f32(prev_ref, next_ref):
    p = prev_ref[0].astype(F32)
    n = next_ref[0].astype(F32)
    return p[p.shape[0] - HALO_F32:], n[:HALO_F32]


def _out_ffn_kernel(final, xp_ref, xc_ref, xn_ref, ap_ref, ac_ref, an_ref, mp_ref, mc_ref, mn_ref,
                    mod_ref, wo_ref, nw_ref, wu_ref, cw_ref, cb_ref, wd_ref, fw_ref, o_ref, u_ref):
    j = pl.program_id(1)
    n_tiles = pl.num_programs(1)
    mod = mod_ref[0]
    g1 = mod[:, 2 * D_MODEL:3 * D_MODEL]
    sh2 = mod[:, 3 * D_MODEL:4 * D_MODEL]
    sc2 = mod[:, 4 * D_MODEL:5 * D_MODEL]
    g2 = mod[:, 5 * D_MODEL:6 * D_MODEL]

    ap, an = _halo_f32(ap_ref, an_ref)
    mp, mn = _halo_f32(mp_ref, mn_ref)
    mix = jnp.concatenate([
        jnp.concatenate([ap, mp], axis=1),
        jnp.concatenate([ac_ref[0], mc_ref[0]], axis=1).astype(F32),
        jnp.concatenate([an, mn], axis=1)], axis=0).astype(BF16)
    x = jnp.concatenate([xp_ref[0], xc_ref[0], xn_ref[0]], axis=0)
    x1 = x + g1 * _dot(mix, wo_ref[...])
    f = _rmsnorm_mod(x1, nw_ref[...], sh2, sc2).astype(BF16)

    has_prev = (j > 1).astype(F32)
    has_next = ((j > 0) & (j < n_tiles - 1)).astype(F32)
    lo = HALO_F32 - 1
    hi = HALO_F32 + TILE
    y = None
    for c0 in range(0, D_FF, FFN_CHUNK):
        halves = []
        for base in (c0, D_FF + c0):
            cols = slice(base, base + FFN_CHUNK)
            u_ref[...] = _dot(f, wu_ref[:, cols])
            u_ref[lo:lo + 1, :] = u_ref[lo:lo + 1, :] * has_prev
            u_ref[hi:hi + 1, :] = u_ref[hi:hi + 1, :] * has_next
            cw = cw_ref[:, cols]
            halves.append(u_ref[pl.ds(lo, TILE), :] * cw[0:1]
                          + u_ref[pl.ds(lo + 1, TILE), :] * cw[1:2]
                          + u_ref[pl.ds(lo + 2, TILE), :] * cw[2:3] + cb_ref[:, cols])
        g, v = halves
        act = (g * (1.0 / (1.0 + jnp.exp(-g))) * v).astype(BF16)
        part = _dot(act, wd_ref[c0:c0 + FFN_CHUNK, :])
        y = part if y is None else y + part
    x2 = x1[HALO_F32:HALO_F32 + TILE] + g2 * y
    if final:
        ms = jnp.mean(x2 * x2, axis=-1, keepdims=True)
        x2 = x2 * lax.rsqrt(ms + EPS) * fw_ref[...]
    o_ref[0] = x2


def _out_ffn_call(xall, oa, obcd, mod, w_out, norm_w, w_up, conv_w, conv_b, w_down, final_w, final):
    nb, rows, _ = xall.shape
    nt = rows // TILE
    h8 = TILE // HALO_F32
    h16 = TILE // HALO_BF16

    def trio(w, halo, per_tile):
        return [
            pl.BlockSpec((1, halo, w), lambda b, j: (b, jnp.maximum(j * per_tile - 1, 0), 0)),
            pl.BlockSpec((1, TILE, w), lambda b, j: (b, j, 0)),
            pl.BlockSpec((1, halo, w), lambda b, j: (b, jnp.minimum((j + 1) * per_tile, nt * per_tile - 1), 0)),
        ]

    if final:
        out_spec = pl.BlockSpec((1, TILE, D_MODEL), lambda b, j: (b, jnp.maximum(j - 1, 0), 0))
        out_shape = jax.ShapeDtypeStruct((nb, rows - CTX_LEN, D_MODEL), F32)
    else:
        out_spec = pl.BlockSpec((1, TILE, D_MODEL), lambda b, j: (b, j, 0))
        out_shape = jax.ShapeDtypeStruct((nb, rows, D_MODEL), F32)
    return pl.pallas_call(
        functools.partial(_out_ffn_kernel, final),
        grid=(nb, nt),
        in_specs=[
            *trio(D_MODEL, HALO_F32, h8),
            *trio(256, HALO_BF16, h16),
            *trio(768, HALO_BF16, h16),
            _mod_spec(nb),
            _resident((D_MODEL, D_MODEL)),
            _resident((1, D_MODEL)),
            _resident((D_MODEL, 2 * D_FF)),
            _resident((3, 2 * D_FF)),
            _resident((1, 2 * D_FF)),
            _resident((D_FF, D_MODEL)),
            _resident((1, D_MODEL)),
        ],
        out_specs=out_spec,
        out_shape=out_shape,
        scratch_shapes=[pltpu.VMEM((EXT, FFN_CHUNK), F32)],
        compiler_params=_params(2),
        name="out_ffn_final" if final else "out_ffn",
    )(xall, xall, xall, oa, oa, oa, obcd, obcd, obcd, mod, w_out, norm_w, w_up, conv_w, conv_b,
      w_down, final_w)


def _rope_tables(n_lat, dim):
    quarter = dim // 4
    t = np.arange(n_lat)
    pos = np.stack([t // GRID_W, t % GRID_W], axis=1).astype(np.float32)
    freqs = jnp.asarray(ROPE_THETA, F32) ** (-(jnp.arange(quarter, dtype=F32) / quarter))
    lane = np.arange(LANES) % dim
    axis = lane // (dim // 2)
    fidx = lane % quarter
    sign = np.where((lane % (dim // 2)) < quarter, -1.0, 1.0).astype(np.float32)
    ang = jnp.asarray(pos)[:, axis] * freqs[fidx][None, :]
    cos = jnp.concatenate([jnp.ones((CTX_LEN, LANES), F32), jnp.cos(ang)], axis=0)
    sin = jnp.concatenate([jnp.zeros((CTX_LEN, LANES), F32), jnp.sin(ang) * sign[None, :]], axis=0)
    return cos, sin


def _na_bias_table(rpb):
    rpt = TILE // GRID_W
    rq = np.arange(rpt)[:, None, None, None]
    cq = np.arange(GRID_W)[None, :, None, None]
    krel = np.arange(3 * rpt)[None, None, :, None]
    ck = np.arange(GRID_W)[None, None, None, :]
    dr = np.clip(krel - rpt - rq + NA_ROWS - 1, 0, 2 * NA_ROWS - 2)
    dc = np.clip(ck - cq, -(NA_COLS - 1), NA_COLS - 1) + NA_COLS - 1
    col_start = np.clip(cq - NA_COLS // 2, 0, GRID_W - NA_COLS)
    col_ok = (ck >= col_start) & (ck < col_start + NA_COLS)
    shape = (rpt, GRID_W, 3 * rpt, GRID_W)
    dr, dc, col_ok = (np.broadcast_to(a, shape) for a in (dr, dc, col_ok))
    bias = rpb.astype(F32)[:, dr, dc]
    bias = jnp.where(col_ok[None], bias, NEG_INF)
    return bias.reshape(rpb.shape[0], TILE, 3 * TILE)


def _swa_head_perm():
    order = (0, 2, 1, 3)
    return np.concatenate([np.arange(h * HEAD_DIM, (h + 1) * HEAD_DIM) for h in order])


def kernel(x, c, ctx, c_ctx, norm_mix_w, norm_ffn_w, w_mod, b_mod, w_in, w_out, diff_lambda_q1, diff_lambda_k1, diff_lambda_q2, diff_lambda_k2, diff_subln_w, swa_sink, na_rpb, sconv_w, ffn_w_up, ffn_conv_w, ffn_conv_b, ffn_w_down, final_norm_w):
    nb, n_lat, d = x.shape
    depth = w_in.shape[0]
    assert d == D_MODEL and ctx.shape[1] == CTX_LEN and n_lat % TILE == 0 and nb < 8
    assert n_lat // GRID_W >= NA_ROWS

    xall = jnp.concatenate([ctx, x], axis=1)
    cin = jnp.concatenate([c, c_ctx[None, :], jnp.zeros((8 - nb - 1, d), F32)], axis=0)
    mods = _mod_call(cin, w_mod, b_mod)

    tabs = (*_rope_tables(n_lat, DIFF_QK_DIM), *_rope_tables(n_lat, HEAD_DIM))
    perm = _swa_head_perm()
    b_q0 = 3 * GROUP_WIDTH
    col_perm = np.arange(IN_COLS)
    col_perm[b_q0:b_q0 + GROUP_WIDTH] = b_q0 + perm
    row_perm = np.arange(D_MODEL)
    row_perm[GROUP_WIDTH:2 * GROUP_WIDTH] = GROUP_WIDTH + perm

    for l in range(depth):
        mod = mods[l].reshape(8, 1, 6 * D_MODEL)
        w_in_l = w_in[l][:, col_perm].astype(BF16)
        qa, ka, vat, bq, bkv, cq, ckv, hd = _in_proj_call(
            xall, mod, norm_mix_w[l][None, :], w_in_l, tabs)

        lambda_init = 0.8 - 0.6 * math.exp(-0.3 * l)
        lam_init = jnp.asarray([lambda_init, 1.0 - lambda_init], F32)
        lam_vecs = jnp.stack([diff_lambda_q1[l], diff_lambda_k1[l],
                              diff_lambda_q2[l], diff_lambda_k2[l]])
        subw = jnp.broadcast_to(diff_subln_w[l][:, None], (HEAD_DIM, QA_TILE))
        oa = _diff_attn_call(lam_init, lam_vecs, qa, ka, vat, subw)

        obcd = _local_mix_call(swa_sink[l], bq, bkv, cq, ckv, _na_bias_table(na_rpb[l]),
                               hd, sconv_w[l])

        xall = _out_ffn_call(
            xall, oa, obcd, mod, w_out[l][row_perm, :].astype(BF16), norm_ffn_w[l][None, :],
            ffn_w_up[l].astype(BF16), ffn_conv_w[l], ffn_conv_b[l][None, :],
            ffn_w_down[l].astype(BF16), final_norm_w[None, :], final=(l == depth - 1))
    return xall
```

```python
import functools
import math

import numpy as np
import jax
import jax.numpy as jnp
from jax import lax
from jax.experimental import pallas as pl
from jax.experimental.pallas import tpu as pltpu

D_MODEL = 1024
DEPTH = 4
GRID_W = 64
CTX_LEN = 256
HEAD_DIM = 64
GROUP_WIDTH = 256
DIFF_HEADS = 4
DIFF_QK_DIM = 32
SWA_HEADS = 4
SWA_KV_HEADS = 2
SWA_WINDOW = 128
NA_HEADS = 4
NA_ROWS = 8
NA_COLS = 16
D_FF = 2816
IN_COLS = 2816
ROPE_THETA = 10000.0
EPS = 1e-6
NEG_INF = -1e30
LOG2E = math.log2(math.e)

TILE = 256
HALO_F32 = 8
HALO_BF16 = 16
LANES = 128
VMEM_LIMIT = 56 * 1024 * 1024

F32 = jnp.float32
BF16 = jnp.bfloat16


def _params(n_grid):
    return pltpu.CompilerParams(
        dimension_semantics=("arbitrary",) * n_grid, vmem_limit_bytes=VMEM_LIMIT)


def _resident(shape):
    nd = len(shape)
    return pl.BlockSpec(shape, lambda *_: (0,) * nd, pipeline_mode=pl.Buffered(1))


def _mod_spec(n_batch):
    return pl.BlockSpec((1, 1, 6 * D_MODEL), lambda b, j: (jnp.where(j == 0, n_batch, b), 0, 0))


def _rmsnorm_mod(x, w, shift, scale):
    ms = jnp.mean(x * x, axis=-1, keepdims=True)
    return (x * lax.rsqrt(ms + EPS) * w) * (1.0 + scale) + shift


def _dot(a, b):
    return jnp.dot(a, b, preferred_element_type=F32)


def _dot_nt(a, b):
    return lax.dot_general(a, b, (((1,), (1,)), ((), ())), preferred_element_type=F32)


MOD_BLOCK = 1536


def _mod_kernel(c_ref, w_ref, b_ref, o_ref):
    c = c_ref[...]
    a = c * (1.0 / (1.0 + jnp.exp(-c)))
    o_ref[0] = _dot(a.astype(BF16), w_ref[0].astype(BF16)) + b_ref[0]


def _mod_call(cin, w_mod, b_mod):
    depth = w_mod.shape[0]
    n = 6 * D_MODEL
    return pl.pallas_call(
        _mod_kernel,
        grid=(depth, n // MOD_BLOCK),
        in_specs=[
            pl.BlockSpec((8, D_MODEL), lambda l, k: (0, 0)),
            pl.BlockSpec((1, D_MODEL, MOD_BLOCK), lambda l, k: (l, 0, k)),
            pl.BlockSpec((1, 1, MOD_BLOCK), lambda l, k: (l, 0, k)),
        ],
        out_specs=pl.BlockSpec((1, 8, MOD_BLOCK), lambda l, k: (l, 0, k)),
        out_shape=jax.ShapeDtypeStruct((depth, 8, n), F32),
        compiler_params=_params(2),
        name="mod_proj",
    )(cin, w_mod, b_mod.reshape(depth, 1, n))


def _rope(x, cos, sin, half):
    lane = lax.broadcasted_iota(jnp.int32, x.shape, 1)
    first = (lane % (2 * half)) < half
    partner = jnp.where(first, pltpu.roll(x, LANES - half, 1), pltpu.roll(x, half, 1))
    return x * cos + partner * sin


def _in_proj_kernel(x_ref, mod_ref, nw_ref, w_ref, cosa_ref, sina_ref, cosb_ref, sinb_ref,
                    qa_ref, ka_ref, vat_ref, bq_ref, bkv_ref, cq_ref, ckv_ref, hd_ref):
    mod = mod_ref[0]
    shift = mod[:, 0:D_MODEL]
    scale = mod[:, D_MODEL:2 * D_MODEL]
    z = _rmsnorm_mod(x_ref[0], nw_ref[...], shift, scale)
    h = _dot(z.astype(BF16), w_ref[...])

    def sec(k, n=1):
        return h[:, k * LANES:(k + n) * LANES]

    cosa, sina = cosa_ref[...], sina_ref[...]
    cosb, sinb = cosb_ref[...], sinb_ref[...]
    qa_scale = DIFF_QK_DIM ** -0.5 * LOG2E
    q_scale = HEAD_DIM ** -0.5
    ha = DIFF_QK_DIM // 4
    hb = HEAD_DIM // 4
    for t in range(2):
        qa_ref[0, :, t * LANES:(t + 1) * LANES] = (
            _rope(sec(t), cosa, sina, ha) * qa_scale).astype(BF16)
        ka_ref[0, :, t * LANES:(t + 1) * LANES] = _rope(sec(2 + t), cosa, sina, ha).astype(BF16)
    vat_ref[0, 0] = sec(4, 2).T.astype(BF16)
    for t in range(2):
        bq_ref[0, :, t * LANES:(t + 1) * LANES] = (
            _rope(sec(6 + t), cosb, sinb, hb) * q_scale).astype(BF16)
    bkv_ref[0, :, 0:LANES] = _rope(sec(8), cosb, sinb, hb).astype(BF16)
    bkv_ref[0, :, LANES:2 * LANES] = sec(9).astype(BF16)
    cq_ref[0] = (sec(10, 2) * q_scale).astype(BF16)
    ckv_ref[0] = sec(12, 4).astype(BF16)
    hd_ref[0] = sec(16, 6).astype(BF16)


def _in_proj_call(xall, mod, norm_w, w_in, tabs):
    nb, rows, _ = xall.shape
    nt = rows // TILE
    tile = lambda w: pl.BlockSpec((1, TILE, w), lambda b, j: (b, j, 0))
    tab = pl.BlockSpec((TILE, LANES), lambda b, j: (j, 0))
    sds = lambda w: jax.ShapeDtypeStruct((nb, rows, w), BF16)
    return pl.pallas_call(
        _in_proj_kernel,
        grid=(nb, nt),
        in_specs=[
            pl.BlockSpec((1, TILE, D_MODEL), lambda b, j: (b, j, 0)),
            _mod_spec(nb),
            _resident((1, D_MODEL)),
            _resident((D_MODEL, IN_COLS)),
            tab, tab, tab, tab,
        ],
        out_specs=[
            tile(256), tile(256),
            pl.BlockSpec((1, 1, 256, TILE), lambda b, j: (b, j, 0, 0)),
            tile(256), tile(256), tile(256), tile(512), tile(768),
        ],
        out_shape=[
            sds(256), sds(256),
            jax.ShapeDtypeStruct((nb, nt, 256, TILE), BF16),
            sds(256), sds(256), sds(256), sds(512), sds(768),
        ],
        compiler_params=_params(2),
        name="in_proj",
    )(xall, mod, norm_w, w_in, *tabs)


QA_TILE = 128
KV_BLOCK = 512
PV_ROWS = HEAD_DIM + HALO_BF16


def _diff_attn_kernel(li_ref, lv_ref, q_ref, k_ref, vt_ref, sw_ref, o_ref,
                      qst_ref, s_ref, mc_ref, m_ref, acc_ref):
    i = pl.program_id(1)
    q = q_ref[0].astype(F32)
    lane = lax.broadcasted_iota(jnp.int32, q.shape, 1) // DIFF_QK_DIM
    for h in range(DIFF_HEADS):
        qs = jnp.concatenate([jnp.where(lane == 2 * h + c, q, 0.0) for c in range(2)], axis=0)
        qst_ref[h] = qs.T.astype(BF16)
    n_cols = 2 * QA_TILE
    n_blk = (k_ref.shape[1] - CTX_LEN) // KV_BLOCK
    tiles_per_blk = KV_BLOCK // TILE

    def value_rows(vt, h):
        ones = jnp.ones((HALO_BF16, vt.shape[1]), BF16)
        return jnp.concatenate([vt[h * HEAD_DIM:(h + 1) * HEAD_DIM], ones], axis=0)

    k0 = k_ref[0, 0:CTX_LEN, :]
    vt0 = vt_ref[0, 0]
    for h in range(DIFF_HEADS):
        s = _dot(k0, qst_ref[h])
        m0 = jnp.max(s, axis=0, keepdims=True)
        acc_ref[h] = _dot(value_rows(vt0, h), jnp.exp2(s - m0).astype(BF16))
        m_ref[h] = m0

    def score_block(blk, slot):
        row0 = pl.multiple_of(CTX_LEN + (blk - 1) * KV_BLOCK, TILE)
        k = k_ref[0, pl.ds(row0, KV_BLOCK), :]
        for h in range(DIFF_HEADS):
            s = _dot(k, qst_ref[h])
            s_ref[slot, h] = s
            mc_ref[slot, h] = jnp.max(s, axis=0, keepdims=True)

    def accumulate_block(blk, slot):
        t0 = 1 + (blk - 1) * tiles_per_blk
        vt = jnp.concatenate([vt_ref[0, t0 + u] for u in range(tiles_per_blk)], axis=1)
        for h in range(DIFF_HEADS):
            m_old = m_ref[h]
            m_new = jnp.maximum(m_old, mc_ref[slot, h])
            p = jnp.exp2(s_ref[slot, h] - m_new).astype(BF16)
            acc_ref[h] = acc_ref[h] * jnp.exp2(m_old - m_new) + _dot(value_rows(vt, h), p)
            m_ref[h] = m_new

    @pl.when(i >= CTX_LEN // QA_TILE)
    def _():
        score_block(1, 0)

        def step(it, carry):
            b = 2 * it + 1
            score_block(b + 1, 1)
            accumulate_block(b, 0)
            score_block(b + 2, 0)
            accumulate_block(b + 1, 1)
            return carry

        lax.fori_loop(0, n_blk // 2 - 1, step, 0)
        score_block(n_blk, 1)
        accumulate_block(n_blk - 1, 0)
        accumulate_block(n_blk, 1)

    lv = lv_ref[...]
    lam = (jnp.exp(jnp.sum(lv[0:1] * lv[1:2], axis=1, keepdims=True))
           - jnp.exp(jnp.sum(lv[2:3] * lv[3:4], axis=1, keepdims=True)) + li_ref[0])
    out_scale = li_ref[1]
    outs = []
    for h in range(DIFF_HEADS):
        a = acc_ref[h]
        o1 = a[0:HEAD_DIM, 0:QA_TILE] / a[HEAD_DIM:HEAD_DIM + 1, 0:QA_TILE]
        o2 = a[0:HEAD_DIM, QA_TILE:n_cols] / a[HEAD_DIM:HEAD_DIM + 1, QA_TILE:n_cols]
        o = o1 - lam * o2
        ms = jnp.mean(o * o, axis=0, keepdims=True)
        outs.append(o * lax.rsqrt(ms + EPS) * sw_ref[...] * out_scale)
    o_ref[0] = jnp.concatenate(outs, axis=0).T.astype(BF16)


def _diff_attn_call(lam_init, lam_vecs, qa, ka, vat, subw):
    nb, rows, _ = qa.shape
    nt = rows // TILE
    return pl.pallas_call(
        _diff_attn_kernel,
        grid=(nb, rows // QA_TILE),
        in_specs=[
            pl.BlockSpec(memory_space=pltpu.SMEM),
            _resident((4, DIFF_QK_DIM)),
            pl.BlockSpec((1, QA_TILE, 256), lambda b, i: (b, i, 0)),
            pl.BlockSpec((1, rows, 256), lambda b, i: (b, 0, 0)),
            pl.BlockSpec((1, nt, 256, TILE), lambda b, i: (b, 0, 0, 0)),
            _resident((HEAD_DIM, QA_TILE)),
        ],
        out_specs=pl.BlockSpec((1, QA_TILE, 256), lambda b, i: (b, i, 0)),
        out_shape=jax.ShapeDtypeStruct((nb, rows, 256), BF16),
        scratch_shapes=[
            pltpu.VMEM((DIFF_HEADS, 256, 2 * QA_TILE), BF16),
            pltpu.VMEM((2, DIFF_HEADS, KV_BLOCK, 2 * QA_TILE), F32),
            pltpu.VMEM((2, DIFF_HEADS, 1, 2 * QA_TILE), F32),
            pltpu.VMEM((DIFF_HEADS, 1, 2 * QA_TILE), F32),
            pltpu.VMEM((DIFF_HEADS, PV_ROWS, 2 * QA_TILE), F32),
        ],
        compiler_params=_params(2),
        name="diff_attn",
    )(lam_init, lam_vecs, qa, ka, vat, subw)


def _softmax_pv(s_parts, v_parts, extra_logit=None):
    m = s_parts[0].max(axis=-1, keepdims=True)
    for s in s_parts[1:]:
        m = jnp.maximum(m, s.max(axis=-1, keepdims=True))
    if extra_logit is not None:
        m = jnp.maximum(m, extra_logit)
    l = jnp.exp(extra_logit - m) if extra_logit is not None else 0.0
    o = None
    for s, v in zip(s_parts, v_parts):
        p = jnp.exp(s - m)
        l = l + p.sum(axis=-1, keepdims=True)
        pv = _dot(p.astype(BF16), v)
        o = pv if o is None else o + pv
    return o / l


def _half_stack(q):
    lane = lax.broadcasted_iota(jnp.int32, q.shape, 1) // HEAD_DIM
    zero = jnp.zeros_like(q)
    return jnp.concatenate([jnp.where(lane == 0, q, zero), jnp.where(lane == 1, q, zero)], axis=0)


def _half_merge(o):
    rows = o.shape[0] // 2
    lane = lax.broadcasted_iota(jnp.int32, (rows, LANES), 1) // HEAD_DIM
    return jnp.where(lane == 0, o[:rows], o[rows:])


def _local_mix_kernel(sink_ref, bq_ref, bkvp_ref, bkvc_ref, bkvn_ref, bkvx_ref,
                      cq_ref, ckvp_ref, ckvc_ref, ckvn_ref, ckvx_ref, ctab_ref,
                      dp_ref, dc_ref, dn_ref, cw_ref, o_ref):
    j = pl.program_id(1)
    n_tiles = pl.num_programs(1)
    n_lat = (n_tiles - 1) * TILE
    half = SWA_WINDOW

    k_loc = jnp.concatenate([bkvp_ref[0, :, 0:LANES], bkvc_ref[0, :, 0:LANES],
                             bkvn_ref[0, :, 0:LANES]], axis=0)
    v_loc = jnp.concatenate([bkvp_ref[0, :, LANES:], bkvc_ref[0, :, LANES:],
                             bkvn_ref[0, :, LANES:]], axis=0)
    k_ctx = bkvx_ref[0, :, 0:LANES]
    v_ctx = bkvx_ref[0, :, LANES:]
    n_loc = TILE + 2 * half
    rq = lax.broadcasted_iota(jnp.int32, (TILE, n_loc), 0)
    kk = lax.broadcasted_iota(jnp.int32, (TILE, n_loc), 1)
    kpos = (j - 1) * TILE - half + kk
    band = (jnp.abs(kk - half - rq) <= SWA_WINDOW) & (kpos >= 0) & (kpos < n_lat) & (j > 0)
    band2 = jnp.concatenate([band, band], axis=0)
    for p in range(2):
        qs = _half_stack(bq_ref[0, :, p * LANES:(p + 1) * LANES])
        s_loc = jnp.where(band2, _dot_nt(qs, k_loc), NEG_INF)
        s_ctx = _dot_nt(qs, k_ctx)
        sink = jnp.concatenate([jnp.full((TILE, 1), sink_ref[p], F32),
                                jnp.full((TILE, 1), sink_ref[2 + p], F32)], axis=0)
        o = _softmax_pv([s_loc, s_ctx], [v_loc, v_ctx], sink)
        o_ref[0, :, p * LANES:(p + 1) * LANES] = _half_merge(o).astype(BF16)

    n_locc = 3 * TILE
    rows_per_tile = TILE // GRID_W
    n_rows = n_lat // GRID_W
    rqc = lax.broadcasted_iota(jnp.int32, (TILE, n_locc), 0) // GRID_W
    krc = lax.broadcasted_iota(jnp.int32, (TILE, n_locc), 1) // GRID_W
    r = (j - 1) * rows_per_tile + rqc
    kr = (j - 2) * rows_per_tile + krc
    rs = jnp.clip(r - NA_ROWS // 2, 0, n_rows - NA_ROWS)
    row_ok = (kr >= rs) & (kr < rs + NA_ROWS) & (j > 0)
    for p in range(2):
        ksl = slice(p * LANES, (p + 1) * LANES)
        vsl = slice(2 * LANES + p * LANES, 2 * LANES + (p + 1) * LANES)
        k_loc = jnp.concatenate([ckvp_ref[0, :, ksl], ckvc_ref[0, :, ksl], ckvn_ref[0, :, ksl]], axis=0)
        v_loc = jnp.concatenate([ckvp_ref[0, :, vsl], ckvc_ref[0, :, vsl], ckvn_ref[0, :, vsl]], axis=0)
        qs = _half_stack(cq_ref[0, :, ksl])
        s_raw = _dot_nt(qs, k_loc)
        s_loc = jnp.concatenate(
            [jnp.where(row_ok, s_raw[k * TILE:(k + 1) * TILE] + ctab_ref[2 * p + k], NEG_INF)
             for k in range(2)], axis=0)
        s_ctx = _dot_nt(qs, ckvx_ref[0, :, ksl])
        o = _softmax_pv([s_loc, s_ctx], [v_loc, ckvx_ref[0, :, vsl]])
        o_ref[0, :, (2 + p) * LANES:(3 + p) * LANES] = _half_merge(o).astype(BF16)

    has_prev = (j > 1).astype(F32)
    has_next = ((j > 0) & (j < n_tiles - 1)).astype(F32)
    dc = dc_ref[0].astype(F32)
    dp = dp_ref[0].astype(F32)[HALO_BF16 - HALO_F32:]
    dn = dn_ref[0].astype(F32)[:HALO_F32]
    gw = GROUP_WIDTH
    z = jnp.concatenate([dp[:, gw:2 * gw] * dp[:, 2 * gw:] * has_prev,
                         dc[:, gw:2 * gw] * dc[:, 2 * gw:],
                         dn[:, gw:2 * gw] * dn[:, 2 * gw:] * has_next], axis=0)
    cw = cw_ref[...]
    c0 = HALO_F32
    conv = (z[c0 - 1:c0 - 1 + TILE] * cw[0:1] + z[c0:c0 + TILE] * cw[1:2]
            + z[c0 + 1:c0 + 1 + TILE] * cw[2:3])
    o_ref[0, :, 4 * LANES:6 * LANES] = (dc[:, 0:gw] * conv).astype(BF16)


def _local_mix_call(sink, bq, bkv, cq, ckv, ctab, hd, conv_w):
    nb, rows, _ = bq.shape
    nt = rows // TILE
    hb = TILE // SWA_WINDOW
    h16 = TILE // HALO_BF16
    cur = lambda w: pl.BlockSpec((1, TILE, w), lambda b, j: (b, j, 0))
    prev = lambda w: pl.BlockSpec((1, TILE, w), lambda b, j: (b, jnp.maximum(j - 1, 0), 0))
    nxt = lambda w: pl.BlockSpec((1, TILE, w), lambda b, j: (b, jnp.minimum(j + 1, nt - 1), 0))
    ctx = lambda w: pl.BlockSpec((1, TILE, w), lambda b, j: (b, 0, 0))
    return pl.pallas_call(
        _local_mix_kernel,
        grid=(nb, nt),
        in_specs=[
            pl.BlockSpec(memory_space=pltpu.SMEM),
            cur(256),
            pl.BlockSpec((1, SWA_WINDOW, 256), lambda b, j: (b, jnp.maximum(j * hb - 1, 0), 0)),
            cur(256),
            pl.BlockSpec((1, SWA_WINDOW, 256), lambda b, j: (b, jnp.minimum((j + 1) * hb, nt * hb - 1), 0)),
            ctx(256),
            cur(256), prev(512), cur(512), nxt(512), ctx(512),
            _resident((NA_HEADS, TILE, 3 * TILE)),
            pl.BlockSpec((1, HALO_BF16, 768), lambda b, j: (b, jnp.maximum(j * h16 - 1, 0), 0)),
            cur(768),
            pl.BlockSpec((1, HALO_BF16, 768), lambda b, j: (b, jnp.minimum((j + 1) * h16, nt * h16 - 1), 0)),
            _resident((3, GROUP_WIDTH)),
        ],
        out_specs=pl.BlockSpec((1, TILE, 768), lambda b, j: (b, j, 0)),
        out_shape=jax.ShapeDtypeStruct((nb, rows, 768), BF16),
        compiler_params=_params(2),
        name="local_mix",
    )(sink, bq, bkv, bkv, bkv, bkv, cq, ckv, ckv, ckv, ckv, ctab, hd, hd, hd, conv_w)


MXU_WIDTH = 256
FFN_CHUNKS = ((0, 6 * MXU_WIDTH), (6 * MXU_WIDTH, D_FF))
FFN_CHUNK_MAX = max(hi - lo for lo, hi in FFN_CHUNKS)
EXT = TILE + 2 * HALO_F32


def _halo_f32(prev_ref, next_ref):
    p = prev_ref[0].astype(F32)
    n = next_ref[0].astype(F32)
    return p[p.shape[0] - HALO_F32:], n[:HALO_F32]


def _out_ffn_kernel(final, xp_ref, xc_ref, xn_ref, ap_ref, ac_ref, an_ref, mp_ref, mc_ref, mn_ref,
                    mod_ref, wo_ref, nw_ref, wu_ref, cw_ref, cb_ref, wd_ref, fw_ref, o_ref, u_ref):
    j = pl.program_id(1)
    n_tiles = pl.num_programs(1)
    mod = mod_ref[0]
    g1 = mod[:, 2 * D_MODEL:3 * D_MODEL]
    sh2 = mod[:, 3 * D_MODEL:4 * D_MODEL]
    sc2 = mod[:, 4 * D_MODEL:5 * D_MODEL]
    g2 = mod[:, 5 * D_MODEL:6 * D_MODEL]

    ap, an = _halo_f32(ap_ref, an_ref)
    mp, mn = _halo_f32(mp_ref, mn_ref)
    mix = jnp.concatenate([
        jnp.concatenate([ap, mp], axis=1),
        jnp.concatenate([ac_ref[0], mc_ref[0]], axis=1).astype(F32),
        jnp.concatenate([an, mn], axis=1)], axis=0).astype(BF16)
    x = jnp.concatenate([xp_ref[0], xc_ref[0], xn_ref[0]], axis=0)
    x1 = x + g1 * _dot(mix, wo_ref[...])
    f = _rmsnorm_mod(x1, nw_ref[...], sh2, sc2).astype(BF16)

    has_prev = (j > 1).astype(F32)
    has_next = ((j > 0) & (j < n_tiles - 1)).astype(F32)
    lo = HALO_F32 - 1
    hi = HALO_F32 + TILE
    y = None
    for c0, c1 in FFN_CHUNKS:
        n = c1 - c0
        halves = []
        for base in (c0, D_FF + c0):
            cols = slice(base, base + n)
            u_ref[:, 0:n] = _dot(f, wu_ref[:, cols])
            u_ref[lo:lo + 1, 0:n] = u_ref[lo:lo + 1, 0:n] * has_prev
            u_ref[hi:hi + 1, 0:n] = u_ref[hi:hi + 1, 0:n] * has_next
            cw = cw_ref[:, cols]
            halves.append(u_ref[pl.ds(lo, TILE), 0:n] * cw[0:1]
                          + u_ref[pl.ds(lo + 1, TILE), 0:n] * cw[1:2]
                          + u_ref[pl.ds(lo + 2, TILE), 0:n] * cw[2:3] + cb_ref[:, cols])
        g, v = halves
        act = (g * (1.0 / (1.0 + jnp.exp(-g))) * v).astype(BF16)
        part = _dot(act, wd_ref[c0:c1, :])
        y = part if y is None else y + part
    x2 = x1[HALO_F32:HALO_F32 + TILE] + g2 * y
    if final:
        ms = jnp.mean(x2 * x2, axis=-1, keepdims=True)
        x2 = x2 * lax.rsqrt(ms + EPS) * fw_ref[...]
    o_ref[0] = x2


def _out_ffn_call(xall, oa, obcd, mod, w_out, norm_w, w_up, conv_w, conv_b, w_down, final_w, final):
    nb, rows, _ = xall.shape
    nt = rows // TILE
    h8 = TILE // HALO_F32
    h16 = TILE // HALO_BF16

    def trio(w, halo, per_tile):
        return [
            pl.BlockSpec((1, halo, w), lambda b, j: (b, jnp.maximum(j * per_tile - 1, 0), 0)),
            pl.BlockSpec((1, TILE, w), lambda b, j: (b, j, 0)),
            pl.BlockSpec((1, halo, w), lambda b, j: (b, jnp.minimum((j + 1) * per_tile, nt * per_tile - 1), 0)),
        ]

    if final:
        out_spec = pl.BlockSpec((1, TILE, D_MODEL), lambda b, j: (b, jnp.maximum(j - 1, 0), 0))
        out_shape = jax.ShapeDtypeStruct((nb, rows - CTX_LEN, D_MODEL), F32)
    else:
        out_spec = pl.BlockSpec((1, TILE, D_MODEL), lambda b, j: (b, j, 0))
        out_shape = jax.ShapeDtypeStruct((nb, rows, D_MODEL), F32)
    return pl.pallas_call(
        functools.partial(_out_ffn_kernel, final),
        grid=(nb, nt),
        in_specs=[
            *trio(D_MODEL, HALO_F32, h8),
            *trio(256, HALO_BF16, h16),
            *trio(768, HALO_BF16, h16),
            _mod_spec(nb),
            _resident((D_MODEL, D_MODEL)),
            _resident((1, D_MODEL)),
            _resident((D_MODEL, 2 * D_FF)),
            _resident((3, 2 * D_FF)),
            _resident((1, 2 * D_FF)),
            _resident((D_FF, D_MODEL)),
            _resident((1, D_MODEL)),
        ],
        out_specs=out_spec,
        out_shape=out_shape,
        scratch_shapes=[pltpu.VMEM((EXT, FFN_CHUNK_MAX), F32)],
        compiler_params=_params(2),
        name="out_ffn_final" if final else "out_ffn",
    )(xall, xall, xall, oa, oa, oa, obcd, obcd, obcd, mod, w_out, norm_w, w_up, conv_w, conv_b,
      w_down, final_w)


def _rope_tables(n_lat, dim):
    quarter = dim // 4
    t = np.arange(n_lat)
    pos = np.stack([t // GRID_W, t % GRID_W], axis=1).astype(np.float32)
    freqs = jnp.asarray(ROPE_THETA, F32) ** (-(jnp.arange(quarter, dtype=F32) / quarter))
    lane = np.arange(LANES) % dim
    axis = lane // (dim // 2)
    fidx = lane % quarter
    sign = np.where((lane % (dim // 2)) < quarter, -1.0, 1.0).astype(np.float32)
    ang = jnp.asarray(pos)[:, axis] * freqs[fidx][None, :]
    cos = jnp.concatenate([jnp.ones((CTX_LEN, LANES), F32), jnp.cos(ang)], axis=0)
    sin = jnp.concatenate([jnp.zeros((CTX_LEN, LANES), F32), jnp.sin(ang) * sign[None, :]], axis=0)
    return cos, sin


def _na_bias_table(rpb):
    rpt = TILE // GRID_W
    n_heads, n_dr, n_dc = rpb.shape
    cq = np.arange(GRID_W)[:, None]
    ck = np.arange(GRID_W)[None, :]
    dc = np.clip(ck - cq, -(NA_COLS - 1), NA_COLS - 1) + NA_COLS - 1
    col_start = np.clip(cq - NA_COLS // 2, 0, GRID_W - NA_COLS)
    col_ok = (ck >= col_start) & (ck < col_start + NA_COLS)
    pick = dc[None, :, :] == np.arange(n_dc)[:, None, None]
    e = jnp.sum(jnp.where(pick[None, None], rpb.astype(F32)[:, :, :, None, None], 0.0), axis=2)
    e = jnp.where(col_ok[None, None], e, NEG_INF)
    blocks = []
    for rq in range(rpt):
        lo = NA_ROWS - 1 - rpt - rq
        blocks.append(e[:, lo:lo + 3 * rpt].transpose(0, 2, 1, 3))
    return jnp.stack(blocks, axis=1).reshape(n_heads, TILE, 3 * TILE)


def _swa_head_order(w, axis, start):
    def blk(lo, hi):
        return lax.slice_in_dim(w, lo, hi, axis=axis)
    heads = [blk(start + h * HEAD_DIM, start + (h + 1) * HEAD_DIM) for h in (0, 2, 1, 3)]
    return jnp.concatenate(
        [blk(0, start), *heads, blk(start + SWA_HEADS * HEAD_DIM, w.shape[axis])], axis=axis)


def kernel(x, c, ctx, c_ctx, norm_mix_w, norm_ffn_w, w_mod, b_mod, w_in, w_out, diff_lambda_q1, diff_lambda_k1, diff_lambda_q2, diff_lambda_k2, diff_subln_w, swa_sink, na_rpb, sconv_w, ffn_w_up, ffn_conv_w, ffn_conv_b, ffn_w_down, final_norm_w):
    nb, n_lat, d = x.shape
    depth = w_in.shape[0]
    assert d == D_MODEL and ctx.shape[1] == CTX_LEN and n_lat % TILE == 0 and nb < 8
    assert n_lat // GRID_W >= NA_ROWS and n_lat % (2 * KV_BLOCK) == 0

    xall = jnp.concatenate([ctx, x], axis=1)
    cin = jnp.concatenate([c, c_ctx[None, :], jnp.zeros((8 - nb - 1, d), F32)], axis=0)
    mods = _mod_call(cin, w_mod, b_mod)

    tabs = (*_rope_tables(n_lat, DIFF_QK_DIM), *_rope_tables(n_lat, HEAD_DIM))
    for l in range(depth):
        mod = mods[l].reshape(8, 1, 6 * D_MODEL)
        w_in_l = _swa_head_order(w_in[l], 1, 3 * GROUP_WIDTH).astype(BF16)
        w_out_l = _swa_head_order(w_out[l], 0, GROUP_WIDTH).astype(BF16)
        qa, ka, vat, bq, bkv, cq, ckv, hd = _in_proj_call(
            xall, mod, norm_mix_w[l][None, :], w_in_l, tabs)

        lambda_init = 0.8 - 0.6 * math.exp(-0.3 * l)
        lam_init = jnp.asarray([lambda_init, 1.0 - lambda_init], F32)
        lam_vecs = jnp.stack([diff_lambda_q1[l], diff_lambda_k1[l],
                              diff_lambda_q2[l], diff_lambda_k2[l]])
        subw = jnp.broadcast_to(diff_subln_w[l][:, None], (HEAD_DIM, QA_TILE))
        oa = _diff_attn_call(lam_init, lam_vecs, qa, ka, vat, subw)

        obcd = _local_mix_call(swa_sink[l], bq, bkv, cq, ckv, _na_bias_table(na_rpb[l]),
                               hd, sconv_w[l])

        xall = _out_ffn_call(
            xall, oa, obcd, mod, w_out_l, norm_ffn_w[l][None, :],
            ffn_w_up[l].astype(BF16), ffn_conv_w[l], ffn_conv_b[l][None, :],
            ffn_w_down[l].astype(BF16), final_norm_w[None, :], final=(l == depth - 1))
    return xall
```

```python
import functools
import math

import numpy as np
import jax
import jax.numpy as jnp
from jax import lax
from jax.experimental import pallas as pl
from jax.experimental.pallas import tpu as pltpu

D_MODEL = 1024
DEPTH = 4
GRID_W = 64
CTX_LEN = 256
HEAD_DIM = 64
GROUP_WIDTH = 256
DIFF_HEADS = 4
DIFF_QK_DIM = 32
SWA_HEADS = 4
SWA_KV_HEADS = 2
SWA_WINDOW = 128
NA_HEADS = 4
NA_ROWS = 8
NA_COLS = 16
D_FF = 2816
IN_COLS = 2816
ROPE_THETA = 10000.0
EPS = 1e-6
NEG_INF = -1e30
LOG2E = math.log2(math.e)

TILE = 256
HALO_F32 = 8
HALO_BF16 = 16
LANES = 128
VMEM_LIMIT = 56 * 1024 * 1024

F32 = jnp.float32
BF16 = jnp.bfloat16


def _params(n_grid):
    return pltpu.CompilerParams(
        dimension_semantics=("arbitrary",) * n_grid, vmem_limit_bytes=VMEM_LIMIT)


def _resident(shape):
    nd = len(shape)
    return pl.BlockSpec(shape, lambda *_: (0,) * nd, pipeline_mode=pl.Buffered(1))


def _mod_spec(n_batch):
    return pl.BlockSpec((1, 1, 6 * D_MODEL), lambda b, j: (jnp.where(j == 0, n_batch, b), 0, 0))


def _rmsnorm_mod(x, w, shift, scale):
    ms = jnp.mean(x * x, axis=-1, keepdims=True)
    return (x * lax.rsqrt(ms + EPS) * w) * (1.0 + scale) + shift


def _dot(a, b):
    return jnp.dot(a, b, preferred_element_type=F32)


def _dot_nt(a, b):
    return lax.dot_general(a, b, (((1,), (1,)), ((), ())), preferred_element_type=F32)


MOD_BLOCK = 1536


def _mod_kernel(c_ref, w_ref, b_ref, o_ref):
    c = c_ref[...]
    a = c * (1.0 / (1.0 + jnp.exp(-c)))
    o_ref[0] = _dot(a.astype(BF16), w_ref[0].astype(BF16)) + b_ref[0]


def _mod_call(cin, w_mod, b_mod):
    depth = w_mod.shape[0]
    n = 6 * D_MODEL
    return pl.pallas_call(
        _mod_kernel,
        grid=(depth, n // MOD_BLOCK),
        in_specs=[
            pl.BlockSpec((8, D_MODEL), lambda l, k: (0, 0)),
            pl.BlockSpec((1, D_MODEL, MOD_BLOCK), lambda l, k: (l, 0, k)),
            pl.BlockSpec((1, 1, MOD_BLOCK), lambda l, k: (l, 0, k)),
        ],
        out_specs=pl.BlockSpec((1, 8, MOD_BLOCK), lambda l, k: (l, 0, k)),
        out_shape=jax.ShapeDtypeStruct((depth, 8, n), F32),
        compiler_params=_params(2),
        name="mod_proj",
    )(cin, w_mod, b_mod.reshape(depth, 1, n))


def _rope(x, cos, sin, half):
    lane = lax.broadcasted_iota(jnp.int32, x.shape, 1)
    first = (lane % (2 * half)) < half
    partner = jnp.where(first, pltpu.roll(x, LANES - half, 1), pltpu.roll(x, half, 1))
    return x * cos + partner * sin


def _in_proj_kernel(x_ref, mod_ref, nw_ref, w_ref, cosa_ref, sina_ref, cosb_ref, sinb_ref,
                    qa_ref, ka_ref, vat_ref, bq_ref, bkv_ref, cq_ref, ckv_ref, hd_ref):
    mod = mod_ref[0]
    shift = mod[:, 0:D_MODEL]
    scale = mod[:, D_MODEL:2 * D_MODEL]
    z = _rmsnorm_mod(x_ref[0], nw_ref[...], shift, scale)
    h = _dot(z.astype(BF16), w_ref[...])

    def sec(k, n=1):
        return h[:, k * LANES:(k + n) * LANES]

    cosa, sina = cosa_ref[...], sina_ref[...]
    cosb, sinb = cosb_ref[...], sinb_ref[...]
    qa_scale = DIFF_QK_DIM ** -0.5 * LOG2E
    q_scale = HEAD_DIM ** -0.5
    ha = DIFF_QK_DIM // 4
    hb = HEAD_DIM // 4
    for t in range(2):
        qa_ref[0, :, t * LANES:(t + 1) * LANES] = (
            _rope(sec(t), cosa, sina, ha) * qa_scale).astype(BF16)
        ka_ref[0, :, t * LANES:(t + 1) * LANES] = _rope(sec(2 + t), cosa, sina, ha).astype(BF16)
    vat_ref[0, 0] = sec(4, 2).T.astype(BF16)
    for t in range(2):
        bq_ref[0, :, t * LANES:(t + 1) * LANES] = (
            _rope(sec(6 + t), cosb, sinb, hb) * q_scale).astype(BF16)
    bkv_ref[0, :, 0:LANES] = _rope(sec(8), cosb, sinb, hb).astype(BF16)
    bkv_ref[0, :, LANES:2 * LANES] = sec(9).astype(BF16)
    cq_ref[0] = (sec(10, 2) * q_scale).astype(BF16)
    ckv_ref[0] = sec(12, 4).astype(BF16)
    hd_ref[0] = sec(16, 6).astype(BF16)


def _in_proj_call(xall, mod, norm_w, w_in, tabs):
    nb, rows, _ = xall.shape
    nt = rows // TILE
    tile = lambda w: pl.BlockSpec((1, TILE, w), lambda b, j: (b, j, 0))
    tab = pl.BlockSpec((TILE, LANES), lambda b, j: (j, 0))
    sds = lambda w: jax.ShapeDtypeStruct((nb, rows, w), BF16)
    return pl.pallas_call(
        _in_proj_kernel,
        grid=(nb, nt),
        in_specs=[
            pl.BlockSpec((1, TILE, D_MODEL), lambda b, j: (b, j, 0)),
            _mod_spec(nb),
            _resident((1, D_MODEL)),
            _resident((D_MODEL, IN_COLS)),
            tab, tab, tab, tab,
        ],
        out_specs=[
            tile(256), tile(256),
            pl.BlockSpec((1, 1, 256, TILE), lambda b, j: (b, j, 0, 0)),
            tile(256), tile(256), tile(256), tile(512), tile(768),
        ],
        out_shape=[
            sds(256), sds(256),
            jax.ShapeDtypeStruct((nb, nt, 256, TILE), BF16),
            sds(256), sds(256), sds(256), sds(512), sds(768),
        ],
        compiler_params=_params(2),
        name="in_proj",
    )(xall, mod, norm_w, w_in, *tabs)


QA_TILE = 256
KV_BLOCK = 512
PV_ROWS = HEAD_DIM + HALO_BF16


def _diff_attn_kernel(li_ref, lv_ref, q_ref, k_ref, vt_ref, sw_ref, o_ref,
                      qst_ref, sa_ref, sb_ref, mca_ref, mcb_ref, m_ref, acc_ref):
    i = pl.program_id(1)
    s_refs = (sa_ref, sb_ref)
    mc_refs = (mca_ref, mcb_ref)
    q = q_ref[0].astype(F32)
    lane = lax.broadcasted_iota(jnp.int32, q.shape, 1) // DIFF_QK_DIM
    for h in range(DIFF_HEADS):
        qs = jnp.concatenate([jnp.where(lane == 2 * h + c, q, 0.0) for c in range(2)], axis=0)
        qst_ref[h] = qs.T.astype(BF16)
    n_cols = 2 * QA_TILE
    n_blk = (k_ref.shape[1] - CTX_LEN) // KV_BLOCK
    tiles_per_blk = KV_BLOCK // TILE

    def value_rows(vt, h):
        ones = jnp.ones((HALO_BF16, vt.shape[1]), BF16)
        return jnp.concatenate([vt[h * HEAD_DIM:(h + 1) * HEAD_DIM], ones], axis=0)

    def ctx_scores(h):
        return _dot(k_ref[0, 0:CTX_LEN, :], qst_ref[h])

    def ctx_init(h, s):
        m0 = jnp.max(s, axis=0, keepdims=True)
        acc_ref[h] = _dot(value_rows(vt_ref[0, 0], h), jnp.exp2(s - m0).astype(BF16))
        m_ref[h] = m0

    def score_head(blk, slot, h):
        row0 = pl.multiple_of(CTX_LEN + (blk - 1) * KV_BLOCK, TILE)
        s = _dot(k_ref[0, pl.ds(row0, KV_BLOCK), :], qst_ref[h])
        s_refs[slot][h] = s
        mc_refs[slot][h] = jnp.max(s, axis=0, keepdims=True)

    def block_step(score_blk, acc_blk):
        if acc_blk is not None:
            ablk, aslot = acc_blk
            t0 = 1 + (ablk - 1) * tiles_per_blk
            vt = jnp.concatenate([vt_ref[0, t0 + u] for u in range(tiles_per_blk)], axis=1)
        for h in range(DIFF_HEADS):
            if score_blk is not None:
                score_head(*score_blk, h)
            if acc_blk is not None:
                m_old = m_ref[h]
                m_new = jnp.maximum(m_old, mc_refs[aslot][h])
                p = jnp.exp2(s_refs[aslot][h] - m_new).astype(BF16)
                acc_ref[h] = acc_ref[h] * jnp.exp2(m_old - m_new) + _dot(value_rows(vt, h), p)
                m_ref[h] = m_new

    @pl.when(i < CTX_LEN // QA_TILE)
    def _():
        for h in range(DIFF_HEADS):
            ctx_init(h, ctx_scores(h))

    @pl.when(i >= CTX_LEN // QA_TILE)
    def _():
        scores0 = [ctx_scores(h) for h in range(DIFF_HEADS)]
        for h in range(DIFF_HEADS):
            score_head(1, 0, h)
            ctx_init(h, scores0[h])

        def step(it, carry):
            b = 2 * it + 1
            block_step((b + 1, 1), (b, 0))
            block_step((b + 2, 0), (b + 1, 1))
            return carry

        lax.fori_loop(0, n_blk // 2 - 1, step, 0)
        block_step((n_blk, 1), (n_blk - 1, 0))
        block_step(None, (n_blk, 1))

    lv = lv_ref[...]
    lam = (jnp.exp(jnp.sum(lv[0:1] * lv[1:2], axis=1, keepdims=True))
           - jnp.exp(jnp.sum(lv[2:3] * lv[3:4], axis=1, keepdims=True)) + li_ref[0])
    out_scale = li_ref[1]
    outs = []
    for h in range(DIFF_HEADS):
        a = acc_ref[h]
        o1 = a[0:HEAD_DIM, 0:QA_TILE] / a[HEAD_DIM:HEAD_DIM + 1, 0:QA_TILE]
        o2 = a[0:HEAD_DIM, QA_TILE:n_cols] / a[HEAD_DIM:HEAD_DIM + 1, QA_TILE:n_cols]
        o = o1 - lam * o2
        ms = jnp.mean(o * o, axis=0, keepdims=True)
        outs.append(o * lax.rsqrt(ms + EPS) * sw_ref[...] * out_scale)
    o_ref[0] = jnp.concatenate(outs, axis=0).T.astype(BF16)


def _diff_attn_call(lam_init, lam_vecs, qa, ka, vat, subw):
    nb, rows, _ = qa.shape
    nt = rows // TILE
    return pl.pallas_call(
        _diff_attn_kernel,
        grid=(nb, rows // QA_TILE),
        in_specs=[
            pl.BlockSpec(memory_space=pltpu.SMEM),
            _resident((4, DIFF_QK_DIM)),
            pl.BlockSpec((1, QA_TILE, 256), lambda b, i: (b, i, 0)),
            pl.BlockSpec((1, rows, 256), lambda b, i: (b, 0, 0)),
            pl.BlockSpec((1, nt, 256, TILE), lambda b, i: (b, 0, 0, 0)),
            _resident((HEAD_DIM, QA_TILE)),
        ],
        out_specs=pl.BlockSpec((1, QA_TILE, 256), lambda b, i: (b, i, 0)),
        out_shape=jax.ShapeDtypeStruct((nb, rows, 256), BF16),
        scratch_shapes=[
            pltpu.VMEM((DIFF_HEADS, 256, 2 * QA_TILE), BF16),
            pltpu.VMEM((DIFF_HEADS, KV_BLOCK, 2 * QA_TILE), F32),
            pltpu.VMEM((DIFF_HEADS, KV_BLOCK, 2 * QA_TILE), F32),
            pltpu.VMEM((DIFF_HEADS, 1, 2 * QA_TILE), F32),
            pltpu.VMEM((DIFF_HEADS, 1, 2 * QA_TILE), F32),
            pltpu.VMEM((DIFF_HEADS, 1, 2 * QA_TILE), F32),
            pltpu.VMEM((DIFF_HEADS, PV_ROWS, 2 * QA_TILE), F32),
        ],
        compiler_params=_params(2),
        name="diff_attn",
    )(lam_init, lam_vecs, qa, ka, vat, subw)


def _softmax_pv(s_parts, v_parts, extra_logit=None):
    m = s_parts[0].max(axis=-1, keepdims=True)
    for s in s_parts[1:]:
        m = jnp.maximum(m, s.max(axis=-1, keepdims=True))
    if extra_logit is not None:
        m = jnp.maximum(m, extra_logit)
    l = jnp.exp(extra_logit - m) if extra_logit is not None else 0.0
    o = None
    for s, v in zip(s_parts, v_parts):
        p = jnp.exp(s - m)
        l = l + p.sum(axis=-1, keepdims=True)
        pv = _dot(p.astype(BF16), v)
        o = pv if o is None else o + pv
    return o / l


def _half_stack(q):
    lane = lax.broadcasted_iota(jnp.int32, q.shape, 1) // HEAD_DIM
    zero = jnp.zeros_like(q)
    return jnp.concatenate([jnp.where(lane == 0, q, zero), jnp.where(lane == 1, q, zero)], axis=0)


def _half_merge(o):
    rows = o.shape[0] // 2
    lane = lax.broadcasted_iota(jnp.int32, (rows, LANES), 1) // HEAD_DIM
    return jnp.where(lane == 0, o[:rows], o[rows:])


def _local_mix_kernel(sink_ref, bq_ref, bkvp_ref, bkvc_ref, bkvn_ref, bkvx_ref,
                      cq_ref, ckvp_ref, ckvc_ref, ckvn_ref, ckvx_ref, ctab_ref,
                      dp_ref, dc_ref, dn_ref, cw_ref, o_ref):
    j = pl.program_id(1)
    n_tiles = pl.num_programs(1)
    n_lat = (n_tiles - 1) * TILE
    half = SWA_WINDOW

    k_loc = jnp.concatenate([bkvp_ref[0, :, 0:LANES], bkvc_ref[0, :, 0:LANES],
                             bkvn_ref[0, :, 0:LANES]], axis=0)
    v_loc = jnp.concatenate([bkvp_ref[0, :, LANES:], bkvc_ref[0, :, LANES:],
                             bkvn_ref[0, :, LANES:]], axis=0)
    k_ctx = bkvx_ref[0, :, 0:LANES]
    v_ctx = bkvx_ref[0, :, LANES:]
    n_loc = TILE + 2 * half
    rq = lax.broadcasted_iota(jnp.int32, (TILE, n_loc), 0)
    kk = lax.broadcasted_iota(jnp.int32, (TILE, n_loc), 1)
    kpos = (j - 1) * TILE - half + kk
    band = (jnp.abs(kk - half - rq) <= SWA_WINDOW) & (kpos >= 0) & (kpos < n_lat) & (j > 0)
    band2 = jnp.concatenate([band, band], axis=0)
    for p in range(2):
        qs = _half_stack(bq_ref[0, :, p * LANES:(p + 1) * LANES])
        s_loc = jnp.where(band2, _dot_nt(qs, k_loc), NEG_INF)
        s_ctx = _dot_nt(qs, k_ctx)
        sink = jnp.concatenate([jnp.full((TILE, 1), sink_ref[p], F32),
                                jnp.full((TILE, 1), sink_ref[2 + p], F32)], axis=0)
        o = _softmax_pv([s_loc, s_ctx], [v_loc, v_ctx], sink)
        o_ref[0, :, p * LANES:(p + 1) * LANES] = _half_merge(o).astype(BF16)

    n_locc = 3 * TILE
    rows_per_tile = TILE // GRID_W
    n_rows = n_lat // GRID_W
    rqc = lax.broadcasted_iota(jnp.int32, (TILE, n_locc), 0) // GRID_W
    krc = lax.broadcasted_iota(jnp.int32, (TILE, n_locc), 1) // GRID_W
    r = (j - 1) * rows_per_tile + rqc
    kr = (j - 2) * rows_per_tile + krc
    rs = jnp.clip(r - NA_ROWS // 2, 0, n_rows - NA_ROWS)
    row_ok = (kr >= rs) & (kr < rs + NA_ROWS) & (j > 0)
    for p in range(2):
        ksl = slice(p * LANES, (p + 1) * LANES)
        vsl = slice(2 * LANES + p * LANES, 2 * LANES + (p + 1) * LANES)
        k_loc = jnp.concatenate([ckvp_ref[0, :, ksl], ckvc_ref[0, :, ksl], ckvn_ref[0, :, ksl]], axis=0)
        v_loc = jnp.concatenate([ckvp_ref[0, :, vsl], ckvc_ref[0, :, vsl], ckvn_ref[0, :, vsl]], axis=0)
        qs = _half_stack(cq_ref[0, :, ksl])
        s_raw = _dot_nt(qs, k_loc)
        s_loc = jnp.concatenate(
            [jnp.where(row_ok, s_raw[k * TILE:(k + 1) * TILE] + ctab_ref[2 * p + k], NEG_INF)
             for k in range(2)], axis=0)
        s_ctx = _dot_nt(qs, ckvx_ref[0, :, ksl])
        o = _softmax_pv([s_loc, s_ctx], [v_loc, ckvx_ref[0, :, vsl]])
        o_ref[0, :, (2 + p) * LANES:(3 + p) * LANES] = _half_merge(o).astype(BF16)

    has_prev = (j > 1).astype(F32)
    has_next = ((j > 0) & (j < n_tiles - 1)).astype(F32)
    dc = dc_ref[0].astype(F32)
    dp = dp_ref[0].astype(F32)[HALO_BF16 - HALO_F32:]
    dn = dn_ref[0].astype(F32)[:HALO_F32]
    gw = GROUP_WIDTH
    z = jnp.concatenate([dp[:, gw:2 * gw] * dp[:, 2 * gw:] * has_prev,
                         dc[:, gw:2 * gw] * dc[:, 2 * gw:],
                         dn[:, gw:2 * gw] * dn[:, 2 * gw:] * has_next], axis=0)
    cw = cw_ref[...]
    c0 = HALO_F32
    conv = (z[c0 - 1:c0 - 1 + TILE] * cw[0:1] + z[c0:c0 + TILE] * cw[1:2]
            + z[c0 + 1:c0 + 1 + TILE] * cw[2:3])
    o_ref[0, :, 4 * LANES:6 * LANES] = (dc[:, 0:gw] * conv).astype(BF16)


def _local_mix_call(sink, bq, bkv, cq, ckv, ctab, hd, conv_w):
    nb, rows, _ = bq.shape
    nt = rows // TILE
    hb = TILE // SWA_WINDOW
    h16 = TILE // HALO_BF16
    cur = lambda w: pl.BlockSpec((1, TILE, w), lambda b, j: (b, j, 0))
    prev = lambda w: pl.BlockSpec((1, TILE, w), lambda b, j: (b, jnp.maximum(j - 1, 0), 0))
    nxt = lambda w: pl.BlockSpec((1, TILE, w), lambda b, j: (b, jnp.minimum(j + 1, nt - 1), 0))
    ctx = lambda w: pl.BlockSpec((1, TILE, w), lambda b, j: (b, 0, 0))
    return pl.pallas_call(
        _local_mix_kernel,
        grid=(nb, nt),
        in_specs=[
            pl.BlockSpec(memory_space=pltpu.SMEM),
            cur(256),
            pl.BlockSpec((1, SWA_WINDOW, 256), lambda b, j: (b, jnp.maximum(j * hb - 1, 0), 0)),
            cur(256),
            pl.BlockSpec((1, SWA_WINDOW, 256), lambda b, j: (b, jnp.minimum((j + 1) * hb, nt * hb - 1), 0)),
            ctx(256),
            cur(256), prev(512), cur(512), nxt(512), ctx(512),
            _resident((NA_HEADS, TILE, 3 * TILE)),
            pl.BlockSpec((1, HALO_BF16, 768), lambda b, j: (b, jnp.maximum(j * h16 - 1, 0), 0)),
            cur(768),
            pl.BlockSpec((1, HALO_BF16, 768), lambda b, j: (b, jnp.minimum((j + 1) * h16, nt * h16 - 1), 0)),
            _resident((3, GROUP_WIDTH)),
        ],
        out_specs=pl.BlockSpec((1, TILE, 768), lambda b, j: (b, j, 0)),
        out_shape=jax.ShapeDtypeStruct((nb, rows, 768), BF16),
        compiler_params=_params(2),
        name="local_mix",
    )(sink, bq, bkv, bkv, bkv, bkv, cq, ckv, ckv, ckv, ckv, ctab, hd, hd, hd, conv_w)


MXU_WIDTH = 256
FFN_CHUNKS = ((0, 6 * MXU_WIDTH), (6 * MXU_WIDTH, D_FF))
FFN_CHUNK_MAX = max(hi - lo for lo, hi in FFN_CHUNKS)
EXT = TILE + 2 * HALO_F32


def _halo_f32(prev_ref, next_ref):
    p = prev_ref[0].astype(F32)
    n = next_ref[0].astype(F32)
    return p[p.shape[0] - HALO_F32:], n[:HALO_F32]


def _out_ffn_kernel(final, xp_ref, xc_ref, xn_ref, ap_ref, ac_ref, an_ref, mp_ref, mc_ref, mn_ref,
                    mod_ref, wo_ref, nw_ref, wu_ref, cw_ref, cb_ref, wd_ref, fw_ref, o_ref, u_ref):
    j = pl.program_id(1)
    n_tiles = pl.num_programs(1)
    mod = mod_ref[0]
    g1 = mod[:, 2 * D_MODEL:3 * D_MODEL]
    sh2 = mod[:, 3 * D_MODEL:4 * D_MODEL]
    sc2 = mod[:, 4 * D_MODEL:5 * D_MODEL]
    g2 = mod[:, 5 * D_MODEL:6 * D_MODEL]

    ap, an = _halo_f32(ap_ref, an_ref)
    mp, mn = _halo_f32(mp_ref, mn_ref)
    mix = jnp.concatenate([
        jnp.concatenate([ap, mp], axis=1),
        jnp.concatenate([ac_ref[0], mc_ref[0]], axis=1).astype(F32),
        jnp.concatenate([an, mn], axis=1)], axis=0).astype(BF16)
    x = jnp.concatenate([xp_ref[0], xc_ref[0], xn_ref[0]], axis=0)
    x1 = x + g1 * _dot(mix, wo_ref[...])
    f = _rmsnorm_mod(x1, nw_ref[...], sh2, sc2).astype(BF16)

    has_prev = (j > 1).astype(F32)
    has_next = ((j > 0) & (j < n_tiles - 1)).astype(F32)
    lo = HALO_F32 - 1
    hi = HALO_F32 + TILE
    y = None
    for c0, c1 in FFN_CHUNKS:
        n = c1 - c0
        halves = []
        for base in (c0, D_FF + c0):
            cols = slice(base, base + n)
            u_ref[:, 0:n] = _dot(f, wu_ref[:, cols])
            u_ref[lo:lo + 1, 0:n] = u_ref[lo:lo + 1, 0:n] * has_prev
            u_ref[hi:hi + 1, 0:n] = u_ref[hi:hi + 1, 0:n] * has_next
            cw = cw_ref[:, cols]
            halves.append(u_ref[pl.ds(lo, TILE), 0:n] * cw[0:1]
                          + u_ref[pl.ds(lo + 1, TILE), 0:n] * cw[1:2]
                          + u_ref[pl.ds(lo + 2, TILE), 0:n] * cw[2:3] + cb_ref[:, cols])
        g, v = halves
        act = (g * (1.0 / (1.0 + jnp.exp(-g))) * v).astype(BF16)
        part = _dot(act, wd_ref[c0:c1, :])
        y = part if y is None else y + part
    x2 = x1[HALO_F32:HALO_F32 + TILE] + g2 * y
    if final:
        ms = jnp.mean(x2 * x2, axis=-1, keepdims=True)
        x2 = x2 * lax.rsqrt(ms + EPS) * fw_ref[...]
    o_ref[0] = x2


def _out_ffn_call(xall, oa, obcd, mod, w_out, norm_w, w_up, conv_w, conv_b, w_down, final_w, final):
    nb, rows, _ = xall.shape
    nt = rows // TILE
    h8 = TILE // HALO_F32
    h16 = TILE // HALO_BF16

    def trio(w, halo, per_tile):
        return [
            pl.BlockSpec((1, halo, w), lambda b, j: (b, jnp.maximum(j * per_tile - 1, 0), 0)),
            pl.BlockSpec((1, TILE, w), lambda b, j: (b, j, 0)),
            pl.BlockSpec((1, halo, w), lambda b, j: (b, jnp.minimum((j + 1) * per_tile, nt * per_tile - 1), 0)),
        ]

    if final:
        out_spec = pl.BlockSpec((1, TILE, D_MODEL), lambda b, j: (b, jnp.maximum(j - 1, 0), 0))
        out_shape = jax.ShapeDtypeStruct((nb, rows - CTX_LEN, D_MODEL), F32)
    else:
        out_spec = pl.BlockSpec((1, TILE, D_MODEL), lambda b, j: (b, j, 0))
        out_shape = jax.ShapeDtypeStruct((nb, rows, D_MODEL), F32)
    return pl.pallas_call(
        functools.partial(_out_ffn_kernel, final),
        grid=(nb, nt),
        in_specs=[
            *trio(D_MODEL, HALO_F32, h8),
            *trio(256, HALO_BF16, h16),
            *trio(768, HALO_BF16, h16),
            _mod_spec(nb),
            _resident((D_MODEL, D_MODEL)),
            _resident((1, D_MODEL)),
            _resident((D_MODEL, 2 * D_FF)),
            _resident((3, 2 * D_FF)),
            _resident((1, 2 * D_FF)),
            _resident((D_FF, D_MODEL)),
            _resident((1, D_MODEL)),
        ],
        out_specs=out_spec,
        out_shape=out_shape,
        scratch_shapes=[pltpu.VMEM((EXT, FFN_CHUNK_MAX), F32)],
        compiler_params=_params(2),
        name="out_ffn_final" if final else "out_ffn",
    )(xall, xall, xall, oa, oa, oa, obcd, obcd, obcd, mod, w_out, norm_w, w_up, conv_w, conv_b,
      w_down, final_w)


def _rope_tables(n_lat, dim):
    quarter = dim // 4
    t = np.arange(n_lat)
    pos = np.stack([t // GRID_W, t % GRID_W], axis=1).astype(np.float32)
    freqs = jnp.asarray(ROPE_THETA, F32) ** (-(jnp.arange(quarter, dtype=F32) / quarter))
    lane = np.arange(LANES) % dim
    axis = lane // (dim // 2)
    fidx = lane % quarter
    sign = np.where((lane % (dim // 2)) < quarter, -1.0, 1.0).astype(np.float32)
    ang = jnp.asarray(pos)[:, axis] * freqs[fidx][None, :]
    cos = jnp.concatenate([jnp.ones((CTX_LEN, LANES), F32), jnp.cos(ang)], axis=0)
    sin = jnp.concatenate([jnp.zeros((CTX_LEN, LANES), F32), jnp.sin(ang) * sign[None, :]], axis=0)
    return cos, sin


def _na_bias_table(rpb):
    rpt = TILE // GRID_W
    n_heads, n_dr, n_dc = rpb.shape
    cq = np.arange(GRID_W)[:, None]
    ck = np.arange(GRID_W)[None, :]
    dc = np.clip(ck - cq, -(NA_COLS - 1), NA_COLS - 1) + NA_COLS - 1
    col_start = np.clip(cq - NA_COLS // 2, 0, GRID_W - NA_COLS)
    col_ok = (ck >= col_start) & (ck < col_start + NA_COLS)
    pick = dc[None, :, :] == np.arange(n_dc)[:, None, None]
    e = jnp.sum(jnp.where(pick[None, None], rpb.astype(F32)[:, :, :, None, None], 0.0), axis=2)
    e = jnp.where(col_ok[None, None], e, NEG_INF)
    blocks = []
    for rq in range(rpt):
        lo = NA_ROWS - 1 - rpt - rq
        blocks.append(e[:, lo:lo + 3 * rpt].transpose(0, 2, 1, 3))
    return jnp.stack(blocks, axis=1).reshape(n_heads, TILE, 3 * TILE)


def _swa_head_order(w, axis, start):
    def blk(lo, hi):
        return lax.slice_in_dim(w, lo, hi, axis=axis)
    heads = [blk(start + h * HEAD_DIM, start + (h + 1) * HEAD_DIM) for h in (0, 2, 1, 3)]
    return jnp.concatenate(
        [blk(0, start), *heads, blk(start + SWA_HEADS * HEAD_DIM, w.shape[axis])], axis=axis)


def kernel(x, c, ctx, c_ctx, norm_mix_w, norm_ffn_w, w_mod, b_mod, w_in, w_out, diff_lambda_q1, diff_lambda_k1, diff_lambda_q2, diff_lambda_k2, diff_subln_w, swa_sink, na_rpb, sconv_w, ffn_w_up, ffn_conv_w, ffn_conv_b, ffn_w_down, final_norm_w):
    nb, n_lat, d = x.shape
    depth = w_in.shape[0]
    assert d == D_MODEL and ctx.shape[1] == CTX_LEN and n_lat % TILE == 0 and nb < 8
    assert n_lat // GRID_W >= NA_ROWS and n_lat % (2 * KV_BLOCK) == 0

    xall = jnp.concatenate([ctx, x], axis=1)
    cin = jnp.concatenate([c, c_ctx[None, :], jnp.zeros((8 - nb - 1, d), F32)], axis=0)
    mods = _mod_call(cin, w_mod, b_mod)

    tabs = (*_rope_tables(n_lat, DIFF_QK_DIM), *_rope_tables(n_lat, HEAD_DIM))
    for l in range(depth):
        mod = mods[l].reshape(8, 1, 6 * D_MODEL)
        w_in_l = _swa_head_order(w_in[l], 1, 3 * GROUP_WIDTH).astype(BF16)
        w_out_l = _swa_head_order(w_out[l], 0, GROUP_WIDTH).astype(BF16)
        qa, ka, vat, bq, bkv, cq, ckv, hd = _in_proj_call(
            xall, mod, norm_mix_w[l][None, :], w_in_l, tabs)

        lambda_init = 0.8 - 0.6 * math.exp(-0.3 * l)
        lam_init = jnp.asarray([lambda_init, 1.0 - lambda_init], F32)
        lam_vecs = jnp.stack([diff_lambda_q1[l], diff_lambda_k1[l],
                              diff_lambda_q2[l], diff_lambda_k2[l]])
        subw = jnp.broadcast_to(diff_subln_w[l][:, None], (HEAD_DIM, QA_TILE))
        oa = _diff_attn_call(lam_init, lam_vecs, qa, ka, vat, subw)

        obcd = _local_mix_call(swa_sink[l], bq, bkv, cq, ckv, _na_bias_table(na_rpb[l]),
                               hd, sconv_w[l])

        xall = _out_ffn_call(
            xall, oa, obcd, mod, w_out_l, norm_ffn_w[l][None, :],
            ffn_w_up[l].astype(BF16), ffn_conv_w[l], ffn_conv_b[l][None, :],
            ffn_w_down[l].astype(BF16), final_norm_w[None, :], final=(l == depth - 1))
    return xall
```

```python
import functools
import math

import numpy as np
import jax
import jax.numpy as jnp
from jax import lax
from jax.experimental import pallas as pl
from jax.experimental.pallas import tpu as pltpu

D_MODEL = 1024
DEPTH = 4
GRID_W = 64
CTX_LEN = 256
HEAD_DIM = 64
GROUP_WIDTH = 256
DIFF_HEADS = 4
DIFF_QK_DIM = 32
SWA_HEADS = 4
SWA_KV_HEADS = 2
SWA_WINDOW = 128
NA_HEADS = 4
NA_ROWS = 8
NA_COLS = 16
D_FF = 2816
IN_COLS = 2816
ROPE_THETA = 10000.0
EPS = 1e-6
NEG_INF = -1e30
LOG2E = math.log2(math.e)

TILE = 256
HALO_F32 = 8
HALO_BF16 = 16
LANES = 128
VMEM_LIMIT = 56 * 1024 * 1024

F32 = jnp.float32
BF16 = jnp.bfloat16


def _params(n_grid, flags=None):
    return pltpu.CompilerParams(
        dimension_semantics=("arbitrary",) * n_grid, vmem_limit_bytes=VMEM_LIMIT, flags=flags)


def _resident(shape):
    nd = len(shape)
    return pl.BlockSpec(shape, lambda *_: (0,) * nd, pipeline_mode=pl.Buffered(1))


def _mod_spec(n_batch):
    return pl.BlockSpec((1, 1, 6 * D_MODEL), lambda b, j: (jnp.where(j == 0, n_batch, b), 0, 0))


def _rmsnorm_mod(x, w, shift, scale):
    ms = jnp.mean(x * x, axis=-1, keepdims=True)
    return (x * lax.rsqrt(ms + EPS) * w) * (1.0 + scale) + shift


def _dot(a, b):
    return jnp.dot(a, b, preferred_element_type=F32)


def _dot_nt(a, b):
    return lax.dot_general(a, b, (((1,), (1,)), ((), ())), preferred_element_type=F32)


MOD_BLOCK = 1536


def _mod_kernel(c_ref, w_ref, b_ref, o_ref):
    c = c_ref[...]
    a = c * (1.0 / (1.0 + jnp.exp(-c)))
    o_ref[0] = _dot(a.astype(BF16), w_ref[0].astype(BF16)) + b_ref[0]


def _mod_call(cin, w_mod, b_mod):
    depth = w_mod.shape[0]
    n = 6 * D_MODEL
    return pl.pallas_call(
        _mod_kernel,
        grid=(depth, n // MOD_BLOCK),
        in_specs=[
            pl.BlockSpec((8, D_MODEL), lambda l, k: (0, 0)),
            pl.BlockSpec((1, D_MODEL, MOD_BLOCK), lambda l, k: (l, 0, k)),
            pl.BlockSpec((1, 1, MOD_BLOCK), lambda l, k: (l, 0, k)),
        ],
        out_specs=pl.BlockSpec((1, 8, MOD_BLOCK), lambda l, k: (l, 0, k)),
        out_shape=jax.ShapeDtypeStruct((depth, 8, n), F32),
        compiler_params=_params(2),
        name="mod_proj",
    )(cin, w_mod, b_mod.reshape(depth, 1, n))


def _rope(x, cos, sin, half):
    lane = lax.broadcasted_iota(jnp.int32, x.shape, 1)
    first = (lane % (2 * half)) < half
    partner = jnp.where(first, pltpu.roll(x, LANES - half, 1), pltpu.roll(x, half, 1))
    return x * cos + partner * sin


def _in_proj_kernel(x_ref, mod_ref, nw_ref, w_ref, cosa_ref, sina_ref, cosb_ref, sinb_ref,
                    qa_ref, ka_ref, vat_ref, bq_ref, bkv_ref, cq_ref, ckv_ref, hd_ref):
    mod = mod_ref[0]
    shift = mod[:, 0:D_MODEL]
    scale = mod[:, D_MODEL:2 * D_MODEL]
    z = _rmsnorm_mod(x_ref[0], nw_ref[...], shift, scale)
    h = _dot(z.astype(BF16), w_ref[...])

    def sec(k, n=1):
        return h[:, k * LANES:(k + n) * LANES]

    cosa, sina = cosa_ref[...], sina_ref[...]
    cosb, sinb = cosb_ref[...], sinb_ref[...]
    qa_scale = DIFF_QK_DIM ** -0.5 * LOG2E
    q_scale = HEAD_DIM ** -0.5
    ha = DIFF_QK_DIM // 4
    hb = HEAD_DIM // 4
    for t in range(2):
        qa_ref[0, :, t * LANES:(t + 1) * LANES] = (
            _rope(sec(t), cosa, sina, ha) * qa_scale).astype(BF16)
        ka_ref[0, :, t * LANES:(t + 1) * LANES] = _rope(sec(2 + t), cosa, sina, ha).astype(BF16)
    vat_ref[0, 0] = sec(4, 2).T.astype(BF16)
    for t in range(2):
        bq_ref[0, :, t * LANES:(t + 1) * LANES] = (
            _rope(sec(6 + t), cosb, sinb, hb) * q_scale).astype(BF16)
    bkv_ref[0, :, 0:LANES] = _rope(sec(8), cosb, sinb, hb).astype(BF16)
    bkv_ref[0, :, LANES:2 * LANES] = sec(9).astype(BF16)
    cq_ref[0] = (sec(10, 2) * q_scale).astype(BF16)
    ckv_ref[0] = sec(12, 4).astype(BF16)
    hd_ref[0] = sec(16, 6).astype(BF16)


def _in_proj_call(xall, mod, norm_w, w_in, tabs):
    nb, rows, _ = xall.shape
    nt = rows // TILE
    tile = lambda w: pl.BlockSpec((1, TILE, w), lambda b, j: (b, j, 0))
    tab = pl.BlockSpec((TILE, LANES), lambda b, j: (j, 0))
    sds = lambda w: jax.ShapeDtypeStruct((nb, rows, w), BF16)
    return pl.pallas_call(
        _in_proj_kernel,
        grid=(nb, nt),
        in_specs=[
            pl.BlockSpec((1, TILE, D_MODEL), lambda b, j: (b, j, 0)),
            _mod_spec(nb),
            _resident((1, D_MODEL)),
            _resident((D_MODEL, IN_COLS)),
            tab, tab, tab, tab,
        ],
        out_specs=[
            tile(256), tile(256),
            pl.BlockSpec((1, 1, 256, TILE), lambda b, j: (b, j, 0, 0)),
            tile(256), tile(256), tile(256), tile(512), tile(768),
        ],
        out_shape=[
            sds(256), sds(256),
            jax.ShapeDtypeStruct((nb, nt, 256, TILE), BF16),
            sds(256), sds(256), sds(256), sds(512), sds(768),
        ],
        compiler_params=_params(2),
        name="in_proj",
    )(xall, mod, norm_w, w_in, *tabs)


QA_TILE = 256
KV_BLOCK = 512
BLOCKS_PER_STEP = 8
LOOKAHEAD = 2
EXP_CAP = 100.0
PV_ROWS = HEAD_DIM + HALO_BF16


def _diff_attn_kernel(li_ref, lv_ref, q_ref, k_ref, vt_ref, sw_ref, o_ref,
                      qst_ref, sa_ref, sb_ref, mca_ref, mcb_ref, m_ref, acc_ref, gap_ref):
    i = pl.program_id(1)
    s_refs = (sa_ref, sb_ref)
    mc_refs = (mca_ref, mcb_ref)
    q = q_ref[0].astype(F32)
    lane = lax.broadcasted_iota(jnp.int32, q.shape, 1) // DIFF_QK_DIM
    for h in range(DIFF_HEADS):
        qs = jnp.concatenate([jnp.where(lane == 2 * h + c, q, 0.0) for c in range(2)], axis=0)
        qst_ref[h] = qs.T.astype(BF16)
    n_cols = 2 * QA_TILE
    n_blk = (k_ref.shape[1] - CTX_LEN) // KV_BLOCK
    tiles_per_blk = KV_BLOCK // TILE

    def value_rows(vt, h):
        ones = jnp.ones((HALO_BF16, vt.shape[1]), BF16)
        return jnp.concatenate([vt[h * HEAD_DIM:(h + 1) * HEAD_DIM], ones], axis=0)

    def ctx_scores(h):
        return _dot(k_ref[0, 0:CTX_LEN, :], qst_ref[h])

    def ctx_init(h, s):
        m0 = jnp.max(s, axis=0, keepdims=True)
        acc_ref[h] = _dot(value_rows(vt_ref[0, 0], h), jnp.exp2(s - m0).astype(BF16))
        m_ref[h] = m0

    def score_head(blk, slot, h):
        row0 = pl.multiple_of(CTX_LEN + (blk - 1) * KV_BLOCK, TILE)
        s = _dot(k_ref[0, pl.ds(row0, KV_BLOCK), :], qst_ref[h])
        s_refs[slot][h] = s
        mc_refs[slot][h] = jnp.max(s, axis=0, keepdims=True)

    def block_step(score_blk, acc_blk):
        if acc_blk is not None:
            ablk, aslot = acc_blk
            t0 = 1 + (ablk - 1) * tiles_per_blk
            vt = jnp.concatenate([vt_ref[0, t0 + u] for u in range(tiles_per_blk)], axis=1)
        for h in range(DIFF_HEADS):
            if score_blk is not None:
                score_head(*score_blk, h)
            if acc_blk is not None:
                m_old = m_ref[h]
                m_new = jnp.maximum(m_old, mc_refs[aslot][h])
                p = jnp.exp2(s_refs[aslot][h] - m_new).astype(BF16)
                acc_ref[h] = acc_ref[h] * jnp.exp2(m_old - m_new) + _dot(value_rows(vt, h), p)
                m_ref[h] = m_new

    @pl.when(i < CTX_LEN // QA_TILE)
    def _():
        for h in range(DIFF_HEADS):
            ctx_init(h, ctx_scores(h))

    def two_pass_latent():
        scores0 = [ctx_scores(h) for h in range(DIFF_HEADS)]
        for h in range(DIFF_HEADS):
            score_head(1, 0, h)
            ctx_init(h, scores0[h])

        def step(it, carry):
            b = 2 * it + 1
            block_step((b + 1, 1), (b, 0))
            block_step((b + 2, 0), (b + 1, 1))
            return carry

        lax.fori_loop(0, n_blk // 2 - 1, step, 0)
        block_step((n_blk, 1), (n_blk - 1, 0))
        block_step(None, (n_blk, 1))

    def one_pass_latent():
        scores0 = [ctx_scores(h) for h in range(DIFF_HEADS)]
        for h in range(DIFF_HEADS):
            ctx_init(h, scores0[h])
            gap_ref[h] = jnp.zeros(gap_ref.shape[1:], F32)

        def scores(unit):
            blk, h, c = unit
            row0 = pl.multiple_of(CTX_LEN + (blk - 1) * KV_BLOCK, TILE)
            return _dot(k_ref[0, pl.ds(row0, KV_BLOCK), :],
                        qst_ref[h, :, c * QA_TILE:(c + 1) * QA_TILE])

        def consume(unit, s, vt):
            _, h, c = unit
            cols = slice(c * QA_TILE, (c + 1) * QA_TILE)
            m_old = m_ref[h, :, cols]
            m_cur = jnp.max(s, axis=0, keepdims=True)
            pv = _dot(value_rows(vt, h), jnp.exp2(s - m_old).astype(BF16))
            m_new = jnp.maximum(m_old, m_cur)
            acc_ref[h, :, cols] = (acc_ref[h, :, cols] + pv) * jnp.exp2(m_old - m_new)
            m_ref[h, :, cols] = m_new
            gap_ref[h, :, cols] = jnp.maximum(gap_ref[h, :, cols], m_cur - m_old)

        def step(it, carry):
            b0 = it * BLOCKS_PER_STEP + 1
            units = [(b0 + u, h, c) for u in range(BLOCKS_PER_STEP)
                     for h in range(DIFF_HEADS) for c in range(2)]
            pending = [scores(u) for u in units[:LOOKAHEAD]]
            for n, unit in enumerate(units):
                if n + LOOKAHEAD < len(units):
                    pending.append(scores(units[n + LOOKAHEAD]))
                if unit[1] == 0 and unit[2] == 0:
                    t0 = 1 + (unit[0] - 1) * tiles_per_blk
                    vt = jnp.concatenate(
                        [vt_ref[0, t0 + u] for u in range(tiles_per_blk)], axis=1)
                consume(unit, pending.pop(0), vt)
            return carry

        lax.fori_loop(0, n_blk // BLOCKS_PER_STEP, step, 0)

    @pl.when(i >= CTX_LEN // QA_TILE)
    def _():
        one_pass_latent()

        @pl.when(jnp.max(gap_ref[...]) > EXP_CAP)
        def _():
            two_pass_latent()

    lv = lv_ref[...]
    lam = (jnp.exp(jnp.sum(lv[0:1] * lv[1:2], axis=1, keepdims=True))
           - jnp.exp(jnp.sum(lv[2:3] * lv[3:4], axis=1, keepdims=True)) + li_ref[0])
    out_scale = li_ref[1]
    outs = []
    for h in range(DIFF_HEADS):
        a = acc_ref[h]
        o1 = a[0:HEAD_DIM, 0:QA_TILE] / a[HEAD_DIM:HEAD_DIM + 1, 0:QA_TILE]
        o2 = a[0:HEAD_DIM, QA_TILE:n_cols] / a[HEAD_DIM:HEAD_DIM + 1, QA_TILE:n_cols]
        o = o1 - lam * o2
        ms = jnp.mean(o * o, axis=0, keepdims=True)
        outs.append(o * lax.rsqrt(ms + EPS) * sw_ref[...] * out_scale)
    o_ref[0] = jnp.concatenate(outs, axis=0).T.astype(BF16)


def _diff_attn_call(lam_init, lam_vecs, qa, ka, vat, subw):
    nb, rows, _ = qa.shape
    nt = rows // TILE
    return pl.pallas_call(
        _diff_attn_kernel,
        grid=(nb, rows // QA_TILE),
        in_specs=[
            pl.BlockSpec(memory_space=pltpu.SMEM),
            _resident((4, DIFF_QK_DIM)),
            pl.BlockSpec((1, QA_TILE, 256), lambda b, i: (b, i, 0)),
            pl.BlockSpec((1, rows, 256), lambda b, i: (b, 0, 0)),
            pl.BlockSpec((1, nt, 256, TILE), lambda b, i: (b, 0, 0, 0)),
            _resident((HEAD_DIM, QA_TILE)),
        ],
        out_specs=pl.BlockSpec((1, QA_TILE, 256), lambda b, i: (b, i, 0)),
        out_shape=jax.ShapeDtypeStruct((nb, rows, 256), BF16),
        scratch_shapes=[
            pltpu.VMEM((DIFF_HEADS, 256, 2 * QA_TILE), BF16),
            pltpu.VMEM((DIFF_HEADS, KV_BLOCK, 2 * QA_TILE), F32),
            pltpu.VMEM((DIFF_HEADS, KV_BLOCK, 2 * QA_TILE), F32),
            pltpu.VMEM((DIFF_HEADS, 1, 2 * QA_TILE), F32),
            pltpu.VMEM((DIFF_HEADS, 1, 2 * QA_TILE), F32),
            pltpu.VMEM((DIFF_HEADS, 1, 2 * QA_TILE), F32),
            pltpu.VMEM((DIFF_HEADS, PV_ROWS, 2 * QA_TILE), F32),
            pltpu.VMEM((DIFF_HEADS, 1, 2 * QA_TILE), F32),
        ],
        compiler_params=_params(2),
        name="diff_attn",
    )(lam_init, lam_vecs, qa, ka, vat, subw)


def _softmax_pv(s_parts, v_parts, extra_logit=None):
    m = s_parts[0].max(axis=-1, keepdims=True)
    for s in s_parts[1:]:
        m = jnp.maximum(m, s.max(axis=-1, keepdims=True))
    if extra_logit is not None:
        m = jnp.maximum(m, extra_logit)
    l = jnp.exp(extra_logit - m) if extra_logit is not None else 0.0
    o = None
    for s, v in zip(s_parts, v_parts):
        p = jnp.exp(s - m)
        l = l + p.sum(axis=-1, keepdims=True)
        pv = _dot(p.astype(BF16), v)
        o = pv if o is None else o + pv
    return o / l


def _half_stack(q):
    lane = lax.broadcasted_iota(jnp.int32, q.shape, 1) // HEAD_DIM
    zero = jnp.zeros_like(q)
    return jnp.concatenate([jnp.where(lane == 0, q, zero), jnp.where(lane == 1, q, zero)], axis=0)


def _half_merge(o):
    rows = o.shape[0] // 2
    lane = lax.broadcasted_iota(jnp.int32, (rows, LANES), 1) // HEAD_DIM
    return jnp.where(lane == 0, o[:rows], o[rows:])


def _local_mix_kernel(sink_ref, bq_ref, bkvp_ref, bkvc_ref, bkvn_ref, bkvx_ref,
                      cq_ref, ckvp_ref, ckvc_ref, ckvn_ref, ckvx_ref, ctab_ref,
                      dp_ref, dc_ref, dn_ref, cw_ref, o_ref):
    j = pl.program_id(1)
    n_tiles = pl.num_programs(1)
    n_lat = (n_tiles - 1) * TILE
    half = SWA_WINDOW

    k_loc = jnp.concatenate([bkvp_ref[0, :, 0:LANES], bkvc_ref[0, :, 0:LANES],
                             bkvn_ref[0, :, 0:LANES]], axis=0)
    v_loc = jnp.concatenate([bkvp_ref[0, :, LANES:], bkvc_ref[0, :, LANES:],
                             bkvn_ref[0, :, LANES:]], axis=0)
    k_ctx = bkvx_ref[0, :, 0:LANES]
    v_ctx = bkvx_ref[0, :, LANES:]
    n_loc = TILE + 2 * half
    rq = lax.broadcasted_iota(jnp.int32, (TILE, n_loc), 0)
    kk = lax.broadcasted_iota(jnp.int32, (TILE, n_loc), 1)
    kpos = (j - 1) * TILE - half + kk
    band = (jnp.abs(kk - half - rq) <= SWA_WINDOW) & (kpos >= 0) & (kpos < n_lat) & (j > 0)
    band2 = jnp.concatenate([band, band], axis=0)
    for p in range(2):
        qs = _half_stack(bq_ref[0, :, p * LANES:(p + 1) * LANES])
        s_loc = jnp.where(band2, _dot_nt(qs, k_loc), NEG_INF)
        s_ctx = _dot_nt(qs, k_ctx)
        sink = jnp.concatenate([jnp.full((TILE, 1), sink_ref[p], F32),
                                jnp.full((TILE, 1), sink_ref[2 + p], F32)], axis=0)
        o = _softmax_pv([s_loc, s_ctx], [v_loc, v_ctx], sink)
        o_ref[0, :, p * LANES:(p + 1) * LANES] = _half_merge(o).astype(BF16)

    n_locc = 3 * TILE
    rows_per_tile = TILE // GRID_W
    n_rows = n_lat // GRID_W
    rqc = lax.broadcasted_iota(jnp.int32, (TILE, n_locc), 0) // GRID_W
    krc = lax.broadcasted_iota(jnp.int32, (TILE, n_locc), 1) // GRID_W
    r = (j - 1) * rows_per_tile + rqc
    kr = (j - 2) * rows_per_tile + krc
    rs = jnp.clip(r - NA_ROWS // 2, 0, n_rows - NA_ROWS)
    row_ok = (kr >= rs) & (kr < rs + NA_ROWS) & (j > 0)
    for p in range(2):
        ksl = slice(p * LANES, (p + 1) * LANES)
        vsl = slice(2 * LANES + p * LANES, 2 * LANES + (p + 1) * LANES)
        k_loc = jnp.concatenate([ckvp_ref[0, :, ksl], ckvc_ref[0, :, ksl], ckvn_ref[0, :, ksl]], axis=0)
        v_loc = jnp.concatenate([ckvp_ref[0, :, vsl], ckvc_ref[0, :, vsl], ckvn_ref[0, :, vsl]], axis=0)
        qs = _half_stack(cq_ref[0, :, ksl])
        s_raw = _dot_nt(qs, k_loc)
        s_loc = jnp.concatenate(
            [jnp.where(row_ok, s_raw[k * TILE:(k + 1) * TILE] + ctab_ref[2 * p + k], NEG_INF)
             for k in range(2)], axis=0)
        s_ctx = _dot_nt(qs, ckvx_ref[0, :, ksl])
        o = _softmax_pv([s_loc, s_ctx], [v_loc, ckvx_ref[0, :, vsl]])
        o_ref[0, :, (2 + p) * LANES:(3 + p) * LANES] = _half_merge(o).astype(BF16)

    has_prev = (j > 1).astype(F32)
    has_next = ((j > 0) & (j < n_tiles - 1)).astype(F32)
    dc = dc_ref[0].astype(F32)
    dp = dp_ref[0].astype(F32)[HALO_BF16 - HALO_F32:]
    dn = dn_ref[0].astype(F32)[:HALO_F32]
    gw = GROUP_WIDTH
    z = jnp.concatenate([dp[:, gw:2 * gw] * dp[:, 2 * gw:] * has_prev,
                         dc[:, gw:2 * gw] * dc[:, 2 * gw:],
                         dn[:, gw:2 * gw] * dn[:, 2 * gw:] * has_next], axis=0)
    cw = cw_ref[...]
    c0 = HALO_F32
    conv = (z[c0 - 1:c0 - 1 + TILE] * cw[0:1] + z[c0:c0 + TILE] * cw[1:2]
            + z[c0 + 1:c0 + 1 + TILE] * cw[2:3])
    o_ref[0, :, 4 * LANES:6 * LANES] = (dc[:, 0:gw] * conv).astype(BF16)


def _local_mix_call(sink, bq, bkv, cq, ckv, ctab, hd, conv_w):
    nb, rows, _ = bq.shape
    nt = rows // TILE
    hb = TILE // SWA_WINDOW
    h16 = TILE // HALO_BF16
    cur = lambda w: pl.BlockSpec((1, TILE, w), lambda b, j: (b, j, 0))
    prev = lambda w: pl.BlockSpec((1, TILE, w), lambda b, j: (b, jnp.maximum(j - 1, 0), 0))
    nxt = lambda w: pl.BlockSpec((1, TILE, w), lambda b, j: (b, jnp.minimum(j + 1, nt - 1), 0))
    ctx = lambda w: pl.BlockSpec((1, TILE, w), lambda b, j: (b, 0, 0))
    return pl.pallas_call(
        _local_mix_kernel,
        grid=(nb, nt),
        in_specs=[
            pl.BlockSpec(memory_space=pltpu.SMEM),
            cur(256),
            pl.BlockSpec((1, SWA_WINDOW, 256), lambda b, j: (b, jnp.maximum(j * hb - 1, 0), 0)),
            cur(256),
            pl.BlockSpec((1, SWA_WINDOW, 256), lambda b, j: (b, jnp.minimum((j + 1) * hb, nt * hb - 1), 0)),
            ctx(256),
            cur(256), prev(512), cur(512), nxt(512), ctx(512),
            _resident((NA_HEADS, TILE, 3 * TILE)),
            pl.BlockSpec((1, HALO_BF16, 768), lambda b, j: (b, jnp.maximum(j * h16 - 1, 0), 0)),
            cur(768),
            pl.BlockSpec((1, HALO_BF16, 768), lambda b, j: (b, jnp.minimum((j + 1) * h16, nt * h16 - 1), 0)),
            _resident((3, GROUP_WIDTH)),
        ],
        out_specs=pl.BlockSpec((1, TILE, 768), lambda b, j: (b, j, 0)),
        out_shape=jax.ShapeDtypeStruct((nb, rows, 768), BF16),
        compiler_params=_params(2),
        name="local_mix",
    )(sink, bq, bkv, bkv, bkv, bkv, cq, ckv, ckv, ckv, ckv, ctab, hd, hd, hd, conv_w)


MXU_WIDTH = 256
FFN_CHUNKS = ((0, 6 * MXU_WIDTH), (6 * MXU_WIDTH, D_FF))
FFN_CHUNK_MAX = max(hi - lo for lo, hi in FFN_CHUNKS)
EXT = TILE + 2 * HALO_F32


def _halo_f32(prev_ref, next_ref):
    p = prev_ref[0].astype(F32)
    n = next_ref[0].astype(F32)
    return p[p.shape[0] - HALO_F32:], n[:HALO_F32]


def _out_ffn_kernel(final, xp_ref, xc_ref, xn_ref, ap_ref, ac_ref, an_ref, mp_ref, mc_ref, mn_ref,
                    mod_ref, wo_ref, nw_ref, wu_ref, cw_ref, cb_ref, wd_ref, fw_ref, o_ref, u_ref):
    j = pl.program_id(1)
    n_tiles = pl.num_programs(1)
    mod = mod_ref[0]
    g1 = mod[:, 2 * D_MODEL:3 * D_MODEL]
    sh2 = mod[:, 3 * D_MODEL:4 * D_MODEL]
    sc2 = mod[:, 4 * D_MODEL:5 * D_MODEL]
    g2 = mod[:, 5 * D_MODEL:6 * D_MODEL]

    ap, an = _halo_f32(ap_ref, an_ref)
    mp, mn = _halo_f32(mp_ref, mn_ref)
    mix = jnp.concatenate([
        jnp.concatenate([ap, mp], axis=1),
        jnp.concatenate([ac_ref[0], mc_ref[0]], axis=1).astype(F32),
        jnp.concatenate([an, mn], axis=1)], axis=0).astype(BF16)
    x = jnp.concatenate([xp_ref[0], xc_ref[0], xn_ref[0]], axis=0)
    x1 = x + g1 * _dot(mix, wo_ref[...])
    f = _rmsnorm_mod(x1, nw_ref[...], sh2, sc2).astype(BF16)

    has_prev = (j > 1).astype(F32)
    has_next = ((j > 0) & (j < n_tiles - 1)).astype(F32)
    lo = HALO_F32 - 1
    hi = HALO_F32 + TILE
    y = None
    for c0, c1 in FFN_CHUNKS:
        n = c1 - c0
        halves = []
        for base in (c0, D_FF + c0):
            cols = slice(base, base + n)
            u_ref[:, 0:n] = _dot(f, wu_ref[:, cols])
            u_ref[lo:lo + 1, 0:n] = u_ref[lo:lo + 1, 0:n] * has_prev
            u_ref[hi:hi + 1, 0:n] = u_ref[hi:hi + 1, 0:n] * has_next
            cw = cw_ref[:, cols]
            halves.append(u_ref[pl.ds(lo, TILE), 0:n] * cw[0:1]
                          + u_ref[pl.ds(lo + 1, TILE), 0:n] * cw[1:2]
                          + u_ref[pl.ds(lo + 2, TILE), 0:n] * cw[2:3] + cb_ref[:, cols])
        g, v = halves
        act = (g * (1.0 / (1.0 + jnp.exp(-g))) * v).astype(BF16)
        part = _dot(act, wd_ref[c0:c1, :])
        y = part if y is None else y + part
    x2 = x1[HALO_F32:HALO_F32 + TILE] + g2 * y
    if final:
        ms = jnp.mean(x2 * x2, axis=-1, keepdims=True)
        x2 = x2 * lax.rsqrt(ms + EPS) * fw_ref[...]
    o_ref[0] = x2


def _out_ffn_call(xall, oa, obcd, mod, w_out, norm_w, w_up, conv_w, conv_b, w_down, final_w, final):
    nb, rows, _ = xall.shape
    nt = rows // TILE
    h8 = TILE // HALO_F32
    h16 = TILE // HALO_BF16

    def trio(w, halo, per_tile):
        return [
            pl.BlockSpec((1, halo, w), lambda b, j: (b, jnp.maximum(j * per_tile - 1, 0), 0)),
            pl.BlockSpec((1, TILE, w), lambda b, j: (b, j, 0)),
            pl.BlockSpec((1, halo, w), lambda b, j: (b, jnp.minimum((j + 1) * per_tile, nt * per_tile - 1), 0)),
        ]

    if final:
        out_spec = pl.BlockSpec((1, TILE, D_MODEL), lambda b, j: (b, jnp.maximum(j - 1, 0), 0))
        out_shape = jax.ShapeDtypeStruct((nb, rows - CTX_LEN, D_MODEL), F32)
    else:
        out_spec = pl.BlockSpec((1, TILE, D_MODEL), lambda b, j: (b, j, 0))
        out_shape = jax.ShapeDtypeStruct((nb, rows, D_MODEL), F32)
    return pl.pallas_call(
        functools.partial(_out_ffn_kernel, final),
        grid=(nb, nt),
        in_specs=[
            *trio(D_MODEL, HALO_F32, h8),
            *trio(256, HALO_BF16, h16),
            *trio(768, HALO_BF16, h16),
            _mod_spec(nb),
            _resident((D_MODEL, D_MODEL)),
            _resident((1, D_MODEL)),
            _resident((D_MODEL, 2 * D_FF)),
            _resident((3, 2 * D_FF)),
            _resident((1, 2 * D_FF)),
            _resident((D_FF, D_MODEL)),
            _resident((1, D_MODEL)),
        ],
        out_specs=out_spec,
        out_shape=out_shape,
        scratch_shapes=[pltpu.VMEM((EXT, FFN_CHUNK_MAX), F32)],
        compiler_params=_params(2),
        name="out_ffn_final" if final else "out_ffn",
    )(xall, xall, xall, oa, oa, oa, obcd, obcd, obcd, mod, w_out, norm_w, w_up, conv_w, conv_b,
      w_down, final_w)


def _rope_tables(n_lat, dim):
    quarter = dim // 4
    t = np.arange(n_lat)
    pos = np.stack([t // GRID_W, t % GRID_W], axis=1).astype(np.float32)
    freqs = jnp.asarray(ROPE_THETA, F32) ** (-(jnp.arange(quarter, dtype=F32) / quarter))
    lane = np.arange(LANES) % dim
    axis = lane // (dim // 2)
    fidx = lane % quarter
    sign = np.where((lane % (dim // 2)) < quarter, -1.0, 1.0).astype(np.float32)
    ang = jnp.asarray(pos)[:, axis] * freqs[fidx][None, :]
    cos = jnp.concatenate([jnp.ones((CTX_LEN, LANES), F32), jnp.cos(ang)], axis=0)
    sin = jnp.concatenate([jnp.zeros((CTX_LEN, LANES), F32), jnp.sin(ang) * sign[None, :]], axis=0)
    return cos, sin


def _na_bias_table(rpb):
    rpt = TILE // GRID_W
    n_heads, n_dr, n_dc = rpb.shape
    cq = np.arange(GRID_W)[:, None]
    ck = np.arange(GRID_W)[None, :]
    dc = np.clip(ck - cq, -(NA_COLS - 1), NA_COLS - 1) + NA_COLS - 1
    col_start = np.clip(cq - NA_COLS // 2, 0, GRID_W - NA_COLS)
    col_ok = (ck >= col_start) & (ck < col_start + NA_COLS)
    pick = dc[None, :, :] == np.arange(n_dc)[:, None, None]
    e = jnp.sum(jnp.where(pick[None, None], rpb.astype(F32)[:, :, :, None, None], 0.0), axis=2)
    e = jnp.where(col_ok[None, None], e, NEG_INF)
    blocks = []
    for rq in range(rpt):
        lo = NA_ROWS - 1 - rpt - rq
        blocks.append(e[:, lo:lo + 3 * rpt].transpose(0, 2, 1, 3))
    return jnp.stack(blocks, axis=1).reshape(n_heads, TILE, 3 * TILE)


def _swa_head_order(w, axis, start):
    def blk(lo, hi):
        return lax.slice_in_dim(w, lo, hi, axis=axis)
    heads = [blk(start + h * HEAD_DIM, start + (h + 1) * HEAD_DIM) for h in (0, 2, 1, 3)]
    return jnp.concatenate(
        [blk(0, start), *heads, blk(start + SWA_HEADS * HEAD_DIM, w.shape[axis])], axis=axis)


def kernel(x, c, ctx, c_ctx, norm_mix_w, norm_ffn_w, w_mod, b_mod, w_in, w_out, diff_lambda_q1, diff_lambda_k1, diff_lambda_q2, diff_lambda_k2, diff_subln_w, swa_sink, na_rpb, sconv_w, ffn_w_up, ffn_conv_w, ffn_conv_b, ffn_w_down, final_norm_w):
    nb, n_lat, d = x.shape
    depth = w_in.shape[0]
    assert d == D_MODEL and ctx.shape[1] == CTX_LEN and n_lat % TILE == 0 and nb < 8
    assert n_lat // GRID_W >= NA_ROWS and n_lat % (max(2, BLOCKS_PER_STEP) * KV_BLOCK) == 0

    xall = jnp.concatenate([ctx, x], axis=1)
    cin = jnp.concatenate([c, c_ctx[None, :], jnp.zeros((8 - nb - 1, d), F32)], axis=0)
    mods = _mod_call(cin, w_mod, b_mod)

    tabs = (*_rope_tables(n_lat, DIFF_QK_DIM), *_rope_tables(n_lat, HEAD_DIM))
    for l in range(depth):
        mod = mods[l].reshape(8, 1, 6 * D_MODEL)
        w_in_l = _swa_head_order(w_in[l], 1, 3 * GROUP_WIDTH).astype(BF16)
        w_out_l = _swa_head_order(w_out[l], 0, GROUP_WIDTH).astype(BF16)
        qa, ka, vat, bq, bkv, cq, ckv, hd = _in_proj_call(
            xall, mod, norm_mix_w[l][None, :], w_in_l, tabs)

        lambda_init = 0.8 - 0.6 * math.exp(-0.3 * l)
        lam_init = jnp.asarray([lambda_init, 1.0 - lambda_init], F32)
        lam_vecs = jnp.stack([diff_lambda_q1[l], diff_lambda_k1[l],
                              diff_lambda_q2[l], diff_lambda_k2[l]])
        subw = jnp.broadcast_to(diff_subln_w[l][:, None], (HEAD_DIM, QA_TILE))
        oa = _diff_attn_call(lam_init, lam_vecs, qa, ka, vat, subw)

        obcd = _local_mix_call(swa_sink[l], bq, bkv, cq, ckv, _na_bias_table(na_rpb[l]),
                               hd, sconv_w[l])

        xall = _out_ffn_call(
            xall, oa, obcd, mod, w_out_l, norm_ffn_w[l][None, :],
            ffn_w_up[l].astype(BF16), ffn_conv_w[l], ffn_conv_b[l][None, :],
            ffn_w_down[l].astype(BF16), final_norm_w[None, :], final=(l == depth - 1))
    return xall
```

```python
import functools
import math

import numpy as np
import jax
import jax.numpy as jnp
from jax import lax
from jax.experimental import pallas as pl
from jax.experimental.pallas import tpu as pltpu

D_MODEL = 1024
DEPTH = 4
GRID_W = 64
CTX_LEN = 256
HEAD_DIM = 64
GROUP_WIDTH = 256
DIFF_HEADS = 4
DIFF_QK_DIM = 32
SWA_HEADS = 4
SWA_KV_HEADS = 2
SWA_WINDOW = 128
NA_HEADS = 4
NA_ROWS = 8
NA_COLS = 16
D_FF = 2816
IN_COLS = 2816
ROPE_THETA = 10000.0
EPS = 1e-6
NEG_INF = -1e30
LOG2E = math.log2(math.e)

TILE = 256
HALO_F32 = 8
HALO_BF16 = 16
LANES = 128
VMEM_LIMIT = 56 * 1024 * 1024

F32 = jnp.float32
BF16 = jnp.bfloat16


def _params(n_grid, flags=None):
    return pltpu.CompilerParams(
        dimension_semantics=("arbitrary",) * n_grid, vmem_limit_bytes=VMEM_LIMIT, flags=flags)


def _resident(shape):
    nd = len(shape)
    return pl.BlockSpec(shape, lambda *_: (0,) * nd, pipeline_mode=pl.Buffered(1))


def _mod_spec(n_batch):
    return pl.BlockSpec((1, 1, 6 * D_MODEL), lambda b, j: (jnp.where(j == 0, n_batch, b), 0, 0))


def _rmsnorm_mod(x, w, shift, scale):
    ms = jnp.mean(x * x, axis=-1, keepdims=True)
    return (x * lax.rsqrt(ms + EPS) * w) * (1.0 + scale) + shift


def _dot(a, b):
    return jnp.dot(a, b, preferred_element_type=F32)


def _dot_nt(a, b):
    return lax.dot_general(a, b, (((1,), (1,)), ((), ())), preferred_element_type=F32)


MOD_BLOCK = 1536


def _mod_kernel(c_ref, w_ref, b_ref, o_ref):
    c = c_ref[...]
    a = c * (1.0 / (1.0 + jnp.exp(-c)))
    o_ref[0] = _dot(a.astype(BF16), w_ref[0].astype(BF16)) + b_ref[0]


def _mod_call(cin, w_mod, b_mod):
    depth = w_mod.shape[0]
    n = 6 * D_MODEL
    return pl.pallas_call(
        _mod_kernel,
        grid=(depth, n // MOD_BLOCK),
        in_specs=[
            pl.BlockSpec((8, D_MODEL), lambda l, k: (0, 0)),
            pl.BlockSpec((1, D_MODEL, MOD_BLOCK), lambda l, k: (l, 0, k)),
            pl.BlockSpec((1, 1, MOD_BLOCK), lambda l, k: (l, 0, k)),
        ],
        out_specs=pl.BlockSpec((1, 8, MOD_BLOCK), lambda l, k: (l, 0, k)),
        out_shape=jax.ShapeDtypeStruct((depth, 8, n), F32),
        compiler_params=_params(2),
        name="mod_proj",
    )(cin, w_mod, b_mod.reshape(depth, 1, n))


def _rope(x, cos, sin, half):
    lane = lax.broadcasted_iota(jnp.int32, x.shape, 1)
    first = (lane % (2 * half)) < half
    partner = jnp.where(first, pltpu.roll(x, LANES - half, 1), pltpu.roll(x, half, 1))
    return x * cos + partner * sin


def _in_proj_kernel(x_ref, mod_ref, nw_ref, w_ref, cosa_ref, sina_ref, cosb_ref, sinb_ref,
                    qa_ref, ka_ref, vat_ref, bqt_ref, bk_ref, bvt_ref, cqt_ref, ck_ref, cvt_ref,
                    hd_ref):
    mod = mod_ref[0]
    shift = mod[:, 0:D_MODEL]
    scale = mod[:, D_MODEL:2 * D_MODEL]
    z = _rmsnorm_mod(x_ref[0], nw_ref[...], shift, scale)
    h = _dot(z.astype(BF16), w_ref[...])

    def sec(k, n=1):
        return h[:, k * LANES:(k + n) * LANES]

    cosa, sina = cosa_ref[...], sina_ref[...]
    cosb, sinb = cosb_ref[...], sinb_ref[...]
    qa_scale = DIFF_QK_DIM ** -0.5 * LOG2E
    q_scale = HEAD_DIM ** -0.5 * LOG2E
    ha = DIFF_QK_DIM // 4
    hb = HEAD_DIM // 4
    for t in range(2):
        qa_ref[0, :, t * LANES:(t + 1) * LANES] = (
            _rope(sec(t), cosa, sina, ha) * qa_scale).astype(BF16)
        ka_ref[0, :, t * LANES:(t + 1) * LANES] = _rope(sec(2 + t), cosa, sina, ha).astype(BF16)
    vat_ref[0, 0] = sec(4, 2).T.astype(BF16)
    for t in range(2):
        bqt_ref[0, 0, t * LANES:(t + 1) * LANES, :] = (
            _rope(sec(6 + t), cosb, sinb, hb) * q_scale).T.astype(BF16)
    bk_ref[0] = _rope(sec(8), cosb, sinb, hb).astype(BF16)
    bvt_ref[0, 0] = sec(9).T.astype(BF16)
    cqt_ref[0, 0] = (sec(10, 2) * q_scale).T.astype(BF16)
    ck_ref[0] = sec(12, 2).astype(BF16)
    cvt_ref[0, 0] = sec(14, 2).T.astype(BF16)
    hd_ref[0] = sec(16, 6).astype(BF16)


def _in_proj_call(xall, mod, norm_w, w_in, tabs):
    nb, rows, _ = xall.shape
    nt = rows // TILE
    tile = lambda w: pl.BlockSpec((1, TILE, w), lambda b, j: (b, j, 0))
    ttile = lambda w: pl.BlockSpec((1, 1, w, TILE), lambda b, j: (b, j, 0, 0))
    tab = pl.BlockSpec((TILE, LANES), lambda b, j: (j, 0))
    sds = lambda w: jax.ShapeDtypeStruct((nb, rows, w), BF16)
    tsds = lambda w: jax.ShapeDtypeStruct((nb, nt, w, TILE), BF16)
    return pl.pallas_call(
        _in_proj_kernel,
        grid=(nb, nt),
        in_specs=[
            pl.BlockSpec((1, TILE, D_MODEL), lambda b, j: (b, j, 0)),
            _mod_spec(nb),
            _resident((1, D_MODEL)),
            _resident((D_MODEL, IN_COLS)),
            tab, tab, tab, tab,
        ],
        out_specs=[
            tile(256), tile(256), ttile(256),
            ttile(256), tile(128), ttile(128),
            ttile(256), tile(256), ttile(256),
            tile(768),
        ],
        out_shape=[
            sds(256), sds(256), tsds(256),
            tsds(256), sds(128), tsds(128),
            tsds(256), sds(256), tsds(256),
            sds(768),
        ],
        compiler_params=_params(2),
        name="in_proj",
    )(xall, mod, norm_w, w_in, *tabs)


QA_TILE = 256
KV_BLOCK = 512
BLOCKS_PER_STEP = 8
LOOKAHEAD = 2
EXP_CAP = 100.0
PV_ROWS = HEAD_DIM + HALO_BF16


def _diff_attn_kernel(li_ref, lv_ref, q_ref, k_ref, vt_ref, sw_ref, o_ref,
                      qst_ref, sa_ref, sb_ref, mca_ref, mcb_ref, m_ref, acc_ref, gap_ref):
    i = pl.program_id(1)
    s_refs = (sa_ref, sb_ref)
    mc_refs = (mca_ref, mcb_ref)
    q = q_ref[0].astype(F32)
    lane = lax.broadcasted_iota(jnp.int32, q.shape, 1) // DIFF_QK_DIM
    for h in range(DIFF_HEADS):
        qs = jnp.concatenate([jnp.where(lane == 2 * h + c, q, 0.0) for c in range(2)], axis=0)
        qst_ref[h] = qs.T.astype(BF16)
    n_cols = 2 * QA_TILE
    n_blk = (k_ref.shape[1] - CTX_LEN) // KV_BLOCK
    tiles_per_blk = KV_BLOCK // TILE

    def value_rows(vt, h):
        ones = jnp.ones((HALO_BF16, vt.shape[1]), BF16)
        return jnp.concatenate([vt[h * HEAD_DIM:(h + 1) * HEAD_DIM], ones], axis=0)

    def ctx_scores(h):
        return _dot(k_ref[0, 0:CTX_LEN, :], qst_ref[h])

    def ctx_init(h, s):
        m0 = jnp.max(s, axis=0, keepdims=True)
        acc_ref[h] = _dot(value_rows(vt_ref[0, 0], h), jnp.exp2(s - m0).astype(BF16))
        m_ref[h] = m0

    def score_head(blk, slot, h):
        row0 = pl.multiple_of(CTX_LEN + (blk - 1) * KV_BLOCK, TILE)
        s = _dot(k_ref[0, pl.ds(row0, KV_BLOCK), :], qst_ref[h])
        s_refs[slot][h] = s
        mc_refs[slot][h] = jnp.max(s, axis=0, keepdims=True)

    def block_step(score_blk, acc_blk):
        if acc_blk is not None:
            ablk, aslot = acc_blk
            t0 = 1 + (ablk - 1) * tiles_per_blk
            vt = jnp.concatenate([vt_ref[0, t0 + u] for u in range(tiles_per_blk)], axis=1)
        for h in range(DIFF_HEADS):
            if score_blk is not None:
                score_head(*score_blk, h)
            if acc_blk is not None:
                m_old = m_ref[h]
                m_new = jnp.maximum(m_old, mc_refs[aslot][h])
                p = jnp.exp2(s_refs[aslot][h] - m_new).astype(BF16)
                acc_ref[h] = acc_ref[h] * jnp.exp2(m_old - m_new) + _dot(value_rows(vt, h), p)
                m_ref[h] = m_new

    @pl.when(i < CTX_LEN // QA_TILE)
    def _():
        for h in range(DIFF_HEADS):
            ctx_init(h, ctx_scores(h))

    def two_pass_latent():
        scores0 = [ctx_scores(h) for h in range(DIFF_HEADS)]
        for h in range(DIFF_HEADS):
            score_head(1, 0, h)
            ctx_init(h, scores0[h])

        def step(it, carry):
            b = 2 * it + 1
            block_step((b + 1, 1), (b, 0))
            block_step((b + 2, 0), (b + 1, 1))
            return carry

        lax.fori_loop(0, n_blk // 2 - 1, step, 0)
        block_step((n_blk, 1), (n_blk - 1, 0))
        block_step(None, (n_blk, 1))

    def one_pass_latent():
        scores0 = [ctx_scores(h) for h in range(DIFF_HEADS)]
        for h in range(DIFF_HEADS):
            ctx_init(h, scores0[h])
            gap_ref[h] = jnp.zeros(gap_ref.shape[1:], F32)

        def scores(unit):
            blk, h, c = unit
            row0 = pl.multiple_of(CTX_LEN + (blk - 1) * KV_BLOCK, TILE)
            return _dot(k_ref[0, pl.ds(row0, KV_BLOCK), :],
                        qst_ref[h, :, c * QA_TILE:(c + 1) * QA_TILE])

        def consume(unit, s, vt):
            _, h, c = unit
            cols = slice(c * QA_TILE, (c + 1) * QA_TILE)
            m_old = m_ref[h, :, cols]
            m_cur = jnp.max(s, axis=0, keepdims=True)
            pv = _dot(value_rows(vt, h), jnp.exp2(s - m_old).astype(BF16))
            m_new = jnp.maximum(m_old, m_cur)
            acc_ref[h, :, cols] = (acc_ref[h, :, cols] + pv) * jnp.exp2(m_old - m_new)
            m_ref[h, :, cols] = m_new
            gap_ref[h, :, cols] = jnp.maximum(gap_ref[h, :, cols], m_cur - m_old)

        def step(it, carry):
            b0 = it * BLOCKS_PER_STEP + 1
            units = [(b0 + u, h, c) for u in range(BLOCKS_PER_STEP)
                     for h in range(DIFF_HEADS) for c in range(2)]
            pending = [scores(u) for u in units[:LOOKAHEAD]]
            for n, unit in enumerate(units):
                if n + LOOKAHEAD < len(units):
                    pending.append(scores(units[n + LOOKAHEAD]))
                if unit[1] == 0 and unit[2] == 0:
                    t0 = 1 + (unit[0] - 1) * tiles_per_blk
                    vt = jnp.concatenate(
                        [vt_ref[0, t0 + u] for u in range(tiles_per_blk)], axis=1)
                consume(unit, pending.pop(0), vt)
            return carry

        lax.fori_loop(0, n_blk // BLOCKS_PER_STEP, step, 0)

    @pl.when(i >= CTX_LEN // QA_TILE)
    def _():
        one_pass_latent()

        @pl.when(jnp.max(gap_ref[...]) > EXP_CAP)
        def _():
            two_pass_latent()

    lv = lv_ref[...]
    lam = (jnp.exp(jnp.sum(lv[0:1] * lv[1:2], axis=1, keepdims=True))
           - jnp.exp(jnp.sum(lv[2:3] * lv[3:4], axis=1, keepdims=True)) + li_ref[0])
    out_scale = li_ref[1]
    outs = []
    for h in range(DIFF_HEADS):
        a = acc_ref[h]
        o1 = a[0:HEAD_DIM, 0:QA_TILE] / a[HEAD_DIM:HEAD_DIM + 1, 0:QA_TILE]
        o2 = a[0:HEAD_DIM, QA_TILE:n_cols] / a[HEAD_DIM:HEAD_DIM + 1, QA_TILE:n_cols]
        o = o1 - lam * o2
        ms = jnp.mean(o * o, axis=0, keepdims=True)
        outs.append(o * lax.rsqrt(ms + EPS) * sw_ref[...] * out_scale)
    o_ref[0] = jnp.concatenate(outs, axis=0).T.astype(BF16)


def _diff_attn_call(lam_init, lam_vecs, qa, ka, vat, subw):
    nb, rows, _ = qa.shape
    nt = rows // TILE
    return pl.pallas_call(
        _diff_attn_kernel,
        grid=(nb, rows // QA_TILE),
        in_specs=[
            pl.BlockSpec(memory_space=pltpu.SMEM),
            _resident((4, DIFF_QK_DIM)),
            pl.BlockSpec((1, QA_TILE, 256), lambda b, i: (b, i, 0)),
            pl.BlockSpec((1, rows, 256), lambda b, i: (b, 0, 0)),
            pl.BlockSpec((1, nt, 256, TILE), lambda b, i: (b, 0, 0, 0)),
            _resident((HEAD_DIM, QA_TILE)),
        ],
        out_specs=pl.BlockSpec((1, QA_TILE, 256), lambda b, i: (b, i, 0)),
        out_shape=jax.ShapeDtypeStruct((nb, rows, 256), BF16),
        scratch_shapes=[
            pltpu.VMEM((DIFF_HEADS, 256, 2 * QA_TILE), BF16),
            pltpu.VMEM((DIFF_HEADS, KV_BLOCK, 2 * QA_TILE), F32),
            pltpu.VMEM((DIFF_HEADS, KV_BLOCK, 2 * QA_TILE), F32),
            pltpu.VMEM((DIFF_HEADS, 1, 2 * QA_TILE), F32),
            pltpu.VMEM((DIFF_HEADS, 1, 2 * QA_TILE), F32),
            pltpu.VMEM((DIFF_HEADS, 1, 2 * QA_TILE), F32),
            pltpu.VMEM((DIFF_HEADS, PV_ROWS, 2 * QA_TILE), F32),
            pltpu.VMEM((DIFF_HEADS, 1, 2 * QA_TILE), F32),
        ],
        compiler_params=_params(2),
        name="diff_attn",
    )(lam_init, lam_vecs, qa, ka, vat, subw)


MIX_LOOKAHEAD = 2


def _head_rows(qt, k):
    row = lax.broadcasted_iota(jnp.int32, qt.shape, 0) // HEAD_DIM
    return jnp.where(row == k, qt, jnp.zeros_like(qt))


def _value_rows(vt_parts, rows):
    v = jnp.concatenate([part[rows] for part in vt_parts], axis=1)
    return jnp.concatenate([v, jnp.ones((HALO_BF16, v.shape[1]), BF16)], axis=0)


def _local_mix_kernel(sink_ref, bqt_ref, bkp_ref, bkc_ref, bkn_ref, bkx_ref,
                      bvtp_ref, bvtc_ref, bvtn_ref, bvtx_ref,
                      cqt_ref, ckp_ref, ckc_ref, ckn_ref, ckx_ref,
                      cvtp_ref, cvtc_ref, cvtn_ref, cvtx_ref, bmask_ref, ctab_ref,
                      dp_ref, dc_ref, dn_ref, cw_ref, o_ref, s_ref):
    j = pl.program_id(1)
    n_tiles = pl.num_programs(1)
    half = SWA_WINDOW

    k_b = jnp.concatenate([bkp_ref[0], bkc_ref[0], bkn_ref[0], bkx_ref[0]], axis=0)
    vt_b = [bvtp_ref[0, 0][:, half:], bvtc_ref[0, 0], bvtn_ref[0, 0][:, :half], bvtx_ref[0, 0]]
    vt_c = [cvtp_ref[0, 0], cvtc_ref[0, 0], cvtn_ref[0, 0], cvtx_ref[0, 0]]

    units = [(mixer, p, k) for mixer in "BC" for p in range(2) for k in range(2)]

    def scores(unit):
        mixer, p, k = unit
        rows = slice(p * LANES, (p + 1) * LANES)
        if mixer == "B":
            return _dot(k_b, _head_rows(bqt_ref[0, 0, rows, :], k))
        k_c = jnp.concatenate([ckp_ref[0, :, rows], ckc_ref[0, :, rows],
                               ckn_ref[0, :, rows], ckx_ref[0, :, rows]], axis=0)
        return _dot(k_c, _head_rows(cqt_ref[0, 0, rows, :], k))

    def attend(unit, s):
        mixer, p, k = unit
        feat = slice(k * HEAD_DIM, (k + 1) * HEAD_DIM)
        if mixer == "B":
            s = s + bmask_ref[0]
            sink = sink_ref[2 * k + p] * LOG2E
            m = jnp.maximum(jnp.max(s, axis=0, keepdims=True), sink)
            pv = _dot(_value_rows(vt_b, feat), jnp.exp2(s - m).astype(BF16))
            return pv[0:HEAD_DIM] / (pv[HEAD_DIM:HEAD_DIM + 1] + jnp.exp2(sink - m))
        s = s + ctab_ref[0, 2 * p + k]
        m = jnp.max(s, axis=0, keepdims=True)
        rows = slice(p * LANES + k * HEAD_DIM, p * LANES + (k + 1) * HEAD_DIM)
        pv = _dot(_value_rows(vt_c, rows), jnp.exp2(s - m).astype(BF16))
        return pv[0:HEAD_DIM] / pv[HEAD_DIM:HEAD_DIM + 1]

    def park(n):
        s = scores(units[n])
        s_ref[n % (MIX_LOOKAHEAD + 1), 0:s.shape[0], :] = s

    for n in range(MIX_LOOKAHEAD):
        park(n)
    outs = []
    for n, unit in enumerate(units):
        if n + MIX_LOOKAHEAD < len(units):
            park(n + MIX_LOOKAHEAD)
        n_keys = (TILE + 2 * SWA_WINDOW if unit[0] == "B" else 3 * TILE) + CTX_LEN
        outs.append(attend(unit, s_ref[n % (MIX_LOOKAHEAD + 1), 0:n_keys, :]))
    o_ref[0, :, 0:GROUP_WIDTH] = jnp.concatenate(outs[0:4], axis=0).T.astype(BF16)
    o_ref[0, :, GROUP_WIDTH:2 * GROUP_WIDTH] = jnp.concatenate(outs[4:8], axis=0).T.astype(BF16)

    has_prev = (j > 1).astype(F32)
    has_next = ((j > 0) & (j < n_tiles - 1)).astype(F32)
    dc = dc_ref[0].astype(F32)
    dp = dp_ref[0].astype(F32)[HALO_BF16 - HALO_F32:]
    dn = dn_ref[0].astype(F32)[:HALO_F32]
    gw = GROUP_WIDTH
    z = jnp.concatenate([dp[:, gw:2 * gw] * dp[:, 2 * gw:] * has_prev,
                         dc[:, gw:2 * gw] * dc[:, 2 * gw:],
                         dn[:, gw:2 * gw] * dn[:, 2 * gw:] * has_next], axis=0)
    cw = cw_ref[...]
    c0 = HALO_F32
    conv = (z[c0 - 1:c0 - 1 + TILE] * cw[0:1] + z[c0:c0 + TILE] * cw[1:2]
            + z[c0 + 1:c0 + 1 + TILE] * cw[2:3])
    o_ref[0, :, 4 * LANES:6 * LANES] = (dc[:, 0:gw] * conv).astype(BF16)


def _tile_variants(nt):
    return (0, 1, 2, nt - 1)


def _local_mix_call(sink, bqt, bk, bvt, cqt, ck, cvt, bmask, ctab, hd, conv_w):
    nb, rows, _ = bk.shape
    nt = rows // TILE
    variant = lambda j: jnp.where(j == 0, 0, jnp.where(j == 1, 1, jnp.where(j == nt - 1, 3, 2)))
    hb = TILE // SWA_WINDOW
    h16 = TILE // HALO_BF16
    prev_j = lambda j: jnp.maximum(j - 1, 0)
    next_j = lambda j: jnp.minimum(j + 1, nt - 1)
    cur = lambda w: pl.BlockSpec((1, TILE, w), lambda b, j: (b, j, 0))
    prev = lambda w: pl.BlockSpec((1, TILE, w), lambda b, j: (b, prev_j(j), 0))
    nxt = lambda w: pl.BlockSpec((1, TILE, w), lambda b, j: (b, next_j(j), 0))
    ctx = lambda w: pl.BlockSpec((1, TILE, w), lambda b, j: (b, 0, 0))
    tcur = lambda w: pl.BlockSpec((1, 1, w, TILE), lambda b, j: (b, j, 0, 0))
    tprev = lambda w: pl.BlockSpec((1, 1, w, TILE), lambda b, j: (b, prev_j(j), 0, 0))
    tnxt = lambda w: pl.BlockSpec((1, 1, w, TILE), lambda b, j: (b, next_j(j), 0, 0))
    tctx = lambda w: pl.BlockSpec((1, 1, w, TILE), lambda b, j: (b, 0, 0, 0))
    return pl.pallas_call(
        _local_mix_kernel,
        grid=(nb, nt),
        in_specs=[
            pl.BlockSpec(memory_space=pltpu.SMEM),
            tcur(256),
            pl.BlockSpec((1, SWA_WINDOW, 128), lambda b, j: (b, jnp.maximum(j * hb - 1, 0), 0)),
            cur(128),
            pl.BlockSpec((1, SWA_WINDOW, 128), lambda b, j: (b, jnp.minimum((j + 1) * hb, nt * hb - 1), 0)),
            ctx(128),
            tprev(128), tcur(128), tnxt(128), tctx(128),
            tcur(256),
            prev(256), cur(256), nxt(256), ctx(256),
            tprev(256), tcur(256), tnxt(256), tctx(256),
            pl.BlockSpec((1, TILE + 2 * SWA_WINDOW + CTX_LEN, TILE), lambda b, j: (variant(j), 0, 0)),
            pl.BlockSpec((1, NA_HEADS, 3 * TILE + CTX_LEN, TILE),
                         lambda b, j: (variant(j), 0, 0, 0)),
            pl.BlockSpec((1, HALO_BF16, 768), lambda b, j: (b, jnp.maximum(j * h16 - 1, 0), 0)),
            cur(768),
            pl.BlockSpec((1, HALO_BF16, 768), lambda b, j: (b, jnp.minimum((j + 1) * h16, nt * h16 - 1), 0)),
            _resident((3, GROUP_WIDTH)),
        ],
        out_specs=pl.BlockSpec((1, TILE, 768), lambda b, j: (b, j, 0)),
        out_shape=jax.ShapeDtypeStruct((nb, rows, 768), BF16),
        scratch_shapes=[pltpu.VMEM((MIX_LOOKAHEAD + 1, 3 * TILE + CTX_LEN, TILE), F32)],
        compiler_params=_params(2),
        name="local_mix",
    )(sink, bqt, bk, bk, bk, bk, bvt, bvt, bvt, bvt, cqt, ck, ck, ck, ck, cvt, cvt, cvt, cvt,
      bmask, ctab, hd, hd, hd, conv_w)


MXU_WIDTH = 256
FFN_CHUNKS = ((0, 6 * MXU_WIDTH), (6 * MXU_WIDTH, D_FF))
FFN_CHUNK_MAX = max(hi - lo for lo, hi in FFN_CHUNKS)
EXT = TILE + 2 * HALO_F32


def _halo_f32(prev_ref, next_ref):
    p = prev_ref[0].astype(F32)
    n = next_ref[0].astype(F32)
    return p[p.shape[0] - HALO_F32:], n[:HALO_F32]


def _out_ffn_kernel(final, xp_ref, xc_ref, xn_ref, ap_ref, ac_ref, an_ref, mp_ref, mc_ref, mn_ref,
                    mod_ref, wo_ref, nw_ref, wu_ref, cw_ref, cb_ref, wd_ref, fw_ref, o_ref, u_ref):
    j = pl.program_id(1)
    n_tiles = pl.num_programs(1)
    mod = mod_ref[0]
    g1 = mod[:, 2 * D_MODEL:3 * D_MODEL]
    sh2 = mod[:, 3 * D_MODEL:4 * D_MODEL]
    sc2 = mod[:, 4 * D_MODEL:5 * D_MODEL]
    g2 = mod[:, 5 * D_MODEL:6 * D_MODEL]

    ap, an = _halo_f32(ap_ref, an_ref)
    mp, mn = _halo_f32(mp_ref, mn_ref)
    mix = jnp.concatenate([
        jnp.concatenate([ap, mp], axis=1),
        jnp.concatenate([ac_ref[0], mc_ref[0]], axis=1).astype(F32),
        jnp.concatenate([an, mn], axis=1)], axis=0).astype(BF16)
    x = jnp.concatenate([xp_ref[0], xc_ref[0], xn_ref[0]], axis=0)
    x1 = x + g1 * _dot(mix, wo_ref[...])
    f = _rmsnorm_mod(x1, nw_ref[...], sh2, sc2).astype(BF16)

    has_prev = (j > 1).astype(F32)
    has_next = ((j > 0) & (j < n_tiles - 1)).astype(F32)
    lo = HALO_F32 - 1
    hi = HALO_F32 + TILE
    y = None
    for c0, c1 in FFN_CHUNKS:
        n = c1 - c0
        halves = []
        for base in (c0, D_FF + c0):
            cols = slice(base, base + n)
            u_ref[:, 0:n] = _dot(f, wu_ref[:, cols])
            u_ref[lo:lo + 1, 0:n] = u_ref[lo:lo + 1, 0:n] * has_prev
            u_ref[hi:hi + 1, 0:n] = u_ref[hi:hi + 1, 0:n] * has_next
            cw = cw_ref[:, cols]
            halves.append(u_ref[pl.ds(lo, TILE), 0:n] * cw[0:1]
                          + u_ref[pl.ds(lo + 1, TILE), 0:n] * cw[1:2]
                          + u_ref[pl.ds(lo + 2, TILE), 0:n] * cw[2:3] + cb_ref[:, cols])
        g, v = halves
        act = (g * (1.0 / (1.0 + jnp.exp(-g))) * v).astype(BF16)
        part = _dot(act, wd_ref[c0:c1, :])
        y = part if y is None else y + part
    x2 = x1[HALO_F32:HALO_F32 + TILE] + g2 * y
    if final:
        ms = jnp.mean(x2 * x2, axis=-1, keepdims=True)
        x2 = x2 * lax.rsqrt(ms + EPS) * fw_ref[...]
    o_ref[0] = x2


def _out_ffn_call(xall, oa, obcd, mod, w_out, norm_w, w_up, conv_w, conv_b, w_down, final_w, final):
    nb, rows, _ = xall.shape
    nt = rows // TILE
    h8 = TILE // HALO_F32
    h16 = TILE // HALO_BF16

    def trio(w, halo, per_tile):
        return [
            pl.BlockSpec((1, halo, w), lambda b, j: (b, jnp.maximum(j * per_tile - 1, 0), 0)),
            pl.BlockSpec((1, TILE, w), lambda b, j: (b, j, 0)),
            pl.BlockSpec((1, halo, w), lambda b, j: (b, jnp.minimum((j + 1) * per_tile, nt * per_tile - 1), 0)),
        ]

    if final:
        out_spec = pl.BlockSpec((1, TILE, D_MODEL), lambda b, j: (b, jnp.maximum(j - 1, 0), 0))
        out_shape = jax.ShapeDtypeStruct((nb, rows - CTX_LEN, D_MODEL), F32)
    else:
        out_spec = pl.BlockSpec((1, TILE, D_MODEL), lambda b, j: (b, j, 0))
        out_shape = jax.ShapeDtypeStruct((nb, rows, D_MODEL), F32)
    return pl.pallas_call(
        functools.partial(_out_ffn_kernel, final),
        grid=(nb, nt),
        in_specs=[
            *trio(D_MODEL, HALO_F32, h8),
            *trio(256, HALO_BF16, h16),
            *trio(768, HALO_BF16, h16),
            _mod_spec(nb),
            _resident((D_MODEL, D_MODEL)),
            _resident((1, D_MODEL)),
            _resident((D_MODEL, 2 * D_FF)),
            _resident((3, 2 * D_FF)),
            _resident((1, 2 * D_FF)),
            _resident((D_FF, D_MODEL)),
            _resident((1, D_MODEL)),
        ],
        out_specs=out_spec,
        out_shape=out_shape,
        scratch_shapes=[pltpu.VMEM((EXT, FFN_CHUNK_MAX), F32)],
        compiler_params=_params(2),
        name="out_ffn_final" if final else "out_ffn",
    )(xall, xall, xall, oa, oa, oa, obcd, obcd, obcd, mod, w_out, norm_w, w_up, conv_w, conv_b,
      w_down, final_w)


def _rope_tables(n_lat, dim):
    quarter = dim // 4
    t = np.arange(n_lat)
    pos = np.stack([t // GRID_W, t % GRID_W], axis=1).astype(np.float32)
    freqs = jnp.asarray(ROPE_THETA, F32) ** (-(jnp.arange(quarter, dtype=F32) / quarter))
    lane = np.arange(LANES) % dim
    axis = lane // (dim // 2)
    fidx = lane % quarter
    sign = np.where((lane % (dim // 2)) < quarter, -1.0, 1.0).astype(np.float32)
    ang = jnp.asarray(pos)[:, axis] * freqs[fidx][None, :]
    cos = jnp.concatenate([jnp.ones((CTX_LEN, LANES), F32), jnp.cos(ang)], axis=0)
    sin = jnp.concatenate([jnp.zeros((CTX_LEN, LANES), F32), jnp.sin(ang) * sign[None, :]], axis=0)
    return cos, sin


def _na_bias_table(rpb):
    rpt = TILE // GRID_W
    n_heads, n_dr, n_dc = rpb.shape
    cq = np.arange(GRID_W)[:, None]
    ck = np.arange(GRID_W)[None, :]
    dc = np.clip(ck - cq, -(NA_COLS - 1), NA_COLS - 1) + NA_COLS - 1
    col_start = np.clip(cq - NA_COLS // 2, 0, GRID_W - NA_COLS)
    col_ok = (ck >= col_start) & (ck < col_start + NA_COLS)
    pick = dc[None, :, :] == np.arange(n_dc)[:, None, None]
    e = jnp.sum(jnp.where(pick[None, None], rpb.astype(F32)[:, :, :, None, None], 0.0), axis=2)
    e = jnp.where(col_ok[None, None], e, NEG_INF)
    blocks = []
    for rq in range(rpt):
        lo = NA_ROWS - 1 - rpt - rq
        blocks.append(e[:, lo:lo + 3 * rpt].transpose(0, 1, 3, 2))
    local = jnp.stack(blocks, axis=3).reshape(n_heads, 3 * TILE, TILE)
    return jnp.concatenate([local, jnp.zeros((n_heads, CTX_LEN, TILE), F32)], axis=1)


def _window_masks(n_lat):
    nt = 1 + n_lat // TILE
    half = SWA_WINDOW
    rpt = TILE // GRID_W
    n_rows = n_lat // GRID_W
    rq = np.arange(TILE)[None, :]
    kb = np.arange(TILE + 2 * half + CTX_LEN)[:, None]
    kc = np.arange(3 * TILE + CTX_LEN)[:, None]
    masks_b, masks_c = [], []
    for j in _tile_variants(nt):
        kpos = (j - 1) * TILE - half + kb
        band = (np.abs(kb - half - rq) <= SWA_WINDOW) & (kpos >= 0) & (kpos < n_lat) & (j > 0)
        masks_b.append(band | (kb >= TILE + 2 * half))
        r = (j - 1) * rpt + rq // GRID_W
        kr = (j - 2) * rpt + kc // GRID_W
        rs = np.clip(r - NA_ROWS // 2, 0, n_rows - NA_ROWS)
        masks_c.append(((kr >= rs) & (kr < rs + NA_ROWS) & (j > 0)) | (kc >= 3 * TILE))
    to_add = lambda ms: jnp.where(jnp.asarray(np.stack(ms)), 0.0, NEG_INF).astype(F32)
    return to_add(masks_b), to_add(masks_c)


def _swa_head_order(w, axis, start):
    def blk(lo, hi):
        return lax.slice_in_dim(w, lo, hi, axis=axis)
    heads = [blk(start + h * HEAD_DIM, start + (h + 1) * HEAD_DIM) for h in (0, 2, 1, 3)]
    return jnp.concatenate(
        [blk(0, start), *heads, blk(start + SWA_HEADS * HEAD_DIM, w.shape[axis])], axis=axis)


def kernel(x, c, ctx, c_ctx, norm_mix_w, norm_ffn_w, w_mod, b_mod, w_in, w_out, diff_lambda_q1, diff_lambda_k1, diff_lambda_q2, diff_lambda_k2, diff_subln_w, swa_sink, na_rpb, sconv_w, ffn_w_up, ffn_conv_w, ffn_conv_b, ffn_w_down, final_norm_w):
    nb, n_lat, d = x.shape
    depth = w_in.shape[0]
    assert d == D_MODEL and ctx.shape[1] == CTX_LEN and n_lat % TILE == 0 and nb < 8
    assert n_lat // GRID_W >= NA_ROWS and n_lat % (max(2, BLOCKS_PER_STEP) * KV_BLOCK) == 0

    xall = jnp.concatenate([ctx, x], axis=1)
    cin = jnp.concatenate([c, c_ctx[None, :], jnp.zeros((8 - nb - 1, d), F32)], axis=0)
    mods = _mod_call(cin, w_mod, b_mod)

    tabs = (*_rope_tables(n_lat, DIFF_QK_DIM), *_rope_tables(n_lat, HEAD_DIM))
    mask_b, mask_c = _window_masks(n_lat)
    for l in range(depth):
        mod = mods[l].reshape(8, 1, 6 * D_MODEL)
        w_in_l = _swa_head_order(w_in[l], 1, 3 * GROUP_WIDTH).astype(BF16)
        w_out_l = _swa_head_order(w_out[l], 0, GROUP_WIDTH).astype(BF16)
        qa, ka, vat, bqt, bk, bvt, cqt, ck, cvt, hd = _in_proj_call(
            xall, mod, norm_mix_w[l][None, :], w_in_l, tabs)

        lambda_init = 0.8 - 0.6 * math.exp(-0.3 * l)
        lam_init = jnp.asarray([lambda_init, 1.0 - lambda_init], F32)
        lam_vecs = jnp.stack([diff_lambda_q1[l], diff_lambda_k1[l],
                              diff_lambda_q2[l], diff_lambda_k2[l]])
        subw = jnp.broadcast_to(diff_subln_w[l][:, None], (HEAD_DIM, QA_TILE))
        oa = _diff_attn_call(lam_init, lam_vecs, qa, ka, vat, subw)

        ctab = (_na_bias_table(na_rpb[l])[None] + mask_c[:, None]) * LOG2E
        obcd = _local_mix_call(swa_sink[l], bqt, bk, bvt, cqt, ck, cvt, mask_b, ctab,
                               hd, sconv_w[l])

        xall = _out_ffn_call(
            xall, oa, obcd, mod, w_out_l, norm_ffn_w[l][None, :],
            ffn_w_up[l].astype(BF16), ffn_conv_w[l], ffn_conv_b[l][None, :],
            ffn_w_down[l].astype(BF16), final_norm_w[None, :], final=(l == depth - 1))
    return xall
```

```python
import functools
import math

import numpy as np
import jax
import jax.numpy as jnp
from jax import lax
from jax.experimental import pallas as pl
from jax.experimental.pallas import tpu as pltpu

D_MODEL = 1024
DEPTH = 4
GRID_W = 64
CTX_LEN = 256
HEAD_DIM = 64
GROUP_WIDTH = 256
DIFF_HEADS = 4
DIFF_QK_DIM = 32
SWA_HEADS = 4
SWA_KV_HEADS = 2
SWA_WINDOW = 128
NA_HEADS = 4
NA_ROWS = 8
NA_COLS = 16
D_FF = 2816
IN_COLS = 2816
ROPE_THETA = 10000.0
EPS = 1e-6
NEG_INF = -1e30
LOG2E = math.log2(math.e)

TILE = 256
HALO_F32 = 8
HALO_BF16 = 16
LANES = 128
VMEM_LIMIT = 56 * 1024 * 1024

F32 = jnp.float32
BF16 = jnp.bfloat16


def _params(n_grid, flags=None):
    return pltpu.CompilerParams(
        dimension_semantics=("arbitrary",) * n_grid, vmem_limit_bytes=VMEM_LIMIT, flags=flags)


def _resident(shape):
    nd = len(shape)
    return pl.BlockSpec(shape, lambda *_: (0,) * nd, pipeline_mode=pl.Buffered(1))


def _mod_spec(n_batch):
    return pl.BlockSpec((1, 1, 6 * D_MODEL), lambda b, j: (jnp.where(j == 0, n_batch, b), 0, 0))


def _rmsnorm_mod(x, w, shift, scale):
    ms = jnp.mean(x * x, axis=-1, keepdims=True)
    return (x * lax.rsqrt(ms + EPS) * w) * (1.0 + scale) + shift


def _dot(a, b):
    return jnp.dot(a, b, preferred_element_type=F32)


def _dot_nt(a, b):
    return lax.dot_general(a, b, (((1,), (1,)), ((), ())), preferred_element_type=F32)


MOD_BLOCK = 1536


def _mod_kernel(c_ref, w_ref, b_ref, o_ref):
    c = c_ref[...]
    a = c * (1.0 / (1.0 + jnp.exp(-c)))
    o_ref[0] = _dot(a.astype(BF16), w_ref[0].astype(BF16)) + b_ref[0]


def _mod_call(cin, w_mod, b_mod):
    depth = w_mod.shape[0]
    n = 6 * D_MODEL
    return pl.pallas_call(
        _mod_kernel,
        grid=(depth, n // MOD_BLOCK),
        in_specs=[
            pl.BlockSpec((8, D_MODEL), lambda l, k: (0, 0)),
            pl.BlockSpec((1, D_MODEL, MOD_BLOCK), lambda l, k: (l, 0, k)),
            pl.BlockSpec((1, 1, MOD_BLOCK), lambda l, k: (l, 0, k)),
        ],
        out_specs=pl.BlockSpec((1, 8, MOD_BLOCK), lambda l, k: (l, 0, k)),
        out_shape=jax.ShapeDtypeStruct((depth, 8, n), F32),
        compiler_params=_params(2),
        name="mod_proj",
    )(cin, w_mod, b_mod.reshape(depth, 1, n))


def _rope(x, cos, sin, half):
    lane = lax.broadcasted_iota(jnp.int32, x.shape, 1)
    first = (lane % (2 * half)) < half
    partner = jnp.where(first, pltpu.roll(x, LANES - half, 1), pltpu.roll(x, half, 1))
    return x * cos + partner * sin


def _in_proj_kernel(x_ref, mod_ref, nw_ref, w_ref, cosa_ref, sina_ref, cosb_ref, sinb_ref,
                    qa_ref, ka_ref, vat_ref, bqt_ref, bk_ref, bvt_ref, cqt_ref, ck_ref, cvt_ref,
                    hd_ref):
    mod = mod_ref[0]
    shift = mod[:, 0:D_MODEL]
    scale = mod[:, D_MODEL:2 * D_MODEL]
    z = _rmsnorm_mod(x_ref[0], nw_ref[...], shift, scale)
    h = _dot(z.astype(BF16), w_ref[...])

    def sec(k, n=1):
        return h[:, k * LANES:(k + n) * LANES]

    cosa, sina = cosa_ref[...], sina_ref[...]
    cosb, sinb = cosb_ref[...], sinb_ref[...]
    qa_scale = DIFF_QK_DIM ** -0.5 * LOG2E
    q_scale = HEAD_DIM ** -0.5 * LOG2E
    ha = DIFF_QK_DIM // 4
    hb = HEAD_DIM // 4
    for t in range(2):
        qa_ref[0, :, t * LANES:(t + 1) * LANES] = (
            _rope(sec(t), cosa, sina, ha) * qa_scale).astype(BF16)
        ka_ref[0, :, t * LANES:(t + 1) * LANES] = _rope(sec(2 + t), cosa, sina, ha).astype(BF16)
    vat_ref[0, 0] = sec(4, 2).T.astype(BF16)
    for t in range(2):
        bqt_ref[0, 0, t * LANES:(t + 1) * LANES, :] = (
            _rope(sec(6 + t), cosb, sinb, hb) * q_scale).T.astype(BF16)
    bk_ref[0] = _rope(sec(8), cosb, sinb, hb).astype(BF16)
    bvt_ref[0, 0] = sec(9).T.astype(BF16)
    cqt_ref[0, 0] = (sec(10, 2) * q_scale).T.astype(BF16)
    ck_ref[0] = sec(12, 2).astype(BF16)
    cvt_ref[0, 0] = sec(14, 2).T.astype(BF16)
    hd_ref[0] = sec(16, 6).astype(BF16)


def _in_proj_call(xall, mod, norm_w, w_in, tabs):
    nb, rows, _ = xall.shape
    nt = rows // TILE
    tile = lambda w: pl.BlockSpec((1, TILE, w), lambda b, j: (b, j, 0))
    ttile = lambda w: pl.BlockSpec((1, 1, w, TILE), lambda b, j: (b, j, 0, 0))
    tab = pl.BlockSpec((TILE, LANES), lambda b, j: (j, 0))
    sds = lambda w: jax.ShapeDtypeStruct((nb, rows, w), BF16)
    tsds = lambda w: jax.ShapeDtypeStruct((nb, nt, w, TILE), BF16)
    return pl.pallas_call(
        _in_proj_kernel,
        grid=(nb, nt),
        in_specs=[
            pl.BlockSpec((1, TILE, D_MODEL), lambda b, j: (b, j, 0)),
            _mod_spec(nb),
            _resident((1, D_MODEL)),
            _resident((D_MODEL, IN_COLS)),
            tab, tab, tab, tab,
        ],
        out_specs=[
            tile(256), tile(256), ttile(256),
            ttile(256), tile(128), ttile(128),
            ttile(256), tile(256), ttile(256),
            tile(768),
        ],
        out_shape=[
            sds(256), sds(256), tsds(256),
            tsds(256), sds(128), tsds(128),
            tsds(256), sds(256), tsds(256),
            sds(768),
        ],
        compiler_params=_params(2),
        name="in_proj",
    )(xall, mod, norm_w, w_in, *tabs)


QA_TILE = 256
KV_BLOCK = 512
BLOCKS_PER_STEP = 16
LOOKAHEAD = 2
EXP_CAP = 100.0
PV_ROWS = HEAD_DIM + HALO_BF16


def _diff_attn_kernel(li_ref, lv_ref, q_ref, k_ref, vt_ref, sw_ref, o_ref,
                      qst_ref, sa_ref, sb_ref, mca_ref, mcb_ref, m_ref, acc_ref, gap_ref):
    i = pl.program_id(1)
    s_refs = (sa_ref, sb_ref)
    mc_refs = (mca_ref, mcb_ref)
    q = q_ref[0].astype(F32)
    lane = lax.broadcasted_iota(jnp.int32, q.shape, 1) // DIFF_QK_DIM
    for h in range(DIFF_HEADS):
        qs = jnp.concatenate([jnp.where(lane == 2 * h + c, q, 0.0) for c in range(2)], axis=0)
        qst_ref[h] = qs.T.astype(BF16)
    n_cols = 2 * QA_TILE
    n_blk = (k_ref.shape[1] - CTX_LEN) // KV_BLOCK
    tiles_per_blk = KV_BLOCK // TILE

    def value_rows(vt, h):
        ones = jnp.ones((HALO_BF16, vt.shape[1]), BF16)
        return jnp.concatenate([vt[h * HEAD_DIM:(h + 1) * HEAD_DIM], ones], axis=0)

    def ctx_scores(h):
        return _dot(k_ref[0, 0:CTX_LEN, :], qst_ref[h])

    def ctx_init(h, s):
        m0 = jnp.max(s, axis=0, keepdims=True)
        acc_ref[h] = _dot(value_rows(vt_ref[0, 0], h), jnp.exp2(s - m0).astype(BF16))
        m_ref[h] = m0

    def score_head(blk, slot, h):
        row0 = pl.multiple_of(CTX_LEN + (blk - 1) * KV_BLOCK, TILE)
        s = _dot(k_ref[0, pl.ds(row0, KV_BLOCK), :], qst_ref[h])
        s_refs[slot][h] = s
        mc_refs[slot][h] = jnp.max(s, axis=0, keepdims=True)

    def block_step(score_blk, acc_blk):
        if acc_blk is not None:
            ablk, aslot = acc_blk
            t0 = 1 + (ablk - 1) * tiles_per_blk
            vt = jnp.concatenate([vt_ref[0, t0 + u] for u in range(tiles_per_blk)], axis=1)
        for h in range(DIFF_HEADS):
            if score_blk is not None:
                score_head(*score_blk, h)
            if acc_blk is not None:
                m_old = m_ref[h]
                m_new = jnp.maximum(m_old, mc_refs[aslot][h])
                p = jnp.exp2(s_refs[aslot][h] - m_new).astype(BF16)
                acc_ref[h] = acc_ref[h] * jnp.exp2(m_old - m_new) + _dot(value_rows(vt, h), p)
                m_ref[h] = m_new

    @pl.when(i < CTX_LEN // QA_TILE)
    def _():
        for h in range(DIFF_HEADS):
            ctx_init(h, ctx_scores(h))

    def two_pass_latent():
        scores0 = [ctx_scores(h) for h in range(DIFF_HEADS)]
        for h in range(DIFF_HEADS):
            score_head(1, 0, h)
            ctx_init(h, scores0[h])

        def step(it, carry):
            b = 2 * it + 1
            block_step((b + 1, 1), (b, 0))
            block_step((b + 2, 0), (b + 1, 1))
            return carry

        lax.fori_loop(0, n_blk // 2 - 1, step, 0)
        block_step((n_blk, 1), (n_blk - 1, 0))
        block_step(None, (n_blk, 1))

    def one_pass_latent():
        scores0 = [ctx_scores(h) for h in range(DIFF_HEADS)]
        for h in range(DIFF_HEADS):
            ctx_init(h, scores0[h])
            gap_ref[h] = jnp.zeros(gap_ref.shape[1:], F32)

        def scores(unit):
            blk, h, c = unit
            row0 = pl.multiple_of(CTX_LEN + (blk - 1) * KV_BLOCK, TILE)
            return _dot(k_ref[0, pl.ds(row0, KV_BLOCK), :],
                        qst_ref[h, :, c * QA_TILE:(c + 1) * QA_TILE])

        def consume(unit, s, vt):
            _, h, c = unit
            cols = slice(c * QA_TILE, (c + 1) * QA_TILE)
            m_old = m_ref[h, :, cols]
            m_cur = jnp.max(s, axis=0, keepdims=True)
            pv = _dot(value_rows(vt, h), jnp.exp2(s - m_old).astype(BF16))
            m_new = jnp.maximum(m_old, m_cur)
            acc_ref[h, :, cols] = (acc_ref[h, :, cols] + pv) * jnp.exp2(m_old - m_new)
            m_ref[h, :, cols] = m_new
            gap_ref[h, :, cols] = jnp.maximum(gap_ref[h, :, cols], m_cur - m_old)

        def step(it, carry):
            b0 = it * BLOCKS_PER_STEP + 1
            units = [(b0 + u, h, c) for u in range(BLOCKS_PER_STEP)
                     for h in range(DIFF_HEADS) for c in range(2)]
            pending = [scores(u) for u in units[:LOOKAHEAD]]
            for n, unit in enumerate(units):
                if n + LOOKAHEAD < len(units):
                    pending.append(scores(units[n + LOOKAHEAD]))
                if unit[1] == 0 and unit[2] == 0:
                    t0 = 1 + (unit[0] - 1) * tiles_per_blk
                    vt = jnp.concatenate(
                        [vt_ref[0, t0 + u] for u in range(tiles_per_blk)], axis=1)
                consume(unit, pending.pop(0), vt)
            return carry

        lax.fori_loop(0, n_blk // BLOCKS_PER_STEP, step, 0)

    @pl.when(i >= CTX_LEN // QA_TILE)
    def _():
        one_pass_latent()

        @pl.when(jnp.max(gap_ref[...]) > EXP_CAP)
        def _():
            two_pass_latent()

    lv = lv_ref[...]
    lam = (jnp.exp(jnp.sum(lv[0:1] * lv[1:2], axis=1, keepdims=True))
           - jnp.exp(jnp.sum(lv[2:3] * lv[3:4], axis=1, keepdims=True)) + li_ref[0])
    out_scale = li_ref[1]
    outs = []
    for h in range(DIFF_HEADS):
        a = acc_ref[h]
        o1 = a[0:HEAD_DIM, 0:QA_TILE] / a[HEAD_DIM:HEAD_DIM + 1, 0:QA_TILE]
        o2 = a[0:HEAD_DIM, QA_TILE:n_cols] / a[HEAD_DIM:HEAD_DIM + 1, QA_TILE:n_cols]
        o = o1 - lam * o2
        ms = jnp.mean(o * o, axis=0, keepdims=True)
        outs.append(o * lax.rsqrt(ms + EPS) * sw_ref[...] * out_scale)
    o_ref[0] = jnp.concatenate(outs, axis=0).T.astype(BF16)


def _diff_attn_call(lam_init, lam_vecs, qa, ka, vat, subw):
    nb, rows, _ = qa.shape
    nt = rows // TILE
    return pl.pallas_call(
        _diff_attn_kernel,
        grid=(nb, rows // QA_TILE),
        in_specs=[
            pl.BlockSpec(memory_space=pltpu.SMEM),
            _resident((4, DIFF_QK_DIM)),
            pl.BlockSpec((1, QA_TILE, 256), lambda b, i: (b, i, 0)),
            pl.BlockSpec((1, rows, 256), lambda b, i: (b, 0, 0)),
            pl.BlockSpec((1, nt, 256, TILE), lambda b, i: (b, 0, 0, 0)),
            _resident((HEAD_DIM, QA_TILE)),
        ],
        out_specs=pl.BlockSpec((1, QA_TILE, 256), lambda b, i: (b, i, 0)),
        out_shape=jax.ShapeDtypeStruct((nb, rows, 256), BF16),
        scratch_shapes=[
            pltpu.VMEM((DIFF_HEADS, 256, 2 * QA_TILE), BF16),
            pltpu.VMEM((DIFF_HEADS, KV_BLOCK, 2 * QA_TILE), F32),
            pltpu.VMEM((DIFF_HEADS, KV_BLOCK, 2 * QA_TILE), F32),
            pltpu.VMEM((DIFF_HEADS, 1, 2 * QA_TILE), F32),
            pltpu.VMEM((DIFF_HEADS, 1, 2 * QA_TILE), F32),
            pltpu.VMEM((DIFF_HEADS, 1, 2 * QA_TILE), F32),
            pltpu.VMEM((DIFF_HEADS, PV_ROWS, 2 * QA_TILE), F32),
            pltpu.VMEM((DIFF_HEADS, 1, 2 * QA_TILE), F32),
        ],
        compiler_params=_params(2),
        name="diff_attn",
    )(lam_init, lam_vecs, qa, ka, vat, subw)


MIX_LOOKAHEAD = 2


def _head_rows(qt, k):
    row = lax.broadcasted_iota(jnp.int32, qt.shape, 0) // HEAD_DIM
    return jnp.where(row == k, qt, jnp.zeros_like(qt))


def _value_rows(vt_parts, rows):
    v = jnp.concatenate([part[rows] for part in vt_parts], axis=1)
    return jnp.concatenate([v, jnp.ones((HALO_BF16, v.shape[1]), BF16)], axis=0)


def _local_mix_kernel(sink_ref, bqt_ref, bkp_ref, bkc_ref, bkn_ref, bkx_ref,
                      bvtp_ref, bvtc_ref, bvtn_ref, bvtx_ref,
                      cqt_ref, ckp_ref, ckc_ref, ckn_ref, ckx_ref,
                      cvtp_ref, cvtc_ref, cvtn_ref, cvtx_ref, bmask_ref, ctab_ref,
                      dp_ref, dc_ref, dn_ref, cw_ref, o_ref, s_ref):
    j = pl.program_id(1)
    n_tiles = pl.num_programs(1)
    half = SWA_WINDOW

    k_b = jnp.concatenate([bkp_ref[0], bkc_ref[0], bkn_ref[0], bkx_ref[0]], axis=0)
    vt_b = [bvtp_ref[0, 0][:, half:], bvtc_ref[0, 0], bvtn_ref[0, 0][:, :half], bvtx_ref[0, 0]]
    vt_c = [cvtp_ref[0, 0], cvtc_ref[0, 0], cvtn_ref[0, 0], cvtx_ref[0, 0]]

    units = [(mixer, p, k) for mixer in "BC" for p in range(2) for k in range(2)]

    def scores(unit):
        mixer, p, k = unit
        rows = slice(p * LANES, (p + 1) * LANES)
        if mixer == "B":
            return _dot(k_b, _head_rows(bqt_ref[0, 0, rows, :], k))
        k_c = jnp.concatenate([ckp_ref[0, :, rows], ckc_ref[0, :, rows],
                               ckn_ref[0, :, rows], ckx_ref[0, :, rows]], axis=0)
        return _dot(k_c, _head_rows(cqt_ref[0, 0, rows, :], k))

    def attend(unit, s):
        mixer, p, k = unit
        feat = slice(k * HEAD_DIM, (k + 1) * HEAD_DIM)
        if mixer == "B":
            s = s + bmask_ref[0]
            sink = sink_ref[2 * k + p] * LOG2E
            m = jnp.maximum(jnp.max(s, axis=0, keepdims=True), sink)
            pv = _dot(_value_rows(vt_b, feat), jnp.exp2(s - m).astype(BF16))
            return pv[0:HEAD_DIM] / (pv[HEAD_DIM:HEAD_DIM + 1] + jnp.exp2(sink - m))
        s = s + ctab_ref[0, 2 * p + k]
        m = jnp.max(s, axis=0, keepdims=True)
        rows = slice(p * LANES + k * HEAD_DIM, p * LANES + (k + 1) * HEAD_DIM)
        pv = _dot(_value_rows(vt_c, rows), jnp.exp2(s - m).astype(BF16))
        return pv[0:HEAD_DIM] / pv[HEAD_DIM:HEAD_DIM + 1]

    def park(n):
        s = scores(units[n])
        s_ref[n % (MIX_LOOKAHEAD + 1), 0:s.shape[0], :] = s

    for n in range(MIX_LOOKAHEAD):
        park(n)
    outs = []
    for n, unit in enumerate(units):
        if n + MIX_LOOKAHEAD < len(units):
            park(n + MIX_LOOKAHEAD)
        n_keys = (TILE + 2 * SWA_WINDOW if unit[0] == "B" else 3 * TILE) + CTX_LEN
        outs.append(attend(unit, s_ref[n % (MIX_LOOKAHEAD + 1), 0:n_keys, :]))
    o_ref[0, :, 0:GROUP_WIDTH] = jnp.concatenate(outs[0:4], axis=0).T.astype(BF16)
    o_ref[0, :, GROUP_WIDTH:2 * GROUP_WIDTH] = jnp.concatenate(outs[4:8], axis=0).T.astype(BF16)

    has_prev = (j > 1).astype(F32)
    has_next = ((j > 0) & (j < n_tiles - 1)).astype(F32)
    dc = dc_ref[0].astype(F32)
    dp = dp_ref[0].astype(F32)[HALO_BF16 - HALO_F32:]
    dn = dn_ref[0].astype(F32)[:HALO_F32]
    gw = GROUP_WIDTH
    z = jnp.concatenate([dp[:, gw:2 * gw] * dp[:, 2 * gw:] * has_prev,
                         dc[:, gw:2 * gw] * dc[:, 2 * gw:],
                         dn[:, gw:2 * gw] * dn[:, 2 * gw:] * has_next], axis=0)
    cw = cw_ref[...]
    c0 = HALO_F32
    conv = (z[c0 - 1:c0 - 1 + TILE] * cw[0:1] + z[c0:c0 + TILE] * cw[1:2]
            + z[c0 + 1:c0 + 1 + TILE] * cw[2:3])
    o_ref[0, :, 4 * LANES:6 * LANES] = (dc[:, 0:gw] * conv).astype(BF16)


def _tile_variants(nt):
    return (0, 1, 2, nt - 1)


def _local_mix_call(sink, bqt, bk, bvt, cqt, ck, cvt, bmask, ctab, hd, conv_w):
    nb, rows, _ = bk.shape
    nt = rows // TILE
    variant = lambda j: jnp.where(j == 0, 0, jnp.where(j == 1, 1, jnp.where(j == nt - 1, 3, 2)))
    hb = TILE // SWA_WINDOW
    h16 = TILE // HALO_BF16
    prev_j = lambda j: jnp.maximum(j - 1, 0)
    next_j = lambda j: jnp.minimum(j + 1, nt - 1)
    cur = lambda w: pl.BlockSpec((1, TILE, w), lambda b, j: (b, j, 0))
    prev = lambda w: pl.BlockSpec((1, TILE, w), lambda b, j: (b, prev_j(j), 0))
    nxt = lambda w: pl.BlockSpec((1, TILE, w), lambda b, j: (b, next_j(j), 0))
    ctx = lambda w: pl.BlockSpec((1, TILE, w), lambda b, j: (b, 0, 0))
    tcur = lambda w: pl.BlockSpec((1, 1, w, TILE), lambda b, j: (b, j, 0, 0))
    tprev = lambda w: pl.BlockSpec((1, 1, w, TILE), lambda b, j: (b, prev_j(j), 0, 0))
    tnxt = lambda w: pl.BlockSpec((1, 1, w, TILE), lambda b, j: (b, next_j(j), 0, 0))
    tctx = lambda w: pl.BlockSpec((1, 1, w, TILE), lambda b, j: (b, 0, 0, 0))
    return pl.pallas_call(
        _local_mix_kernel,
        grid=(nb, nt),
        in_specs=[
            pl.BlockSpec(memory_space=pltpu.SMEM),
            tcur(256),
            pl.BlockSpec((1, SWA_WINDOW, 128), lambda b, j: (b, jnp.maximum(j * hb - 1, 0), 0)),
            cur(128),
            pl.BlockSpec((1, SWA_WINDOW, 128), lambda b, j: (b, jnp.minimum((j + 1) * hb, nt * hb - 1), 0)),
            ctx(128),
            tprev(128), tcur(128), tnxt(128), tctx(128),
            tcur(256),
            prev(256), cur(256), nxt(256), ctx(256),
            tprev(256), tcur(256), tnxt(256), tctx(256),
            pl.BlockSpec((1, TILE + 2 * SWA_WINDOW + CTX_LEN, TILE), lambda b, j: (variant(j), 0, 0)),
            pl.BlockSpec((1, NA_HEADS, 3 * TILE + CTX_LEN, TILE),
                         lambda b, j: (variant(j), 0, 0, 0)),
            pl.BlockSpec((1, HALO_BF16, 768), lambda b, j: (b, jnp.maximum(j * h16 - 1, 0), 0)),
            cur(768),
            pl.BlockSpec((1, HALO_BF16, 768), lambda b, j: (b, jnp.minimum((j + 1) * h16, nt * h16 - 1), 0)),
            _resident((3, GROUP_WIDTH)),
        ],
        out_specs=pl.BlockSpec((1, TILE, 768), lambda b, j: (b, j, 0)),
        out_shape=jax.ShapeDtypeStruct((nb, rows, 768), BF16),
        scratch_shapes=[pltpu.VMEM((MIX_LOOKAHEAD + 1, 3 * TILE + CTX_LEN, TILE), F32)],
        compiler_params=_params(2),
        name="local_mix",
    )(sink, bqt, bk, bk, bk, bk, bvt, bvt, bvt, bvt, cqt, ck, ck, ck, ck, cvt, cvt, cvt, cvt,
      bmask, ctab, hd, hd, hd, conv_w)


MXU_WIDTH = 256
FFN_CHUNKS = ((0, 6 * MXU_WIDTH), (6 * MXU_WIDTH, D_FF))
FFN_CHUNK_MAX = max(hi - lo for lo, hi in FFN_CHUNKS)
EXT = TILE + 2 * HALO_F32


def _halo_f32(prev_ref, next_ref):
    p = prev_ref[0].astype(F32)
    n = next_ref[0].astype(F32)
    return p[p.shape[0] - HALO_F32:], n[:HALO_F32]


def _out_ffn_kernel(final, xp_ref, xc_ref, xn_ref, ap_ref, ac_ref, an_ref, mp_ref, mc_ref, mn_ref,
                    mod_ref, wo_ref, nw_ref, wu_ref, cw_ref, cb_ref, wd_ref, fw_ref, o_ref, u_ref):
    j = pl.program_id(1)
    n_tiles = pl.num_programs(1)
    mod = mod_ref[0]
    g1 = mod[:, 2 * D_MODEL:3 * D_MODEL]
    sh2 = mod[:, 3 * D_MODEL:4 * D_MODEL]
    sc2 = mod[:, 4 * D_MODEL:5 * D_MODEL]
    g2 = mod[:, 5 * D_MODEL:6 * D_MODEL]

    ap, an = _halo_f32(ap_ref, an_ref)
    mp, mn = _halo_f32(mp_ref, mn_ref)
    mix = jnp.concatenate([
        jnp.concatenate([ap, mp], axis=1),
        jnp.concatenate([ac_ref[0], mc_ref[0]], axis=1).astype(F32),
        jnp.concatenate([an, mn], axis=1)], axis=0).astype(BF16)
    x = jnp.concatenate([xp_ref[0], xc_ref[0], xn_ref[0]], axis=0)
    x1 = x + g1 * _dot(mix, wo_ref[...])
    f = _rmsnorm_mod(x1, nw_ref[...], sh2, sc2).astype(BF16)

    has_prev = (j > 1).astype(F32)
    has_next = ((j > 0) & (j < n_tiles - 1)).astype(F32)
    lo = HALO_F32 - 1
    hi = HALO_F32 + TILE
    y = None
    for c0, c1 in FFN_CHUNKS:
        n = c1 - c0
        halves = []
        for base in (c0, D_FF + c0):
            u = _dot(f, wu_ref[:, base:base + n])
            slabs = []
            for t in range(n // LANES):
                cols = slice(base + t * LANES, base + (t + 1) * LANES)
                u_ref[t] = u[:, t * LANES:(t + 1) * LANES]
                u_ref[t, lo:lo + 1, :] = u_ref[t, lo:lo + 1, :] * has_prev
                u_ref[t, hi:hi + 1, :] = u_ref[t, hi:hi + 1, :] * has_next
                cw = cw_ref[:, cols]
                slabs.append(u_ref[t, pl.ds(lo, TILE), :] * cw[0:1]
                             + u_ref[t, pl.ds(lo + 1, TILE), :] * cw[1:2]
                             + u_ref[t, pl.ds(lo + 2, TILE), :] * cw[2:3] + cb_ref[:, cols])
            halves.append(jnp.concatenate(slabs, axis=1))
        g, v = halves
        act = (g * (1.0 / (1.0 + jnp.exp(-g))) * v).astype(BF16)
        part = _dot(act, wd_ref[c0:c1, :])
        y = part if y is None else y + part
    x2 = x1[HALO_F32:HALO_F32 + TILE] + g2 * y
    if final:
        ms = jnp.mean(x2 * x2, axis=-1, keepdims=True)
        x2 = x2 * lax.rsqrt(ms + EPS) * fw_ref[...]
    o_ref[0] = x2


def _out_ffn_call(xall, oa, obcd, mod, w_out, norm_w, w_up, conv_w, conv_b, w_down, final_w, final):
    nb, rows, _ = xall.shape
    nt = rows // TILE
    h8 = TILE // HALO_F32
    h16 = TILE // HALO_BF16

    def trio(w, halo, per_tile):
        return [
            pl.BlockSpec((1, halo, w), lambda b, j: (b, jnp.maximum(j * per_tile - 1, 0), 0)),
            pl.BlockSpec((1, TILE, w), lambda b, j: (b, j, 0)),
            pl.BlockSpec((1, halo, w), lambda b, j: (b, jnp.minimum((j + 1) * per_tile, nt * per_tile - 1), 0)),
        ]

    if final:
        out_spec = pl.BlockSpec((1, TILE, D_MODEL), lambda b, j: (b, jnp.maximum(j - 1, 0), 0))
        out_shape = jax.ShapeDtypeStruct((nb, rows - CTX_LEN, D_MODEL), F32)
    else:
        out_spec = pl.BlockSpec((1, TILE, D_MODEL), lambda b, j: (b, j, 0))
        out_shape = jax.ShapeDtypeStruct((nb, rows, D_MODEL), F32)
    return pl.pallas_call(
        functools.partial(_out_ffn_kernel, final),
        grid=(nb, nt),
        in_specs=[
            *trio(D_MODEL, HALO_F32, h8),
            *trio(256, HALO_BF16, h16),
            *trio(768, HALO_BF16, h16),
            _mod_spec(nb),
            _resident((D_MODEL, D_MODEL)),
            _resident((1, D_MODEL)),
            _resident((D_MODEL, 2 * D_FF)),
            _resident((3, 2 * D_FF)),
            _resident((1, 2 * D_FF)),
            _resident((D_FF, D_MODEL)),
            _resident((1, D_MODEL)),
        ],
        out_specs=out_spec,
        out_shape=out_shape,
        scratch_shapes=[pltpu.VMEM((FFN_CHUNK_MAX // LANES, EXT, LANES), F32)],
        compiler_params=_params(2),
        name="out_ffn_final" if final else "out_ffn",
    )(xall, xall, xall, oa, oa, oa, obcd, obcd, obcd, mod, w_out, norm_w, w_up, conv_w, conv_b,
      w_down, final_w)


def _rope_tables(n_lat, dim):
    quarter = dim // 4
    t = np.arange(n_lat)
    pos = np.stack([t // GRID_W, t % GRID_W], axis=1).astype(np.float32)
    freqs = jnp.asarray(ROPE_THETA, F32) ** (-(jnp.arange(quarter, dtype=F32) / quarter))
    lane = np.arange(LANES) % dim
    axis = lane // (dim // 2)
    fidx = lane % quarter
    sign = np.where((lane % (dim // 2)) < quarter, -1.0, 1.0).astype(np.float32)
    ang = jnp.asarray(pos)[:, axis] * freqs[fidx][None, :]
    cos = jnp.concatenate([jnp.ones((CTX_LEN, LANES), F32), jnp.cos(ang)], axis=0)
    sin = jnp.concatenate([jnp.zeros((CTX_LEN, LANES), F32), jnp.sin(ang) * sign[None, :]], axis=0)
    return cos, sin


def _na_bias_table(rpb):
    rpt = TILE // GRID_W
    n_heads, n_dr, n_dc = rpb.shape
    cq = np.arange(GRID_W)[:, None]
    ck = np.arange(GRID_W)[None, :]
    dc = np.clip(ck - cq, -(NA_COLS - 1), NA_COLS - 1) + NA_COLS - 1
    col_start = np.clip(cq - NA_COLS // 2, 0, GRID_W - NA_COLS)
    col_ok = (ck >= col_start) & (ck < col_start + NA_COLS)
    pick = dc[None, :, :] == np.arange(n_dc)[:, None, None]
    e = jnp.sum(jnp.where(pick[None, None], rpb.astype(F32)[:, :, :, None, None], 0.0), axis=2)
    e = jnp.where(col_ok[None, None], e, NEG_INF)
    blocks = []
    for rq in range(rpt):
        lo = NA_ROWS - 1 - rpt - rq
        blocks.append(e[:, lo:lo + 3 * rpt].transpose(0, 1, 3, 2))
    local = jnp.stack(blocks, axis=3).reshape(n_heads, 3 * TILE, TILE)
    return jnp.concatenate([local, jnp.zeros((n_heads, CTX_LEN, TILE), F32)], axis=1)


def _window_masks(n_lat):
    nt = 1 + n_lat // TILE
    half = SWA_WINDOW
    rpt = TILE // GRID_W
    n_rows = n_lat // GRID_W
    rq = np.arange(TILE)[None, :]
    kb = np.arange(TILE + 2 * half + CTX_LEN)[:, None]
    kc = np.arange(3 * TILE + CTX_LEN)[:, None]
    masks_b, masks_c = [], []
    for j in _tile_variants(nt):
        kpos = (j - 1) * TILE - half + kb
        band = (np.abs(kb - half - rq) <= SWA_WINDOW) & (kpos >= 0) & (kpos < n_lat) & (j > 0)
        masks_b.append(band | (kb >= TILE + 2 * half))
        r = (j - 1) * rpt + rq // GRID_W
        kr = (j - 2) * rpt + kc // GRID_W
        rs = np.clip(r - NA_ROWS // 2, 0, n_rows - NA_ROWS)
        masks_c.append(((kr >= rs) & (kr < rs + NA_ROWS) & (j > 0)) | (kc >= 3 * TILE))
    to_add = lambda ms: jnp.where(jnp.asarray(np.stack(ms)), 0.0, NEG_INF).astype(F32)
    return to_add(masks_b), to_add(masks_c)


def _swa_head_order(w, axis, start):
    def blk(lo, hi):
        return lax.slice_in_dim(w, lo, hi, axis=axis)
    heads = [blk(start + h * HEAD_DIM, start + (h + 1) * HEAD_DIM) for h in (0, 2, 1, 3)]
    return jnp.concatenate(
        [blk(0, start), *heads, blk(start + SWA_HEADS * HEAD_DIM, w.shape[axis])], axis=axis)


def kernel(x, c, ctx, c_ctx, norm_mix_w, norm_ffn_w, w_mod, b_mod, w_in, w_out, diff_lambda_q1, diff_lambda_k1, diff_lambda_q2, diff_lambda_k2, diff_subln_w, swa_sink, na_rpb, sconv_w, ffn_w_up, ffn_conv_w, ffn_conv_b, ffn_w_down, final_norm_w):
    nb, n_lat, d = x.shape
    depth = w_in.shape[0]
    assert d == D_MODEL and ctx.shape[1] == CTX_LEN and n_lat % TILE == 0 and nb < 8
    assert n_lat // GRID_W >= NA_ROWS and n_lat % (max(2, BLOCKS_PER_STEP) * KV_BLOCK) == 0

    xall = jnp.concatenate([ctx, x], axis=1)
    cin = jnp.concatenate([c, c_ctx[None, :], jnp.zeros((8 - nb - 1, d), F32)], axis=0)
    mods = _mod_call(cin, w_mod, b_mod)

    tabs = (*_rope_tables(n_lat, DIFF_QK_DIM), *_rope_tables(n_lat, HEAD_DIM))
    mask_b, mask_c = _window_masks(n_lat)
    for l in range(depth):
        mod = mods[l].reshape(8, 1, 6 * D_MODEL)
        w_in_l = _swa_head_order(w_in[l], 1, 3 * GROUP_WIDTH).astype(BF16)
        w_out_l = _swa_head_order(w_out[l], 0, GROUP_WIDTH).astype(BF16)
        qa, ka, vat, bqt, bk, bvt, cqt, ck, cvt, hd = _in_proj_call(
            xall, mod, norm_mix_w[l][None, :], w_in_l, tabs)

        lambda_init = 0.8 - 0.6 * math.exp(-0.3 * l)
        lam_init = jnp.asarray([lambda_init, 1.0 - lambda_init], F32)
        lam_vecs = jnp.stack([diff_lambda_q1[l], diff_lambda_k1[l],
                              diff_lambda_q2[l], diff_lambda_k2[l]])
        subw = jnp.broadcast_to(diff_subln_w[l][:, None], (HEAD_DIM, QA_TILE))
        oa = _diff_attn_call(lam_init, lam_vecs, qa, ka, vat, subw)

        ctab = (_na_bias_table(na_rpb[l])[None] + mask_c[:, None]) * LOG2E
        obcd = _local_mix_call(swa_sink[l], bqt, bk, bvt, cqt, ck, cvt, mask_b, ctab,
                               hd, sconv_w[l])

        xall = _out_ffn_call(
            xall, oa, obcd, mod, w_out_l, norm_ffn_w[l][None, :],
            ffn_w_up[l].astype(BF16), ffn_conv_w[l], ffn_conv_b[l][None, :],
            ffn_w_down[l].astype(BF16), final_norm_w[None, :], final=(l == depth - 1))
    return xall
```

```python
import functools
import math

import numpy as np
import jax
import jax.numpy as jnp
from jax import lax
from jax.experimental import pallas as pl
from jax.experimental.pallas import tpu as pltpu

D_MODEL = 1024
DEPTH = 4
GRID_W = 64
CTX_LEN = 256
HEAD_DIM = 64
GROUP_WIDTH = 256
DIFF_HEADS = 4
DIFF_QK_DIM = 32
SWA_HEADS = 4
SWA_KV_HEADS = 2
SWA_WINDOW = 128
NA_HEADS = 4
NA_ROWS = 8
NA_COLS = 16
D_FF = 2816
IN_COLS = 2816
ROPE_THETA = 10000.0
EPS = 1e-6
NEG_INF = -1e30
LOG2E = math.log2(math.e)

TILE = 256
HALO_F32 = 8
HALO_BF16 = 16
LANES = 128
VMEM_LIMIT = 56 * 1024 * 1024

F32 = jnp.float32
BF16 = jnp.bfloat16


def _params(n_grid, flags=None):
    return pltpu.CompilerParams(
        dimension_semantics=("arbitrary",) * n_grid, vmem_limit_bytes=VMEM_LIMIT, flags=flags)


def _resident(shape):
    nd = len(shape)
    return pl.BlockSpec(shape, lambda *_: (0,) * nd, pipeline_mode=pl.Buffered(1))


def _mod_spec(n_batch):
    return pl.BlockSpec((1, 1, 6 * D_MODEL), lambda b, j: (jnp.where(j == 0, n_batch, b), 0, 0))


def _rmsnorm_mod(x, w, shift, scale):
    ms = jnp.mean(x * x, axis=-1, keepdims=True)
    return (x * lax.rsqrt(ms + EPS) * w) * (1.0 + scale) + shift


def _dot(a, b):
    return jnp.dot(a, b, preferred_element_type=F32)


def _dot_nt(a, b):
    return lax.dot_general(a, b, (((1,), (1,)), ((), ())), preferred_element_type=F32)


MOD_BLOCK = 1536


def _mod_kernel(c_ref, w_ref, b_ref, o_ref):
    c = c_ref[...]
    a = c * (1.0 / (1.0 + jnp.exp(-c)))
    o_ref[0] = _dot(a.astype(BF16), w_ref[0].astype(BF16)) + b_ref[0]


def _mod_call(cin, w_mod, b_mod):
    depth = w_mod.shape[0]
    n = 6 * D_MODEL
    return pl.pallas_call(
        _mod_kernel,
        grid=(depth, n // MOD_BLOCK),
        in_specs=[
            pl.BlockSpec((8, D_MODEL), lambda l, k: (0, 0)),
            pl.BlockSpec((1, D_MODEL, MOD_BLOCK), lambda l, k: (l, 0, k)),
            pl.BlockSpec((1, 1, MOD_BLOCK), lambda l, k: (l, 0, k)),
        ],
        out_specs=pl.BlockSpec((1, 8, MOD_BLOCK), lambda l, k: (l, 0, k)),
        out_shape=jax.ShapeDtypeStruct((depth, 8, n), F32),
        compiler_params=_params(2),
        name="mod_proj",
    )(cin, w_mod, b_mod.reshape(depth, 1, n))


def _rope(x, cos, sin, half):
    lane = lax.broadcasted_iota(jnp.int32, x.shape, 1)
    first = (lane % (2 * half)) < half
    partner = jnp.where(first, pltpu.roll(x, LANES - half, 1), pltpu.roll(x, half, 1))
    return x * cos + partner * sin


def _in_proj_kernel(x_ref, mod_ref, nw_ref, w_ref, cosa_ref, sina_ref, cosb_ref, sinb_ref,
                    qa_ref, ka_ref, vat_ref, bqt_ref, bk_ref, bvt_ref, cqt_ref, ck_ref, cvt_ref,
                    hd_ref):
    mod = mod_ref[0]
    shift = mod[:, 0:D_MODEL]
    scale = mod[:, D_MODEL:2 * D_MODEL]
    z = _rmsnorm_mod(x_ref[0], nw_ref[...], shift, scale)
    h = _dot(z.astype(BF16), w_ref[...])

    def sec(k, n=1):
        return h[:, k * LANES:(k + n) * LANES]

    cosa, sina = cosa_ref[...], sina_ref[...]
    cosb, sinb = cosb_ref[...], sinb_ref[...]
    qa_scale = DIFF_QK_DIM ** -0.5 * LOG2E
    q_scale = HEAD_DIM ** -0.5 * LOG2E
    ha = DIFF_QK_DIM // 4
    hb = HEAD_DIM // 4
    for t in range(2):
        qa_ref[0, :, t * LANES:(t + 1) * LANES] = (
            _rope(sec(t), cosa, sina, ha) * qa_scale).astype(BF16)
        ka_ref[0, :, t * LANES:(t + 1) * LANES] = _rope(sec(2 + t), cosa, sina, ha).astype(BF16)
    vat_ref[0, 0] = sec(4, 2).T.astype(BF16)
    for t in range(2):
        bqt_ref[0, 0, t * LANES:(t + 1) * LANES, :] = (
            _rope(sec(6 + t), cosb, sinb, hb) * q_scale).T.astype(BF16)
    bk_ref[0] = _rope(sec(8), cosb, sinb, hb).astype(BF16)
    bvt_ref[0, 0] = sec(9).T.astype(BF16)
    cqt_ref[0, 0] = (sec(10, 2) * q_scale).T.astype(BF16)
    ck_ref[0] = sec(12, 2).astype(BF16)
    cvt_ref[0, 0] = sec(14, 2).T.astype(BF16)
    hd_ref[0] = sec(16, 6).astype(BF16)


def _in_proj_call(xall, mod, norm_w, w_in, tabs):
    nb, rows, _ = xall.shape
    nt = rows // TILE
    tile = lambda w: pl.BlockSpec((1, TILE, w), lambda b, j: (b, j, 0))
    ttile = lambda w: pl.BlockSpec((1, 1, w, TILE), lambda b, j: (b, j, 0, 0))
    tab = pl.BlockSpec((TILE, LANES), lambda b, j: (j, 0))
    sds = lambda w: jax.ShapeDtypeStruct((nb, rows, w), BF16)
    tsds = lambda w: jax.ShapeDtypeStruct((nb, nt, w, TILE), BF16)
    return pl.pallas_call(
        _in_proj_kernel,
        grid=(nb, nt),
        in_specs=[
            pl.BlockSpec((1, TILE, D_MODEL), lambda b, j: (b, j, 0)),
            _mod_spec(nb),
            _resident((1, D_MODEL)),
            _resident((D_MODEL, IN_COLS)),
            tab, tab, tab, tab,
        ],
        out_specs=[
            tile(256), tile(256), ttile(256),
            ttile(256), tile(128), ttile(128),
            ttile(256), tile(256), ttile(256),
            tile(768),
        ],
        out_shape=[
            sds(256), sds(256), tsds(256),
            tsds(256), sds(128), tsds(128),
            tsds(256), sds(256), tsds(256),
            sds(768),
        ],
        compiler_params=_params(2),
        name="in_proj",
    )(xall, mod, norm_w, w_in, *tabs)


QA_TILE = 256
KV_BLOCK = 512
BLOCKS_PER_STEP = 16
LOOKAHEAD = 2
EXP_CAP = 100.0
PV_ROWS = HEAD_DIM + HALO_BF16


def _diff_attn_kernel(li_ref, lv_ref, q_ref, k_ref, vt_ref, sw_ref, o_ref,
                      qst_ref, sa_ref, sb_ref, mca_ref, mcb_ref, m_ref, acc_ref, gap_ref):
    i = pl.program_id(1)
    s_refs = (sa_ref, sb_ref)
    mc_refs = (mca_ref, mcb_ref)
    q = q_ref[0].astype(F32)
    lane = lax.broadcasted_iota(jnp.int32, q.shape, 1) // DIFF_QK_DIM
    for h in range(DIFF_HEADS):
        qs = jnp.concatenate([jnp.where(lane == 2 * h + c, q, 0.0) for c in range(2)], axis=0)
        qst_ref[h] = qs.T.astype(BF16)
    n_cols = 2 * QA_TILE
    n_blk = (k_ref.shape[1] - CTX_LEN) // KV_BLOCK
    tiles_per_blk = KV_BLOCK // TILE

    def value_rows(vt, h):
        ones = jnp.ones((HALO_BF16, vt.shape[1]), BF16)
        return jnp.concatenate([vt[h * HEAD_DIM:(h + 1) * HEAD_DIM], ones], axis=0)

    def ctx_scores(h):
        return _dot(k_ref[0, 0:CTX_LEN, :], qst_ref[h])

    def ctx_init(h, s):
        m0 = jnp.max(s, axis=0, keepdims=True)
        acc_ref[h] = _dot(value_rows(vt_ref[0, 0], h), jnp.exp2(s - m0).astype(BF16))
        m_ref[h] = m0

    def score_head(blk, slot, h):
        row0 = pl.multiple_of(CTX_LEN + (blk - 1) * KV_BLOCK, TILE)
        s = _dot(k_ref[0, pl.ds(row0, KV_BLOCK), :], qst_ref[h])
        s_refs[slot][h] = s
        mc_refs[slot][h] = jnp.max(s, axis=0, keepdims=True)

    def block_step(score_blk, acc_blk):
        if acc_blk is not None:
            ablk, aslot = acc_blk
            t0 = 1 + (ablk - 1) * tiles_per_blk
            vt = jnp.concatenate([vt_ref[0, t0 + u] for u in range(tiles_per_blk)], axis=1)
        for h in range(DIFF_HEADS):
            if score_blk is not None:
                score_head(*score_blk, h)
            if acc_blk is not None:
                m_old = m_ref[h]
                m_new = jnp.maximum(m_old, mc_refs[aslot][h])
                p = jnp.exp2(s_refs[aslot][h] - m_new).astype(BF16)
                acc_ref[h] = acc_ref[h] * jnp.exp2(m_old - m_new) + _dot(value_rows(vt, h), p)
                m_ref[h] = m_new

    def two_pass_latent():
        scores0 = [ctx_scores(h) for h in range(DIFF_HEADS)]
        for h in range(DIFF_HEADS):
            score_head(1, 0, h)
            ctx_init(h, scores0[h])

        def step(it, carry):
            b = 2 * it + 1
            block_step((b + 1, 1), (b, 0))
            block_step((b + 2, 0), (b + 1, 1))
            return carry

        lax.fori_loop(0, n_blk // 2 - 1, step, 0)
        block_step((n_blk, 1), (n_blk - 1, 0))
        block_step(None, (n_blk, 1))

    def one_pass_latent():
        scores0 = [ctx_scores(h) for h in range(DIFF_HEADS)]
        for h in range(DIFF_HEADS):
            ctx_init(h, scores0[h])
            gap_ref[h] = jnp.zeros(gap_ref.shape[1:], F32)

        def scores(unit):
            blk, h, c = unit
            row0 = pl.multiple_of(CTX_LEN + (blk - 1) * KV_BLOCK, TILE)
            return _dot(k_ref[0, pl.ds(row0, KV_BLOCK), :],
                        qst_ref[h, :, c * QA_TILE:(c + 1) * QA_TILE])

        def consume(unit, s, vt):
            _, h, c = unit
            cols = slice(c * QA_TILE, (c + 1) * QA_TILE)
            m_old = m_ref[h, :, cols]
            m_cur = jnp.max(s, axis=0, keepdims=True)
            pv = _dot(value_rows(vt, h), jnp.exp2(s - m_old).astype(BF16))
            m_new = jnp.maximum(m_old, m_cur)
            acc_ref[h, :, cols] = (acc_ref[h, :, cols] + pv) * jnp.exp2(m_old - m_new)
            m_ref[h, :, cols] = m_new
            gap_ref[h, :, cols] = jnp.maximum(gap_ref[h, :, cols], m_cur - m_old)

        def step(it, carry):
            b0 = it * BLOCKS_PER_STEP + 1
            units = [(b0 + u, h, c) for u in range(BLOCKS_PER_STEP)
                     for h in range(DIFF_HEADS) for c in range(2)]
            pending = [scores(u) for u in units[:LOOKAHEAD]]
            for n, unit in enumerate(units):
                if n + LOOKAHEAD < len(units):
                    pending.append(scores(units[n + LOOKAHEAD]))
                if unit[1] == 0 and unit[2] == 0:
                    t0 = 1 + (unit[0] - 1) * tiles_per_blk
                    vt = jnp.concatenate(
                        [vt_ref[0, t0 + u] for u in range(tiles_per_blk)], axis=1)
                consume(unit, pending.pop(0), vt)
            return carry

        lax.fori_loop(0, n_blk // BLOCKS_PER_STEP, step, 0)

    def finalize():
        lv = lv_ref[...]
        lam = (jnp.exp(jnp.sum(lv[0:1] * lv[1:2], axis=1, keepdims=True))
               - jnp.exp(jnp.sum(lv[2:3] * lv[3:4], axis=1, keepdims=True)) + li_ref[0])
        out_scale = li_ref[1]
        outs = []
        for h in range(DIFF_HEADS):
            a = acc_ref[h]
            o1 = a[0:HEAD_DIM, 0:QA_TILE] / a[HEAD_DIM:HEAD_DIM + 1, 0:QA_TILE]
            o2 = a[0:HEAD_DIM, QA_TILE:n_cols] / a[HEAD_DIM:HEAD_DIM + 1, QA_TILE:n_cols]
            o = o1 - lam * o2
            ms = jnp.mean(o * o, axis=0, keepdims=True)
            outs.append(o * lax.rsqrt(ms + EPS) * sw_ref[...] * out_scale)
        o_ref[0] = jnp.concatenate(outs, axis=0).T.astype(BF16)

    @pl.when(i < CTX_LEN // QA_TILE)
    def _():
        for h in range(DIFF_HEADS):
            ctx_init(h, ctx_scores(h))
        finalize()

    @pl.when(i >= CTX_LEN // QA_TILE)
    def _():
        one_pass_latent()
        finalize()

        @pl.when(jnp.max(gap_ref[...]) > EXP_CAP)
        def _():
            two_pass_latent()
            finalize()


def _diff_attn_call(lam_init, lam_vecs, qa, ka, vat, subw):
    nb, rows, _ = qa.shape
    nt = rows // TILE
    return pl.pallas_call(
        _diff_attn_kernel,
        grid=(nb, rows // QA_TILE),
        in_specs=[
            pl.BlockSpec(memory_space=pltpu.SMEM),
            _resident((4, DIFF_QK_DIM)),
            pl.BlockSpec((1, QA_TILE, 256), lambda b, i: (b, i, 0)),
            pl.BlockSpec((1, rows, 256), lambda b, i: (b, 0, 0)),
            pl.BlockSpec((1, nt, 256, TILE), lambda b, i: (b, 0, 0, 0)),
            _resident((HEAD_DIM, QA_TILE)),
        ],
        out_specs=pl.BlockSpec((1, QA_TILE, 256), lambda b, i: (b, i, 0)),
        out_shape=jax.ShapeDtypeStruct((nb, rows, 256), BF16),
        scratch_shapes=[
            pltpu.VMEM((DIFF_HEADS, 256, 2 * QA_TILE), BF16),
            pltpu.VMEM((DIFF_HEADS, KV_BLOCK, 2 * QA_TILE), F32),
            pltpu.VMEM((DIFF_HEADS, KV_BLOCK, 2 * QA_TILE), F32),
            pltpu.VMEM((DIFF_HEADS, 1, 2 * QA_TILE), F32),
            pltpu.VMEM((DIFF_HEADS, 1, 2 * QA_TILE), F32),
            pltpu.VMEM((DIFF_HEADS, 1, 2 * QA_TILE), F32),
            pltpu.VMEM((DIFF_HEADS, PV_ROWS, 2 * QA_TILE), F32),
            pltpu.VMEM((DIFF_HEADS, 1, 2 * QA_TILE), F32),
        ],
        compiler_params=_params(2),
        name="diff_attn",
    )(lam_init, lam_vecs, qa, ka, vat, subw)


MIX_LOOKAHEAD = 2


def _head_rows(qt, k):
    row = lax.broadcasted_iota(jnp.int32, qt.shape, 0) // HEAD_DIM
    return jnp.where(row == k, qt, jnp.zeros_like(qt))


def _value_rows(vt_parts, rows):
    v = jnp.concatenate([part[rows] for part in vt_parts], axis=1)
    return jnp.concatenate([v, jnp.ones((HALO_BF16, v.shape[1]), BF16)], axis=0)


def _local_mix_kernel(sink_ref, bqt_ref, bkp_ref, bkc_ref, bkn_ref, bkx_ref,
                      bvtp_ref, bvtc_ref, bvtn_ref, bvtx_ref,
                      cqt_ref, ckp_ref, ckc_ref, ckn_ref, ckx_ref,
                      cvtp_ref, cvtc_ref, cvtn_ref, cvtx_ref, bmask_ref, ctab_ref,
                      dp_ref, dc_ref, dn_ref, cw_ref, o_ref, s_ref):
    j = pl.program_id(1)
    n_tiles = pl.num_programs(1)
    half = SWA_WINDOW

    k_b = jnp.concatenate([bkp_ref[0], bkc_ref[0], bkn_ref[0], bkx_ref[0]], axis=0)
    vt_b = [bvtp_ref[0, 0][:, half:], bvtc_ref[0, 0], bvtn_ref[0, 0][:, :half], bvtx_ref[0, 0]]
    vt_c = [cvtp_ref[0, 0], cvtc_ref[0, 0], cvtn_ref[0, 0], cvtx_ref[0, 0]]

    units = [(mixer, p, k) for mixer in "BC" for p in range(2) for k in range(2)]

    def scores(unit):
        mixer, p, k = unit
        rows = slice(p * LANES, (p + 1) * LANES)
        if mixer == "B":
            return _dot(k_b, _head_rows(bqt_ref[0, 0, rows, :], k))
        k_c = jnp.concatenate([ckp_ref[0, :, rows], ckc_ref[0, :, rows],
                               ckn_ref[0, :, rows], ckx_ref[0, :, rows]], axis=0)
        return _dot(k_c, _head_rows(cqt_ref[0, 0, rows, :], k))

    def attend(unit, s):
        mixer, p, k = unit
        feat = slice(k * HEAD_DIM, (k + 1) * HEAD_DIM)
        if mixer == "B":
            s = s + bmask_ref[0]
            sink = sink_ref[2 * k + p] * LOG2E
            m = jnp.maximum(jnp.max(s, axis=0, keepdims=True), sink)
            pv = _dot(_value_rows(vt_b, feat), jnp.exp2(s - m).astype(BF16))
            return pv[0:HEAD_DIM] / (pv[HEAD_DIM:HEAD_DIM + 1] + jnp.exp2(sink - m))
        s = s + ctab_ref[0, 2 * p + k]
        m = jnp.max(s, axis=0, keepdims=True)
        rows = slice(p * LANES + k * HEAD_DIM, p * LANES + (k + 1) * HEAD_DIM)
        pv = _dot(_value_rows(vt_c, rows), jnp.exp2(s - m).astype(BF16))
        return pv[0:HEAD_DIM] / pv[HEAD_DIM:HEAD_DIM + 1]

    def park(n):
        s = scores(units[n])
        s_ref[n % (MIX_LOOKAHEAD + 1), 0:s.shape[0], :] = s

    for n in range(MIX_LOOKAHEAD):
        park(n)
    outs = []
    for n, unit in enumerate(units):
        if n + MIX_LOOKAHEAD < len(units):
            park(n + MIX_LOOKAHEAD)
        n_keys = (TILE + 2 * SWA_WINDOW if unit[0] == "B" else 3 * TILE) + CTX_LEN
        outs.append(attend(unit, s_ref[n % (MIX_LOOKAHEAD + 1), 0:n_keys, :]))
    o_ref[0, :, 0:GROUP_WIDTH] = jnp.concatenate(outs[0:4], axis=0).T.astype(BF16)
    o_ref[0, :, GROUP_WIDTH:2 * GROUP_WIDTH] = jnp.concatenate(outs[4:8], axis=0).T.astype(BF16)

    has_prev = (j > 1).astype(F32)
    has_next = ((j > 0) & (j < n_tiles - 1)).astype(F32)
    dc = dc_ref[0].astype(F32)
    dp = dp_ref[0].astype(F32)[HALO_BF16 - HALO_F32:]
    dn = dn_ref[0].astype(F32)[:HALO_F32]
    gw = GROUP_WIDTH
    z = jnp.concatenate([dp[:, gw:2 * gw] * dp[:, 2 * gw:] * has_prev,
                         dc[:, gw:2 * gw] * dc[:, 2 * gw:],
                         dn[:, gw:2 * gw] * dn[:, 2 * gw:] * has_next], axis=0)
    cw = cw_ref[...]
    c0 = HALO_F32
    conv = (z[c0 - 1:c0 - 1 + TILE] * cw[0:1] + z[c0:c0 + TILE] * cw[1:2]
            + z[c0 + 1:c0 + 1 + TILE] * cw[2:3])
    o_ref[0, :, 4 * LANES:6 * LANES] = (dc[:, 0:gw] * conv).astype(BF16)


def _tile_variants(nt):
    return (0, 1, 2, nt - 1)


def _local_mix_call(sink, bqt, bk, bvt, cqt, ck, cvt, bmask, ctab, hd, conv_w):
    nb, rows, _ = bk.shape
    nt = rows // TILE
    variant = lambda j: jnp.where(j == 0, 0, jnp.where(j == 1, 1, jnp.where(j == nt - 1, 3, 2)))
    hb = TILE // SWA_WINDOW
    h16 = TILE // HALO_BF16
    prev_j = lambda j: jnp.maximum(j - 1, 0)
    next_j = lambda j: jnp.minimum(j + 1, nt - 1)
    cur = lambda w: pl.BlockSpec((1, TILE, w), lambda b, j: (b, j, 0))
    prev = lambda w: pl.BlockSpec((1, TILE, w), lambda b, j: (b, prev_j(j), 0))
    nxt = lambda w: pl.BlockSpec((1, TILE, w), lambda b, j: (b, next_j(j), 0))
    ctx = lambda w: pl.BlockSpec((1, TILE, w), lambda b, j: (b, 0, 0))
    tcur = lambda w: pl.BlockSpec((1, 1, w, TILE), lambda b, j: (b, j, 0, 0))
    tprev = lambda w: pl.BlockSpec((1, 1, w, TILE), lambda b, j: (b, prev_j(j), 0, 0))
    tnxt = lambda w: pl.BlockSpec((1, 1, w, TILE), lambda b, j: (b, next_j(j), 0, 0))
    tctx = lambda w: pl.BlockSpec((1, 1, w, TILE), lambda b, j: (b, 0, 0, 0))
    return pl.pallas_call(
        _local_mix_kernel,
        grid=(nb, nt),
        in_specs=[
            pl.BlockSpec(memory_space=pltpu.SMEM),
            tcur(256),
            pl.BlockSpec((1, SWA_WINDOW, 128), lambda b, j: (b, jnp.maximum(j * hb - 1, 0), 0)),
            cur(128),
            pl.BlockSpec((1, SWA_WINDOW, 128), lambda b, j: (b, jnp.minimum((j + 1) * hb, nt * hb - 1), 0)),
            ctx(128),
            tprev(128), tcur(128), tnxt(128), tctx(128),
            tcur(256),
            prev(256), cur(256), nxt(256), ctx(256),
            tprev(256), tcur(256), tnxt(256), tctx(256),
            pl.BlockSpec((1, TILE + 2 * SWA_WINDOW + CTX_LEN, TILE), lambda b, j: (variant(j), 0, 0)),
            pl.BlockSpec((1, NA_HEADS, 3 * TILE + CTX_LEN, TILE),
                         lambda b, j: (variant(j), 0, 0, 0)),
            pl.BlockSpec((1, HALO_BF16, 768), lambda b, j: (b, jnp.maximum(j * h16 - 1, 0), 0)),
            cur(768),
            pl.BlockSpec((1, HALO_BF16, 768), lambda b, j: (b, jnp.minimum((j + 1) * h16, nt * h16 - 1), 0)),
            _resident((3, GROUP_WIDTH)),
        ],
        out_specs=pl.BlockSpec((1, TILE, 768), lambda b, j: (b, j, 0)),
        out_shape=jax.ShapeDtypeStruct((nb, rows, 768), BF16),
        scratch_shapes=[pltpu.VMEM((MIX_LOOKAHEAD + 1, 3 * TILE + CTX_LEN, TILE), F32)],
        compiler_params=_params(2),
        name="local_mix",
    )(sink, bqt, bk, bk, bk, bk, bvt, bvt, bvt, bvt, cqt, ck, ck, ck, ck, cvt, cvt, cvt, cvt,
      bmask, ctab, hd, hd, hd, conv_w)


MXU_WIDTH = 256
FFN_CHUNKS = ((0, 6 * MXU_WIDTH), (6 * MXU_WIDTH, D_FF))
FFN_CHUNK_MAX = max(hi - lo for lo, hi in FFN_CHUNKS)
EXT = TILE + 2 * HALO_F32


def _halo_f32(prev_ref, next_ref):
    p = prev_ref[0].astype(F32)
    n = next_ref[0].astype(F32)
    return p[p.shape[0] - HALO_F32:], n[:HALO_F32]


def _ffn_tile(final, j, n_tiles, xp_ref, xc_ref, xn_ref, ap_ref, ac_ref, an_ref,
              mp_ref, mc_ref, mn_ref, mod_ref, wo_ref, nw_ref, wu_ref, cw_ref, cb_ref, wd_ref,
              fw_ref, u_ref):
    mod = mod_ref[0]
    g1 = mod[:, 2 * D_MODEL:3 * D_MODEL]
    sh2 = mod[:, 3 * D_MODEL:4 * D_MODEL]
    sc2 = mod[:, 4 * D_MODEL:5 * D_MODEL]
    g2 = mod[:, 5 * D_MODEL:6 * D_MODEL]

    ap, an = _halo_f32(ap_ref, an_ref)
    mp, mn = _halo_f32(mp_ref, mn_ref)
    mix = jnp.concatenate([
        jnp.concatenate([ap, mp], axis=1),
        jnp.concatenate([ac_ref[0], mc_ref[0]], axis=1).astype(F32),
        jnp.concatenate([an, mn], axis=1)], axis=0).astype(BF16)
    x = jnp.concatenate([xp_ref[0], xc_ref[0], xn_ref[0]], axis=0)
    x1 = x + g1 * _dot(mix, wo_ref[...])
    f = _rmsnorm_mod(x1, nw_ref[...], sh2, sc2).astype(BF16)

    has_prev = (j > 1).astype(F32)
    has_next = ((j > 0) & (j < n_tiles - 1)).astype(F32)
    lo = HALO_F32 - 1
    hi = HALO_F32 + TILE
    y = None
    for c0, c1 in FFN_CHUNKS:
        n = c1 - c0
        halves = []
        for base in (c0, D_FF + c0):
            u = _dot(f, wu_ref[:, base:base + n])
            slabs = []
            for t in range(n // LANES):
                cols = slice(base + t * LANES, base + (t + 1) * LANES)
                u_ref[t] = u[:, t * LANES:(t + 1) * LANES]
                u_ref[t, lo:lo + 1, :] = u_ref[t, lo:lo + 1, :] * has_prev
                u_ref[t, hi:hi + 1, :] = u_ref[t, hi:hi + 1, :] * has_next
                cw = cw_ref[:, cols]
                slabs.append(u_ref[t, pl.ds(lo, TILE), :] * cw[0:1]
                             + u_ref[t, pl.ds(lo + 1, TILE), :] * cw[1:2]
                             + u_ref[t, pl.ds(lo + 2, TILE), :] * cw[2:3] + cb_ref[:, cols])
            halves.append(jnp.concatenate(slabs, axis=1))
        g, v = halves
        act = (g * (1.0 / (1.0 + jnp.exp(-g))) * v).astype(BF16)
        part = _dot(act, wd_ref[c0:c1, :])
        y = part if y is None else y + part
    x2 = x1[HALO_F32:HALO_F32 + TILE] + g2 * y
    if final:
        ms = jnp.mean(x2 * x2, axis=-1, keepdims=True)
        x2 = x2 * lax.rsqrt(ms + EPS) * fw_ref[...]
    return x2


N_TILE_REFS = 10


def _out_ffn_kernel(final, *refs):
    tile_refs, weights, (o_ref, u_ref) = refs[:N_TILE_REFS], refs[N_TILE_REFS:-2], refs[-2:]
    o_ref[0] = _ffn_tile(final, pl.program_id(1), pl.num_programs(1), *tile_refs, *weights, u_ref)


def _out_ffn_call(xall, oa, obcd, mod, w_out, norm_w, w_up, conv_w, conv_b, w_down, final_w, final):
    nb, rows, _ = xall.shape
    nt = rows // TILE
    h8 = TILE // HALO_F32
    h16 = TILE // HALO_BF16

    def tile_specs(tile_of):
        def trio(w, halo, per_tile):
            return [
                pl.BlockSpec((1, halo, w),
                             lambda b, s: (b, jnp.maximum(tile_of(s) * per_tile - 1, 0), 0)),
                pl.BlockSpec((1, TILE, w), lambda b, s: (b, tile_of(s), 0)),
                pl.BlockSpec((1, halo, w), lambda b, s: (
                    b, jnp.minimum((tile_of(s) + 1) * per_tile, nt * per_tile - 1), 0)),
            ]
        mod_spec = pl.BlockSpec((1, 1, 6 * D_MODEL),
                                lambda b, s: (jnp.where(tile_of(s) == 0, nb, b), 0, 0))
        return [*trio(D_MODEL, HALO_F32, h8), *trio(256, HALO_BF16, h16),
                *trio(768, HALO_BF16, h16), mod_spec]

    weight_specs = [
        _resident((D_MODEL, D_MODEL)),
        _resident((1, D_MODEL)),
        _resident((D_MODEL, 2 * D_FF)),
        _resident((3, 2 * D_FF)),
        _resident((1, 2 * D_FF)),
        _resident((D_FF, D_MODEL)),
        _resident((1, D_MODEL)),
    ]
    if final:
        out_spec = pl.BlockSpec((1, TILE, D_MODEL), lambda b, j: (b, jnp.maximum(j - 1, 0), 0))
        out_shape = jax.ShapeDtypeStruct((nb, rows - CTX_LEN, D_MODEL), F32)
    else:
        out_spec = pl.BlockSpec((1, TILE, D_MODEL), lambda b, j: (b, j, 0))
        out_shape = jax.ShapeDtypeStruct((nb, rows, D_MODEL), F32)
    return pl.pallas_call(
        functools.partial(_out_ffn_kernel, final),
        grid=(nb, nt),
        in_specs=[*tile_specs(lambda s: s), *weight_specs],
        out_specs=out_spec,
        out_shape=out_shape,
        scratch_shapes=[pltpu.VMEM((FFN_CHUNK_MAX // LANES, EXT, LANES), F32)],
        compiler_params=_params(2),
        name="out_ffn_final" if final else "out_ffn",
    )(xall, xall, xall, oa, oa, oa, obcd, obcd, obcd, mod, w_out, norm_w, w_up, conv_w, conv_b,
      w_down, final_w)


def _rope_tables(n_lat, dim):
    quarter = dim // 4
    t = np.arange(n_lat)
    pos = np.stack([t // GRID_W, t % GRID_W], axis=1).astype(np.float32)
    freqs = jnp.asarray(ROPE_THETA, F32) ** (-(jnp.arange(quarter, dtype=F32) / quarter))
    lane = np.arange(LANES) % dim
    axis = lane // (dim // 2)
    fidx = lane % quarter
    sign = np.where((lane % (dim // 2)) < quarter, -1.0, 1.0).astype(np.float32)
    ang = jnp.asarray(pos)[:, axis] * freqs[fidx][None, :]
    cos = jnp.concatenate([jnp.ones((CTX_LEN, LANES), F32), jnp.cos(ang)], axis=0)
    sin = jnp.concatenate([jnp.zeros((CTX_LEN, LANES), F32), jnp.sin(ang) * sign[None, :]], axis=0)
    return cos, sin


def _na_bias_table(rpb):
    rpt = TILE // GRID_W
    n_heads, n_dr, n_dc = rpb.shape
    cq = np.arange(GRID_W)[:, None]
    ck = np.arange(GRID_W)[None, :]
    dc = np.clip(ck - cq, -(NA_COLS - 1), NA_COLS - 1) + NA_COLS - 1
    col_start = np.clip(cq - NA_COLS // 2, 0, GRID_W - NA_COLS)
    col_ok = (ck >= col_start) & (ck < col_start + NA_COLS)
    pick = dc[None, :, :] == np.arange(n_dc)[:, None, None]
    e = jnp.sum(jnp.where(pick[None, None], rpb.astype(F32)[:, :, :, None, None], 0.0), axis=2)
    e = jnp.where(col_ok[None, None], e, NEG_INF)
    blocks = []
    for rq in range(rpt):
        lo = NA_ROWS - 1 - rpt - rq
        blocks.append(e[:, lo:lo + 3 * rpt].transpose(0, 1, 3, 2))
    local = jnp.stack(blocks, axis=3).reshape(n_heads, 3 * TILE, TILE)
    return jnp.concatenate([local, jnp.zeros((n_heads, CTX_LEN, TILE), F32)], axis=1)


def _window_masks(n_lat):
    nt = 1 + n_lat // TILE
    half = SWA_WINDOW
    rpt = TILE // GRID_W
    n_rows = n_lat // GRID_W
    rq = np.arange(TILE)[None, :]
    kb = np.arange(TILE + 2 * half + CTX_LEN)[:, None]
    kc = np.arange(3 * TILE + CTX_LEN)[:, None]
    masks_b, masks_c = [], []
    for j in _tile_variants(nt):
        kpos = (j - 1) * TILE - half + kb
        band = (np.abs(kb - half - rq) <= SWA_WINDOW) & (kpos >= 0) & (kpos < n_lat) & (j > 0)
        masks_b.append(band | (kb >= TILE + 2 * half))
        r = (j - 1) * rpt + rq // GRID_W
        kr = (j - 2) * rpt + kc // GRID_W
        rs = np.clip(r - NA_ROWS // 2, 0, n_rows - NA_ROWS)
        masks_c.append(((kr >= rs) & (kr < rs + NA_ROWS) & (j > 0)) | (kc >= 3 * TILE))
    to_add = lambda ms: jnp.where(jnp.asarray(np.stack(ms)), 0.0, NEG_INF).astype(F32)
    return to_add(masks_b), to_add(masks_c)


def _swa_head_order(w, axis, start):
    def blk(lo, hi):
        return lax.slice_in_dim(w, lo, hi, axis=axis)
    heads = [blk(start + h * HEAD_DIM, start + (h + 1) * HEAD_DIM) for h in (0, 2, 1, 3)]
    return jnp.concatenate(
        [blk(0, start), *heads, blk(start + SWA_HEADS * HEAD_DIM, w.shape[axis])], axis=axis)


def kernel(x, c, ctx, c_ctx, norm_mix_w, norm_ffn_w, w_mod, b_mod, w_in, w_out, diff_lambda_q1, diff_lambda_k1, diff_lambda_q2, diff_lambda_k2, diff_subln_w, swa_sink, na_rpb, sconv_w, ffn_w_up, ffn_conv_w, ffn_conv_b, ffn_w_down, final_norm_w):
    nb, n_lat, d = x.shape
    depth = w_in.shape[0]
    assert d == D_MODEL and ctx.shape[1] == CTX_LEN and n_lat % TILE == 0 and nb < 8
    assert n_lat // GRID_W >= NA_ROWS and n_lat % (max(2, BLOCKS_PER_STEP) * KV_BLOCK) == 0

    xall = jnp.concatenate([ctx, x], axis=1)
    cin = jnp.concatenate([c, c_ctx[None, :], jnp.zeros((8 - nb - 1, d), F32)], axis=0)
    mods = _mod_call(cin, w_mod, b_mod)

    tabs = (*_rope_tables(n_lat, DIFF_QK_DIM), *_rope_tables(n_lat, HEAD_DIM))
    mask_b, mask_c = _window_masks(n_lat)
    for l in range(depth):
        mod = mods[l].reshape(8, 1, 6 * D_MODEL)
        w_in_l = _swa_head_order(w_in[l], 1, 3 * GROUP_WIDTH).astype(BF16)
        w_out_l = _swa_head_order(w_out[l], 0, GROUP_WIDTH).astype(BF16)
        qa, ka, vat, bqt, bk, bvt, cqt, ck, cvt, hd = _in_proj_call(
            xall, mod, norm_mix_w[l][None, :], w_in_l, tabs)

        lambda_init = 0.8 - 0.6 * math.exp(-0.3 * l)
        lam_init = jnp.asarray([lambda_init, 1.0 - lambda_init], F32)
        lam_vecs = jnp.stack([diff_lambda_q1[l], diff_lambda_k1[l],
                              diff_lambda_q2[l], diff_lambda_k2[l]])
        subw = jnp.broadcast_to(diff_subln_w[l][:, None], (HEAD_DIM, QA_TILE))
        oa = _diff_attn_call(lam_init, lam_vecs, qa, ka, vat, subw)

        ctab = (_na_bias_table(na_rpb[l])[None] + mask_c[:, None]) * LOG2E
        obcd = _local_mix_call(swa_sink[l], bqt, bk, bvt, cqt, ck, cvt, mask_b, ctab,
                               hd, sconv_w[l])

        xall = _out_ffn_call(
            xall, oa, obcd, mod, w_out_l, norm_ffn_w[l][None, :],
            ffn_w_up[l].astype(BF16), ffn_conv_w[l], ffn_conv_b[l][None, :],
            ffn_w_down[l].astype(BF16), final_norm_w[None, :], final=(l == depth - 1))
    return xall
```

```python
import functools
import math

import numpy as np
import jax
import jax.numpy as jnp
from jax import lax
from jax.experimental import pallas as pl
from jax.experimental.pallas import tpu as pltpu

D_MODEL = 1024
GRID_W = 64
CTX_LEN = 256
HEAD_DIM = 64
GROUP_WIDTH = 256
DIFF_HEADS = 4
DIFF_QK_DIM = 32
SWA_HEADS = 4
SWA_WINDOW = 128
NA_HEADS = 4
NA_ROWS = 8
NA_COLS = 16
D_FF = 2816
IN_COLS = 2816
ROPE_THETA = 10000.0
EPS = 1e-6
NEG_INF = -1e30
LOG2E = math.log2(math.e)

TILE = 256
HALO_F32 = 8
HALO_BF16 = 16
LANES = 128
VMEM_LIMIT = 56 * 1024 * 1024

F32 = jnp.float32
BF16 = jnp.bfloat16


def _params(n_grid):
    return pltpu.CompilerParams(
        dimension_semantics=("arbitrary",) * n_grid, vmem_limit_bytes=VMEM_LIMIT)


def _resident(shape):
    nd = len(shape)
    return pl.BlockSpec(shape, lambda *_: (0,) * nd, pipeline_mode=pl.Buffered(1))


def _mod_spec(n_batch):
    return pl.BlockSpec((1, 1, 6 * D_MODEL), lambda b, j: (jnp.where(j == 0, n_batch, b), 0, 0))


def _rmsnorm_mod(x, w, shift, scale):
    ms = jnp.mean(x * x, axis=-1, keepdims=True)
    return (x * lax.rsqrt(ms + EPS) * w) * (1.0 + scale) + shift


def _dot(a, b):
    return jnp.dot(a, b, preferred_element_type=F32)


MOD_BLOCK = 1536


def _mod_kernel(c_ref, w_ref, b_ref, o_ref):
    c = c_ref[...]
    a = c * (1.0 / (1.0 + jnp.exp(-c)))
    o_ref[0] = _dot(a.astype(BF16), w_ref[0].astype(BF16)) + b_ref[0]


def _mod_call(cin, w_mod, b_mod):
    depth = w_mod.shape[0]
    n = 6 * D_MODEL
    return pl.pallas_call(
        _mod_kernel,
        grid=(depth, n // MOD_BLOCK),
        in_specs=[
            pl.BlockSpec((8, D_MODEL), lambda l, k: (0, 0)),
            pl.BlockSpec((1, D_MODEL, MOD_BLOCK), lambda l, k: (l, 0, k)),
            pl.BlockSpec((1, 1, MOD_BLOCK), lambda l, k: (l, 0, k)),
        ],
        out_specs=pl.BlockSpec((1, 8, MOD_BLOCK), lambda l, k: (l, 0, k)),
        out_shape=jax.ShapeDtypeStruct((depth, 8, n), F32),
        compiler_params=_params(2),
        name="mod_proj",
    )(cin, w_mod, b_mod.reshape(depth, 1, n))


def _rope(x, cos, sin, half):
    lane = lax.broadcasted_iota(jnp.int32, x.shape, 1)
    first = (lane % (2 * half)) < half
    partner = jnp.where(first, pltpu.roll(x, LANES - half, 1), pltpu.roll(x, half, 1))
    return x * cos + partner * sin


def _in_proj_kernel(x_ref, mod_ref, nw_ref, w_ref, cosa_ref, sina_ref, cosb_ref, sinb_ref,
                    qa_ref, ka_ref, vat_ref, bqt_ref, bk_ref, bvt_ref, cqt_ref, ck_ref, cvt_ref,
                    hd_ref):
    mod = mod_ref[0]
    shift = mod[:, 0:D_MODEL]
    scale = mod[:, D_MODEL:2 * D_MODEL]
    z = _rmsnorm_mod(x_ref[0], nw_ref[...], shift, scale)
    h = _dot(z.astype(BF16), w_ref[...])

    def sec(k, n=1):
        return h[:, k * LANES:(k + n) * LANES]

    cosa, sina = cosa_ref[...], sina_ref[...]
    cosb, sinb = cosb_ref[...], sinb_ref[...]
    qa_scale = DIFF_QK_DIM ** -0.5 * LOG2E
    q_scale = HEAD_DIM ** -0.5 * LOG2E
    ha = DIFF_QK_DIM // 4
    hb = HEAD_DIM // 4
    for t in range(2):
        qa_ref[0, :, t * LANES:(t + 1) * LANES] = (
            _rope(sec(t), cosa, sina, ha) * qa_scale).astype(BF16)
        ka_ref[0, :, t * LANES:(t + 1) * LANES] = _rope(sec(2 + t), cosa, sina, ha).astype(BF16)
    vat_ref[0, 0] = sec(4, 2).T.astype(BF16)
    for t in range(2):
        bqt_ref[0, 0, t * LANES:(t + 1) * LANES, :] = (
            _rope(sec(6 + t), cosb, sinb, hb) * q_scale).T.astype(BF16)
    bk_ref[0] = _rope(sec(8), cosb, sinb, hb).astype(BF16)
    bvt_ref[0, 0] = sec(9).T.astype(BF16)
    cqt_ref[0, 0] = (sec(10, 2) * q_scale).T.astype(BF16)
    ck_ref[0] = sec(12, 2).astype(BF16)
    cvt_ref[0, 0] = sec(14, 2).T.astype(BF16)
    hd_ref[0] = sec(16, 6).astype(BF16)


def _in_proj_call(xall, mod, norm_w, w_in, tabs):
    nb, rows, _ = xall.shape
    nt = rows // TILE
    tile = lambda w: pl.BlockSpec((1, TILE, w), lambda b, j: (b, j, 0))
    ttile = lambda w: pl.BlockSpec((1, 1, w, TILE), lambda b, j: (b, j, 0, 0))
    tab = pl.BlockSpec((TILE, LANES), lambda b, j: (j, 0))
    sds = lambda w: jax.ShapeDtypeStruct((nb, rows, w), BF16)
    tsds = lambda w: jax.ShapeDtypeStruct((nb, nt, w, TILE), BF16)
    return pl.pallas_call(
        _in_proj_kernel,
        grid=(nb, nt),
        in_specs=[
            pl.BlockSpec((1, TILE, D_MODEL), lambda b, j: (b, j, 0)),
            _mod_spec(nb),
            _resident((1, D_MODEL)),
            _resident((D_MODEL, IN_COLS)),
            tab, tab, tab, tab,
        ],
        out_specs=[
            tile(256), tile(256), ttile(256),
            ttile(256), tile(128), ttile(128),
            ttile(256), tile(256), ttile(256),
            tile(768),
        ],
        out_shape=[
            sds(256), sds(256), tsds(256),
            tsds(256), sds(128), tsds(128),
            tsds(256), sds(256), tsds(256),
            sds(768),
        ],
        compiler_params=_params(2),
        name="in_proj",
    )(xall, mod, norm_w, w_in, *tabs)


QA_TILE = 256
KV_BLOCK = 512
BLOCKS_PER_STEP = 16
LOOKAHEAD = 2
EXP_CAP = 100.0
PV_ROWS = HEAD_DIM + HALO_BF16


def _diff_attn_kernel(li_ref, lv_ref, q_ref, k_ref, vt_ref, sw_ref, o_ref,
                      qst_ref, sa_ref, sb_ref, mca_ref, mcb_ref, m_ref, acc_ref, gap_ref):
    i = pl.program_id(1)
    s_refs = (sa_ref, sb_ref)
    mc_refs = (mca_ref, mcb_ref)
    q = q_ref[0].astype(F32)
    lane = lax.broadcasted_iota(jnp.int32, q.shape, 1) // DIFF_QK_DIM
    for h in range(DIFF_HEADS):
        qs = jnp.concatenate([jnp.where(lane == 2 * h + c, q, 0.0) for c in range(2)], axis=0)
        qst_ref[h] = qs.T.astype(BF16)
    n_cols = 2 * QA_TILE
    n_blk = (k_ref.shape[1] - CTX_LEN) // KV_BLOCK
    tiles_per_blk = KV_BLOCK // TILE

    def value_rows(vt, h):
        ones = jnp.ones((HALO_BF16, vt.shape[1]), BF16)
        return jnp.concatenate([vt[h * HEAD_DIM:(h + 1) * HEAD_DIM], ones], axis=0)

    def ctx_scores(h):
        return _dot(k_ref[0, 0:CTX_LEN, :], qst_ref[h])

    def ctx_init(h, s):
        m0 = jnp.max(s, axis=0, keepdims=True)
        acc_ref[h] = _dot(value_rows(vt_ref[0, 0], h), jnp.exp2(s - m0).astype(BF16))
        m_ref[h] = m0

    def score_head(blk, slot, h):
        row0 = pl.multiple_of(CTX_LEN + (blk - 1) * KV_BLOCK, TILE)
        s = _dot(k_ref[0, pl.ds(row0, KV_BLOCK), :], qst_ref[h])
        s_refs[slot][h] = s
        mc_refs[slot][h] = jnp.max(s, axis=0, keepdims=True)

    def block_step(score_blk, acc_blk):
        if acc_blk is not None:
            ablk, aslot = acc_blk
            t0 = 1 + (ablk - 1) * tiles_per_blk
            vt = jnp.concatenate([vt_ref[0, t0 + u] for u in range(tiles_per_blk)], axis=1)
        for h in range(DIFF_HEADS):
            if score_blk is not None:
                score_head(*score_blk, h)
            if acc_blk is not None:
                m_old = m_ref[h]
                m_new = jnp.maximum(m_old, mc_refs[aslot][h])
                p = jnp.exp2(s_refs[aslot][h] - m_new).astype(BF16)
                acc_ref[h] = acc_ref[h] * jnp.exp2(m_old - m_new) + _dot(value_rows(vt, h), p)
                m_ref[h] = m_new

    def two_pass_latent():
        scores0 = [ctx_scores(h) for h in range(DIFF_HEADS)]
        for h in range(DIFF_HEADS):
            score_head(1, 0, h)
            ctx_init(h, scores0[h])

        def step(it, carry):
            b = 2 * it + 1
            block_step((b + 1, 1), (b, 0))
            block_step((b + 2, 0), (b + 1, 1))
            return carry

        lax.fori_loop(0, n_blk // 2 - 1, step, 0)
        block_step((n_blk, 1), (n_blk - 1, 0))
        block_step(None, (n_blk, 1))

    def one_pass_latent():
        scores0 = [ctx_scores(h) for h in range(DIFF_HEADS)]
        for h in range(DIFF_HEADS):
            ctx_init(h, scores0[h])
            gap_ref[h] = jnp.zeros(gap_ref.shape[1:], F32)

        def scores(unit):
            blk, h, c = unit
            row0 = pl.multiple_of(CTX_LEN + (blk - 1) * KV_BLOCK, TILE)
            return _dot(k_ref[0, pl.ds(row0, KV_BLOCK), :],
                        qst_ref[h, :, c * QA_TILE:(c + 1) * QA_TILE])

        def consume(unit, s, vt):
            _, h, c = unit
            cols = slice(c * QA_TILE, (c + 1) * QA_TILE)
            m_old = m_ref[h, :, cols]
            m_cur = jnp.max(s, axis=0, keepdims=True)
            pv = _dot(value_rows(vt, h), jnp.exp2(s - m_old).astype(BF16))
            m_new = jnp.maximum(m_old, m_cur)
            acc_ref[h, :, cols] = (acc_ref[h, :, cols] + pv) * jnp.exp2(m_old - m_new)
            m_ref[h, :, cols] = m_new
            gap_ref[h, :, cols] = jnp.maximum(gap_ref[h, :, cols], m_cur - m_old)

        def step(it, carry):
            b0 = it * BLOCKS_PER_STEP + 1
            units = [(b0 + u, h, c) for u in range(BLOCKS_PER_STEP)
                     for h in range(DIFF_HEADS) for c in range(2)]
            pending = [scores(u) for u in units[:LOOKAHEAD]]
            for n, unit in enumerate(units):
                if n + LOOKAHEAD < len(units):
                    pending.append(scores(units[n + LOOKAHEAD]))
                if unit[1] == 0 and unit[2] == 0:
                    t0 = 1 + (unit[0] - 1) * tiles_per_blk
                    vt = jnp.concatenate(
                        [vt_ref[0, t0 + u] for u in range(tiles_per_blk)], axis=1)
                consume(unit, pending.pop(0), vt)
            return carry

        lax.fori_loop(0, n_blk // BLOCKS_PER_STEP, step, 0)

    def finalize():
        lv = lv_ref[...]
        lam = (jnp.exp(jnp.sum(lv[0:1] * lv[1:2], axis=1, keepdims=True))
               - jnp.exp(jnp.sum(lv[2:3] * lv[3:4], axis=1, keepdims=True)) + li_ref[0])
        out_scale = li_ref[1]
        outs = []
        for h in range(DIFF_HEADS):
            a = acc_ref[h]
            o1 = a[0:HEAD_DIM, 0:QA_TILE] / a[HEAD_DIM:HEAD_DIM + 1, 0:QA_TILE]
            o2 = a[0:HEAD_DIM, QA_TILE:n_cols] / a[HEAD_DIM:HEAD_DIM + 1, QA_TILE:n_cols]
            o = o1 - lam * o2
            ms = jnp.mean(o * o, axis=0, keepdims=True)
            outs.append(o * lax.rsqrt(ms + EPS) * sw_ref[...] * out_scale)
        o_ref[0] = jnp.concatenate(outs, axis=0).T.astype(BF16)

    @pl.when(i < CTX_LEN // QA_TILE)
    def _():
        for h in range(DIFF_HEADS):
            ctx_init(h, ctx_scores(h))
        finalize()

    @pl.when(i >= CTX_LEN // QA_TILE)
    def _():
        one_pass_latent()
        finalize()

        @pl.when(jnp.max(gap_ref[...]) > EXP_CAP)
        def _():
            two_pass_latent()
            finalize()


def _diff_attn_call(lam_init, lam_vecs, qa, ka, vat, subw):
    nb, rows, _ = qa.shape
    nt = rows // TILE
    return pl.pallas_call(
        _diff_attn_kernel,
        grid=(nb, rows // QA_TILE),
        in_specs=[
            pl.BlockSpec(memory_space=pltpu.SMEM),
            _resident((4, DIFF_QK_DIM)),
            pl.BlockSpec((1, QA_TILE, 256), lambda b, i: (b, i, 0)),
            pl.BlockSpec((1, rows, 256), lambda b, i: (b, 0, 0)),
            pl.BlockSpec((1, nt, 256, TILE), lambda b, i: (b, 0, 0, 0)),
            _resident((HEAD_DIM, QA_TILE)),
        ],
        out_specs=pl.BlockSpec((1, QA_TILE, 256), lambda b, i: (b, i, 0)),
        out_shape=jax.ShapeDtypeStruct((nb, rows, 256), BF16),
        scratch_shapes=[
            pltpu.VMEM((DIFF_HEADS, 256, 2 * QA_TILE), BF16),
            pltpu.VMEM((DIFF_HEADS, KV_BLOCK, 2 * QA_TILE), F32),
            pltpu.VMEM((DIFF_HEADS, KV_BLOCK, 2 * QA_TILE), F32),
            pltpu.VMEM((DIFF_HEADS, 1, 2 * QA_TILE), F32),
            pltpu.VMEM((DIFF_HEADS, 1, 2 * QA_TILE), F32),
            pltpu.VMEM((DIFF_HEADS, 1, 2 * QA_TILE), F32),
            pltpu.VMEM((DIFF_HEADS, PV_ROWS, 2 * QA_TILE), F32),
            pltpu.VMEM((DIFF_HEADS, 1, 2 * QA_TILE), F32),
        ],
        compiler_params=_params(2),
        name="diff_attn",
    )(lam_init, lam_vecs, qa, ka, vat, subw)


MIX_LOOKAHEAD = 2


def _head_rows(qt, k):
    row = lax.broadcasted_iota(jnp.int32, qt.shape, 0) // HEAD_DIM
    return jnp.where(row == k, qt, jnp.zeros_like(qt))


def _value_rows(vt_parts, rows):
    v = jnp.concatenate([part[rows] for part in vt_parts], axis=1)
    return jnp.concatenate([v, jnp.ones((HALO_BF16, v.shape[1]), BF16)], axis=0)


def _local_mix_kernel(sink_ref, bqt_ref, bkp_ref, bkc_ref, bkn_ref, bkx_ref,
                      bvtp_ref, bvtc_ref, bvtn_ref, bvtx_ref,
                      cqt_ref, ckp_ref, ckc_ref, ckn_ref, ckx_ref,
                      cvtp_ref, cvtc_ref, cvtn_ref, cvtx_ref, bmask_ref, ctab_ref,
                      dp_ref, dc_ref, dn_ref, cw_ref, o_ref, s_ref):
    j = pl.program_id(1)
    n_tiles = pl.num_programs(1)
    half = SWA_WINDOW

    k_b = jnp.concatenate([bkp_ref[0], bkc_ref[0], bkn_ref[0], bkx_ref[0]], axis=0)
    vt_b = [bvtp_ref[0, 0][:, half:], bvtc_ref[0, 0], bvtn_ref[0, 0][:, :half], bvtx_ref[0, 0]]
    vt_c = [cvtp_ref[0, 0], cvtc_ref[0, 0], cvtn_ref[0, 0], cvtx_ref[0, 0]]

    units = [(mixer, p, k) for mixer in "BC" for p in range(2) for k in range(2)]

    def scores(unit):
        mixer, p, k = unit
        rows = slice(p * LANES, (p + 1) * LANES)
        if mixer == "B":
            return _dot(k_b, _head_rows(bqt_ref[0, 0, rows, :], k))
        k_c = jnp.concatenate([ckp_ref[0, :, rows], ckc_ref[0, :, rows],
                               ckn_ref[0, :, rows], ckx_ref[0, :, rows]], axis=0)
        return _dot(k_c, _head_rows(cqt_ref[0, 0, rows, :], k))

    def attend(unit, s):
        mixer, p, k = unit
        feat = slice(k * HEAD_DIM, (k + 1) * HEAD_DIM)
        if mixer == "B":
            s = s + bmask_ref[0]
            sink = sink_ref[2 * k + p] * LOG2E
            m = jnp.maximum(jnp.max(s, axis=0, keepdims=True), sink)
            pv = _dot(_value_rows(vt_b, feat), jnp.exp2(s - m).astype(BF16))
            return pv[0:HEAD_DIM] / (pv[HEAD_DIM:HEAD_DIM + 1] + jnp.exp2(sink - m))
        s = s + ctab_ref[0, 2 * p + k]
        m = jnp.max(s, axis=0, keepdims=True)
        rows = slice(p * LANES + k * HEAD_DIM, p * LANES + (k + 1) * HEAD_DIM)
        pv = _dot(_value_rows(vt_c, rows), jnp.exp2(s - m).astype(BF16))
        return pv[0:HEAD_DIM] / pv[HEAD_DIM:HEAD_DIM + 1]

    def park(n):
        s = scores(units[n])
        s_ref[n % (MIX_LOOKAHEAD + 1), 0:s.shape[0], :] = s

    for n in range(MIX_LOOKAHEAD):
        park(n)
    outs = []
    for n, unit in enumerate(units):
        if n + MIX_LOOKAHEAD < len(units):
            park(n + MIX_LOOKAHEAD)
        n_keys = (TILE + 2 * SWA_WINDOW if unit[0] == "B" else 3 * TILE) + CTX_LEN
        outs.append(attend(unit, s_ref[n % (MIX_LOOKAHEAD + 1), 0:n_keys, :]))
    o_ref[0, :, 0:GROUP_WIDTH] = jnp.concatenate(outs[0:4], axis=0).T.astype(BF16)
    o_ref[0, :, GROUP_WIDTH:2 * GROUP_WIDTH] = jnp.concatenate(outs[4:8], axis=0).T.astype(BF16)

    has_prev = (j > 1).astype(F32)
    has_next = ((j > 0) & (j < n_tiles - 1)).astype(F32)
    dc = dc_ref[0].astype(F32)
    dp = dp_ref[0].astype(F32)[HALO_BF16 - HALO_F32:]
    dn = dn_ref[0].astype(F32)[:HALO_F32]
    gw = GROUP_WIDTH
    z = jnp.concatenate([dp[:, gw:2 * gw] * dp[:, 2 * gw:] * has_prev,
                         dc[:, gw:2 * gw] * dc[:, 2 * gw:],
                         dn[:, gw:2 * gw] * dn[:, 2 * gw:] * has_next], axis=0)
    cw = cw_ref[...]
    c0 = HALO_F32
    conv = (z[c0 - 1:c0 - 1 + TILE] * cw[0:1] + z[c0:c0 + TILE] * cw[1:2]
            + z[c0 + 1:c0 + 1 + TILE] * cw[2:3])
    o_ref[0, :, 4 * LANES:6 * LANES] = (dc[:, 0:gw] * conv).astype(BF16)


def _tile_variants(nt):
    return (0, 1, 2, nt - 1)


def _local_mix_call(sink, bqt, bk, bvt, cqt, ck, cvt, bmask, ctab, hd, conv_w):
    nb, rows, _ = bk.shape
    nt = rows // TILE
    variant = lambda j: jnp.where(j == 0, 0, jnp.where(j == 1, 1, jnp.where(j == nt - 1, 3, 2)))
    hb = TILE // SWA_WINDOW
    h16 = TILE // HALO_BF16
    prev_j = lambda j: jnp.maximum(j - 1, 0)
    next_j = lambda j: jnp.minimum(j + 1, nt - 1)
    cur = lambda w: pl.BlockSpec((1, TILE, w), lambda b, j: (b, j, 0))
    prev = lambda w: pl.BlockSpec((1, TILE, w), lambda b, j: (b, prev_j(j), 0))
    nxt = lambda w: pl.BlockSpec((1, TILE, w), lambda b, j: (b, next_j(j), 0))
    ctx = lambda w: pl.BlockSpec((1, TILE, w), lambda b, j: (b, 0, 0))
    tcur = lambda w: pl.BlockSpec((1, 1, w, TILE), lambda b, j: (b, j, 0, 0))
    tprev = lambda w: pl.BlockSpec((1, 1, w, TILE), lambda b, j: (b, prev_j(j), 0, 0))
    tnxt = lambda w: pl.BlockSpec((1, 1, w, TILE), lambda b, j: (b, next_j(j), 0, 0))
    tctx = lambda w: pl.BlockSpec((1, 1, w, TILE), lambda b, j: (b, 0, 0, 0))
    return pl.pallas_call(
        _local_mix_kernel,
        grid=(nb, nt),
        in_specs=[
            pl.BlockSpec(memory_space=pltpu.SMEM),
            tcur(256),
            pl.BlockSpec((1, SWA_WINDOW, 128), lambda b, j: (b, jnp.maximum(j * hb - 1, 0), 0)),
            cur(128),
            pl.BlockSpec((1, SWA_WINDOW, 128), lambda b, j: (b, jnp.minimum((j + 1) * hb, nt * hb - 1), 0)),
            ctx(128),
            tprev(128), tcur(128), tnxt(128), tctx(128),
            tcur(256),
            prev(256), cur(256), nxt(256), ctx(256),
            tprev(256), tcur(256), tnxt(256), tctx(256),
            pl.BlockSpec((1, TILE + 2 * SWA_WINDOW + CTX_LEN, TILE), lambda b, j: (variant(j), 0, 0)),
            pl.BlockSpec((1, NA_HEADS, 3 * TILE + CTX_LEN, TILE),
                         lambda b, j: (variant(j), 0, 0, 0)),
            pl.BlockSpec((1, HALO_BF16, 768), lambda b, j: (b, jnp.maximum(j * h16 - 1, 0), 0)),
            cur(768),
            pl.BlockSpec((1, HALO_BF16, 768), lambda b, j: (b, jnp.minimum((j + 1) * h16, nt * h16 - 1), 0)),
            _resident((3, GROUP_WIDTH)),
        ],
        out_specs=pl.BlockSpec((1, TILE, 768), lambda b, j: (b, j, 0)),
        out_shape=jax.ShapeDtypeStruct((nb, rows, 768), BF16),
        scratch_shapes=[pltpu.VMEM((MIX_LOOKAHEAD + 1, 3 * TILE + CTX_LEN, TILE), F32)],
        compiler_params=_params(2),
        name="local_mix",
    )(sink, bqt, bk, bk, bk, bk, bvt, bvt, bvt, bvt, cqt, ck, ck, ck, ck, cvt, cvt, cvt, cvt,
      bmask, ctab, hd, hd, hd, conv_w)


MXU_WIDTH = 256
FFN_CHUNKS = ((0, 6 * MXU_WIDTH), (6 * MXU_WIDTH, D_FF))
FFN_CHUNK_MAX = max(hi - lo for lo, hi in FFN_CHUNKS)
EXT = TILE + 2 * HALO_F32


def _halo_f32(prev_ref, next_ref):
    p = prev_ref[0].astype(F32)
    n = next_ref[0].astype(F32)
    return p[p.shape[0] - HALO_F32:], n[:HALO_F32]


def _ffn_tile(final, j, n_tiles, xp_ref, xc_ref, xn_ref, ap_ref, ac_ref, an_ref,
              mp_ref, mc_ref, mn_ref, mod_ref, wo_ref, nw_ref, wu_ref, cw_ref, cb_ref, wd_ref,
              fw_ref, u_ref):
    mod = mod_ref[0]
    g1 = mod[:, 2 * D_MODEL:3 * D_MODEL]
    sh2 = mod[:, 3 * D_MODEL:4 * D_MODEL]
    sc2 = mod[:, 4 * D_MODEL:5 * D_MODEL]
    g2 = mod[:, 5 * D_MODEL:6 * D_MODEL]

    ap, an = _halo_f32(ap_ref, an_ref)
    mp, mn = _halo_f32(mp_ref, mn_ref)
    mix = jnp.concatenate([
        jnp.concatenate([ap, mp], axis=1),
        jnp.concatenate([ac_ref[0], mc_ref[0]], axis=1).astype(F32),
        jnp.concatenate([an, mn], axis=1)], axis=0).astype(BF16)
    x = jnp.concatenate([xp_ref[0], xc_ref[0], xn_ref[0]], axis=0)
    x1 = x + g1 * _dot(mix, wo_ref[...])
    f = _rmsnorm_mod(x1, nw_ref[...], sh2, sc2).astype(BF16)

    has_prev = (j > 1).astype(F32)
    has_next = ((j > 0) & (j < n_tiles - 1)).astype(F32)
    lo = HALO_F32 - 1
    hi = HALO_F32 + TILE
    y = None
    for c0, c1 in FFN_CHUNKS:
        n = c1 - c0
        halves = []
        for base in (c0, D_FF + c0):
            u = _dot(f, wu_ref[:, base:base + n])
            slabs = []
            for t in range(n // LANES):
                cols = slice(base + t * LANES, base + (t + 1) * LANES)
                u_ref[t] = u[:, t * LANES:(t + 1) * LANES]
                u_ref[t, lo:lo + 1, :] = u_ref[t, lo:lo + 1, :] * has_prev
                u_ref[t, hi:hi + 1, :] = u_ref[t, hi:hi + 1, :] * has_next
                cw = cw_ref[:, cols]
                slabs.append(u_ref[t, pl.ds(lo, TILE), :] * cw[0:1]
                             + u_ref[t, pl.ds(lo + 1, TILE), :] * cw[1:2]
                             + u_ref[t, pl.ds(lo + 2, TILE), :] * cw[2:3] + cb_ref[:, cols])
            halves.append(jnp.concatenate(slabs, axis=1))
        g, v = halves
        act = (g * (1.0 / (1.0 + jnp.exp(-g))) * v).astype(BF16)
        part = _dot(act, wd_ref[c0:c1, :])
        y = part if y is None else y + part
    x2 = x1[HALO_F32:HALO_F32 + TILE] + g2 * y
    if final:
        ms = jnp.mean(x2 * x2, axis=-1, keepdims=True)
        x2 = x2 * lax.rsqrt(ms + EPS) * fw_ref[...]
    return x2


N_TILE_REFS = 10


def _out_ffn_kernel(final, *refs):
    tile_refs, weights, (o_ref, u_ref) = refs[:N_TILE_REFS], refs[N_TILE_REFS:-2], refs[-2:]
    o_ref[0] = _ffn_tile(final, pl.program_id(1), pl.num_programs(1), *tile_refs, *weights, u_ref)


def _out_ffn_call(xall, oa, obcd, mod, w_out, norm_w, w_up, conv_w, conv_b, w_down, final_w, final):
    nb, rows, _ = xall.shape
    nt = rows // TILE
    h8 = TILE // HALO_F32
    h16 = TILE // HALO_BF16

    def trio(w, halo, per_tile):
        return [
            pl.BlockSpec((1, halo, w), lambda b, j: (b, jnp.maximum(j * per_tile - 1, 0), 0)),
            pl.BlockSpec((1, TILE, w), lambda b, j: (b, j, 0)),
            pl.BlockSpec((1, halo, w), lambda b, j: (
                b, jnp.minimum((j + 1) * per_tile, nt * per_tile - 1), 0)),
        ]

    tile_specs = [*trio(D_MODEL, HALO_F32, h8), *trio(256, HALO_BF16, h16),
                  *trio(768, HALO_BF16, h16), _mod_spec(nb)]
    weight_specs = [
        _resident((D_MODEL, D_MODEL)),
        _resident((1, D_MODEL)),
        _resident((D_MODEL, 2 * D_FF)),
        _resident((3, 2 * D_FF)),
        _resident((1, 2 * D_FF)),
        _resident((D_FF, D_MODEL)),
        _resident((1, D_MODEL)),
    ]
    if final:
        out_spec = pl.BlockSpec((1, TILE, D_MODEL), lambda b, j: (b, jnp.maximum(j - 1, 0), 0))
        out_shape = jax.ShapeDtypeStruct((nb, rows - CTX_LEN, D_MODEL), F32)
    else:
        out_spec = pl.BlockSpec((1, TILE, D_MODEL), lambda b, j: (b, j, 0))
        out_shape = jax.ShapeDtypeStruct((nb, rows, D_MODEL), F32)
    return pl.pallas_call(
        functools.partial(_out_ffn_kernel, final),
        grid=(nb, nt),
        in_specs=[*tile_specs, *weight_specs],
        out_specs=out_spec,
        out_shape=out_shape,
        scratch_shapes=[pltpu.VMEM((FFN_CHUNK_MAX // LANES, EXT, LANES), F32)],
        compiler_params=_params(2),
        name="out_ffn_final" if final else "out_ffn",
    )(xall, xall, xall, oa, oa, oa, obcd, obcd, obcd, mod, w_out, norm_w, w_up, conv_w, conv_b,
      w_down, final_w)


def _rope_tables(n_lat, dim):
    quarter = dim // 4
    t = np.arange(n_lat)
    pos = np.stack([t // GRID_W, t % GRID_W], axis=1).astype(np.float32)
    freqs = jnp.asarray(ROPE_THETA, F32) ** (-(jnp.arange(quarter, dtype=F32) / quarter))
    lane = np.arange(LANES) % dim
    axis = lane // (dim // 2)
    fidx = lane % quarter
    sign = np.where((lane % (dim // 2)) < quarter, -1.0, 1.0).astype(np.float32)
    ang = jnp.asarray(pos)[:, axis] * freqs[fidx][None, :]
    cos = jnp.concatenate([jnp.ones((CTX_LEN, LANES), F32), jnp.cos(ang)], axis=0)
    sin = jnp.concatenate([jnp.zeros((CTX_LEN, LANES), F32), jnp.sin(ang) * sign[None, :]], axis=0)
    return cos, sin


def _na_bias_table(rpb):
    rpt = TILE // GRID_W
    n_heads, n_dr, n_dc = rpb.shape
    cq = np.arange(GRID_W)[:, None]
    ck = np.arange(GRID_W)[None, :]
    dc = np.clip(ck - cq, -(NA_COLS - 1), NA_COLS - 1) + NA_COLS - 1
    col_start = np.clip(cq - NA_COLS // 2, 0, GRID_W - NA_COLS)
    col_ok = (ck >= col_start) & (ck < col_start + NA_COLS)
    pick = dc[None, :, :] == np.arange(n_dc)[:, None, None]
    e = jnp.sum(jnp.where(pick[None, None], rpb.astype(F32)[:, :, :, None, None], 0.0), axis=2)
    e = jnp.where(col_ok[None, None], e, NEG_INF)
    blocks = []
    for rq in range(rpt):
        lo = NA_ROWS - 1 - rpt - rq
        blocks.append(e[:, lo:lo + 3 * rpt].transpose(0, 1, 3, 2))
    local = jnp.stack(blocks, axis=3).reshape(n_heads, 3 * TILE, TILE)
    return jnp.concatenate([local, jnp.zeros((n_heads, CTX_LEN, TILE), F32)], axis=1)


def _window_masks(n_lat):
    nt = 1 + n_lat // TILE
    half = SWA_WINDOW
    rpt = TILE // GRID_W
    n_rows = n_lat // GRID_W
    rq = np.arange(TILE)[None, :]
    kb = np.arange(TILE + 2 * half + CTX_LEN)[:, None]
    kc = np.arange(3 * TILE + CTX_LEN)[:, None]
    masks_b, masks_c = [], []
    for j in _tile_variants(nt):
        kpos = (j - 1) * TILE - half + kb
        band = (np.abs(kb - half - rq) <= SWA_WINDOW) & (kpos >= 0) & (kpos < n_lat) & (j > 0)
        masks_b.append(band | (kb >= TILE + 2 * half))
        r = (j - 1) * rpt + rq // GRID_W
        kr = (j - 2) * rpt + kc // GRID_W
        rs = np.clip(r - NA_ROWS // 2, 0, n_rows - NA_ROWS)
        masks_c.append(((kr >= rs) & (kr < rs + NA_ROWS) & (j > 0)) | (kc >= 3 * TILE))
    to_add = lambda ms: jnp.where(jnp.asarray(np.stack(ms)), 0.0, NEG_INF).astype(F32)
    return to_add(masks_b), to_add(masks_c)


def _swa_head_order(w, axis, start):
    def blk(lo, hi):
        return lax.slice_in_dim(w, lo, hi, axis=axis)
    heads = [blk(start + h * HEAD_DIM, start + (h + 1) * HEAD_DIM) for h in (0, 2, 1, 3)]
    return jnp.concatenate(
        [blk(0, start), *heads, blk(start + SWA_HEADS * HEAD_DIM, w.shape[axis])], axis=axis)


def kernel(x, c, ctx, c_ctx, norm_mix_w, norm_ffn_w, w_mod, b_mod, w_in, w_out, diff_lambda_q1, diff_lambda_k1, diff_lambda_q2, diff_lambda_k2, diff_subln_w, swa_sink, na_rpb, sconv_w, ffn_w_up, ffn_conv_w, ffn_conv_b, ffn_w_down, final_norm_w):
    nb, n_lat, d = x.shape
    depth = w_in.shape[0]
    assert d == D_MODEL and ctx.shape[1] == CTX_LEN and n_lat % TILE == 0 and nb < 8
    assert n_lat // GRID_W >= NA_ROWS and n_lat % (max(2, BLOCKS_PER_STEP) * KV_BLOCK) == 0

    xall = jnp.concatenate([ctx, x], axis=1)
    cin = jnp.concatenate([c, c_ctx[None, :], jnp.zeros((8 - nb - 1, d), F32)], axis=0)
    mods = _mod_call(cin, w_mod, b_mod)

    tabs = (*_rope_tables(n_lat, DIFF_QK_DIM), *_rope_tables(n_lat, HEAD_DIM))
    mask_b, mask_c = _window_masks(n_lat)
    for l in range(depth):
        mod = mods[l].reshape(8, 1, 6 * D_MODEL)
        w_in_l = _swa_head_order(w_in[l], 1, 3 * GROUP_WIDTH).astype(BF16)
        w_out_l = _swa_head_order(w_out[l], 0, GROUP_WIDTH).astype(BF16)
        qa, ka, vat, bqt, bk, bvt, cqt, ck, cvt, hd = _in_proj_call(
            xall, mod, norm_mix_w[l][None, :], w_in_l, tabs)

        lambda_init = 0.8 - 0.6 * math.exp(-0.3 * l)
        lam_init = jnp.asarray([lambda_init, 1.0 - lambda_init], F32)
        lam_vecs = jnp.stack([diff_lambda_q1[l], diff_lambda_k1[l],
                              diff_lambda_q2[l], diff_lambda_k2[l]])
        subw = jnp.broadcast_to(diff_subln_w[l][:, None], (HEAD_DIM, QA_TILE))
        oa = _diff_attn_call(lam_init, lam_vecs, qa, ka, vat, subw)

        ctab = (_na_bias_table(na_rpb[l])[None] + mask_c[:, None]) * LOG2E
        obcd = _local_mix_call(swa_sink[l], bqt, bk, bvt, cqt, ck, cvt, mask_b, ctab,
                               hd, sconv_w[l])

        xall = _out_ffn_call(
            xall, oa, obcd, mod, w_out_l, norm_ffn_w[l][None, :],
            ffn_w_up[l].astype(BF16), ffn_conv_w[l], ffn_conv_b[l][None, :],
            ffn_w_down[l].astype(BF16), final_norm_w[None, :], final=(l == depth - 1))
    return xall
```

```python
import functools
import math

import numpy as np
import jax
import jax.numpy as jnp
from jax import lax
from jax.experimental import pallas as pl
from jax.experimental.pallas import tpu as pltpu

D_MODEL = 1024
GRID_W = 64
CTX_LEN = 256
HEAD_DIM = 64
GROUP_WIDTH = 256
DIFF_HEADS = 4
DIFF_QK_DIM = 32
SWA_HEADS = 4
SWA_WINDOW = 128
NA_HEADS = 4
NA_ROWS = 8
NA_COLS = 16
D_FF = 2816
IN_COLS = 2816
ROPE_THETA = 10000.0
EPS = 1e-6
NEG_INF = -1e30
LOG2E = math.log2(math.e)

TILE = 256
HALO_F32 = 8
HALO_BF16 = 16
LANES = 128
VMEM_LIMIT = 56 * 1024 * 1024

F32 = jnp.float32
BF16 = jnp.bfloat16


def _params(n_grid):
    return pltpu.CompilerParams(
        dimension_semantics=("arbitrary",) * n_grid, vmem_limit_bytes=VMEM_LIMIT)


def _resident(shape):
    nd = len(shape)
    return pl.BlockSpec(shape, lambda *_: (0,) * nd, pipeline_mode=pl.Buffered(1))


def _mod_spec(n_batch):
    return pl.BlockSpec((1, 1, 6 * D_MODEL), lambda b, j: (jnp.where(j == 0, n_batch, b), 0, 0))


def _rmsnorm_mod(x, w, shift, scale):
    ms = jnp.mean(x * x, axis=-1, keepdims=True)
    return (x * lax.rsqrt(ms + EPS) * w) * (1.0 + scale) + shift


def _dot(a, b):
    return jnp.dot(a, b, preferred_element_type=F32)


MOD_BLOCK = 1536


def _mod_kernel(c_ref, w_ref, b_ref, o_ref):
    c = c_ref[...]
    a = c * (1.0 / (1.0 + jnp.exp(-c)))
    o_ref[0] = _dot(a.astype(BF16), w_ref[0].astype(BF16)) + b_ref[0]


def _mod_call(cin, w_mod, b_mod):
    depth = w_mod.shape[0]
    n = 6 * D_MODEL
    return pl.pallas_call(
        _mod_kernel,
        grid=(depth, n // MOD_BLOCK),
        in_specs=[
            pl.BlockSpec((8, D_MODEL), lambda l, k: (0, 0)),
            pl.BlockSpec((1, D_MODEL, MOD_BLOCK), lambda l, k: (l, 0, k)),
            pl.BlockSpec((1, 1, MOD_BLOCK), lambda l, k: (l, 0, k)),
        ],
        out_specs=pl.BlockSpec((1, 8, MOD_BLOCK), lambda l, k: (l, 0, k)),
        out_shape=jax.ShapeDtypeStruct((depth, 8, n), F32),
        compiler_params=_params(2),
        name="mod_proj",
    )(cin, w_mod, b_mod.reshape(depth, 1, n))


def _rope(x, cos, sin, half):
    lane = lax.broadcasted_iota(jnp.int32, x.shape, 1)
    first = (lane % (2 * half)) < half
    partner = jnp.where(first, pltpu.roll(x, LANES - half, 1), pltpu.roll(x, half, 1))
    return x * cos + partner * sin


def _in_proj_kernel(x_ref, mod_ref, nw_ref, w_ref, cosa_ref, sina_ref, cosb_ref, sinb_ref,
                    qa_ref, ka_ref, vat_ref, bqt_ref, bk_ref, bvt_ref, cqt_ref, ck_ref, cvt_ref,
                    hd_ref):
    mod = mod_ref[0]
    shift = mod[:, 0:D_MODEL]
    scale = mod[:, D_MODEL:2 * D_MODEL]
    z = _rmsnorm_mod(x_ref[0], nw_ref[...], shift, scale)
    h = _dot(z.astype(BF16), w_ref[...])

    def sec(k, n=1):
        return h[:, k * LANES:(k + n) * LANES]

    cosa, sina = cosa_ref[...], sina_ref[...]
    cosb, sinb = cosb_ref[...], sinb_ref[...]
    qa_scale = DIFF_QK_DIM ** -0.5 * LOG2E
    q_scale = HEAD_DIM ** -0.5 * LOG2E
    ha = DIFF_QK_DIM // 4
    hb = HEAD_DIM // 4
    for t in range(2):
        qa_ref[0, :, t * LANES:(t + 1) * LANES] = (
            _rope(sec(t), cosa, sina, ha) * qa_scale).astype(BF16)
        ka_ref[0, :, t * LANES:(t + 1) * LANES] = _rope(sec(2 + t), cosa, sina, ha).astype(BF16)
    vat_ref[0, 0] = sec(4, 2).T.astype(BF16)
    for t in range(2):
        bqt_ref[0, 0, t * LANES:(t + 1) * LANES, :] = (
            _rope(sec(6 + t), cosb, sinb, hb) * q_scale).T.astype(BF16)
    bk_ref[0] = _rope(sec(8), cosb, sinb, hb).astype(BF16)
    bvt_ref[0, 0] = sec(9).T.astype(BF16)
    cqt_ref[0, 0] = (sec(10, 2) * q_scale).T.astype(BF16)
    ck_ref[0] = sec(12, 2).astype(BF16)
    cvt_ref[0, 0] = sec(14, 2).T.astype(BF16)
    hd_ref[0] = sec(16, 6).astype(BF16)


def _in_proj_call(xall, mod, norm_w, w_in, tabs):
    nb, rows, _ = xall.shape
    nt = rows // TILE
    tile = lambda w: pl.BlockSpec((1, TILE, w), lambda b, j: (b, j, 0))
    ttile = lambda w: pl.BlockSpec((1, 1, w, TILE), lambda b, j: (b, j, 0, 0))
    tab = pl.BlockSpec((TILE, LANES), lambda b, j: (j, 0))
    sds = lambda w: jax.ShapeDtypeStruct((nb, rows, w), BF16)
    tsds = lambda w: jax.ShapeDtypeStruct((nb, nt, w, TILE), BF16)
    return pl.pallas_call(
        _in_proj_kernel,
        grid=(nb, nt),
        in_specs=[
            pl.BlockSpec((1, TILE, D_MODEL), lambda b, j: (b, j, 0)),
            _mod_spec(nb),
            _resident((1, D_MODEL)),
            _resident((D_MODEL, IN_COLS)),
            tab, tab, tab, tab,
        ],
        out_specs=[
            tile(256), tile(256), ttile(256),
            ttile(256), tile(128), ttile(128),
            ttile(256), tile(256), ttile(256),
            tile(768),
        ],
        out_shape=[
            sds(256), sds(256), tsds(256),
            tsds(256), sds(128), tsds(128),
            tsds(256), sds(256), tsds(256),
            sds(768),
        ],
        compiler_params=_params(2),
        name="in_proj",
    )(xall, mod, norm_w, w_in, *tabs)


QA_TILE = 256
KV_BLOCK = 512
BLOCKS_PER_STEP = 16
LOOKAHEAD = 2
EXP_CAP = 100.0
PV_ROWS = HEAD_DIM + HALO_BF16


def _diff_attn_kernel(li_ref, lv_ref, q_ref, k_ref, vt_ref, sw_ref, o_ref,
                      qst_ref, sa_ref, sb_ref, mca_ref, mcb_ref, m_ref, acc_ref, gap_ref):
    i = pl.program_id(1)
    s_refs = (sa_ref, sb_ref)
    mc_refs = (mca_ref, mcb_ref)
    q = q_ref[0].astype(F32)
    lane = lax.broadcasted_iota(jnp.int32, q.shape, 1) // DIFF_QK_DIM
    for h in range(DIFF_HEADS):
        qs = jnp.concatenate([jnp.where(lane == 2 * h + c, q, 0.0) for c in range(2)], axis=0)
        qst_ref[h] = qs.T.astype(BF16)
    n_cols = 2 * QA_TILE
    n_blk = (k_ref.shape[1] - CTX_LEN) // KV_BLOCK
    tiles_per_blk = KV_BLOCK // TILE

    def value_rows(vt, h):
        ones = jnp.ones((HALO_BF16, vt.shape[1]), BF16)
        return jnp.concatenate([vt[h * HEAD_DIM:(h + 1) * HEAD_DIM], ones], axis=0)

    def ctx_scores(h):
        return _dot(k_ref[0, 0:CTX_LEN, :], qst_ref[h])

    def ctx_init(h, s):
        m0 = jnp.max(s, axis=0, keepdims=True)
        acc_ref[h] = _dot(value_rows(vt_ref[0, 0], h), jnp.exp2(s - m0).astype(BF16))
        m_ref[h] = m0

    def score_head(blk, slot, h):
        row0 = pl.multiple_of(CTX_LEN + (blk - 1) * KV_BLOCK, TILE)
        s = _dot(k_ref[0, pl.ds(row0, KV_BLOCK), :], qst_ref[h])
        s_refs[slot][h] = s
        mc_refs[slot][h] = jnp.max(s, axis=0, keepdims=True)

    def block_step(score_blk, acc_blk):
        if acc_blk is not None:
            ablk, aslot = acc_blk
            t0 = 1 + (ablk - 1) * tiles_per_blk
            vt = jnp.concatenate([vt_ref[0, t0 + u] for u in range(tiles_per_blk)], axis=1)
        for h in range(DIFF_HEADS):
            if score_blk is not None:
                score_head(*score_blk, h)
            if acc_blk is not None:
                m_old = m_ref[h]
                m_new = jnp.maximum(m_old, mc_refs[aslot][h])
                p = jnp.exp2(s_refs[aslot][h] - m_new).astype(BF16)
                acc_ref[h] = acc_ref[h] * jnp.exp2(m_old - m_new) + _dot(value_rows(vt, h), p)
                m_ref[h] = m_new

    def two_pass_latent():
        scores0 = [ctx_scores(h) for h in range(DIFF_HEADS)]
        for h in range(DIFF_HEADS):
            score_head(1, 0, h)
            ctx_init(h, scores0[h])

        def step(it, carry):
            b = 2 * it + 1
            block_step((b + 1, 1), (b, 0))
            block_step((b + 2, 0), (b + 1, 1))
            return carry

        lax.fori_loop(0, n_blk // 2 - 1, step, 0)
        block_step((n_blk, 1), (n_blk - 1, 0))
        block_step(None, (n_blk, 1))

    def one_pass_latent():
        scores0 = [ctx_scores(h) for h in range(DIFF_HEADS)]
        for h in range(DIFF_HEADS):
            ctx_init(h, scores0[h])
            gap_ref[h] = jnp.zeros(gap_ref.shape[1:], F32)

        def scores(unit):
            blk, h, c = unit
            row0 = pl.multiple_of(CTX_LEN + (blk - 1) * KV_BLOCK, TILE)
            return _dot(k_ref[0, pl.ds(row0, KV_BLOCK), :],
                        qst_ref[h, :, c * QA_TILE:(c + 1) * QA_TILE])

        def consume(unit, s, vt):
            _, h, c = unit
            cols = slice(c * QA_TILE, (c + 1) * QA_TILE)
            m_old = m_ref[h, :, cols]
            m_cur = jnp.max(s, axis=0, keepdims=True)
            pv = _dot(value_rows(vt, h), jnp.exp2(s - m_old).astype(BF16))
            m_new = jnp.maximum(m_old, m_cur)
            acc_ref[h, :, cols] = (acc_ref[h, :, cols] + pv) * jnp.exp2(m_old - m_new)
            m_ref[h, :, cols] = m_new
            gap_ref[h, :, cols] = jnp.maximum(gap_ref[h, :, cols], m_cur - m_old)

        def step(it, carry):
            b0 = it * BLOCKS_PER_STEP + 1
            units = [(b0 + u, h, c) for u in range(BLOCKS_PER_STEP)
                     for h in range(DIFF_HEADS) for c in range(2)]
            pending = [scores(u) for u in units[:LOOKAHEAD]]
            for n, unit in enumerate(units):
                if n + LOOKAHEAD < len(units):
                    pending.append(scores(units[n + LOOKAHEAD]))
                if unit[1] == 0 and unit[2] == 0:
                    t0 = 1 + (unit[0] - 1) * tiles_per_blk
                    vt = jnp.concatenate(
                        [vt_ref[0, t0 + u] for u in range(tiles_per_blk)], axis=1)
                consume(unit, pending.pop(0), vt)
            return carry

        lax.fori_loop(0, n_blk // BLOCKS_PER_STEP, step, 0)

    def finalize():
        lv = lv_ref[...]
        lam = (jnp.exp(jnp.sum(lv[0:1] * lv[1:2], axis=1, keepdims=True))
               - jnp.exp(jnp.sum(lv[2:3] * lv[3:4], axis=1, keepdims=True)) + li_ref[0])
        out_scale = li_ref[1]
        outs = []
        for h in range(DIFF_HEADS):
            a = acc_ref[h]
            o1 = a[0:HEAD_DIM, 0:QA_TILE] / a[HEAD_DIM:HEAD_DIM + 1, 0:QA_TILE]
            o2 = a[0:HEAD_DIM, QA_TILE:n_cols] / a[HEAD_DIM:HEAD_DIM + 1, QA_TILE:n_cols]
            o = o1 - lam * o2
            ms = jnp.mean(o * o, axis=0, keepdims=True)
            outs.append(o * lax.rsqrt(ms + EPS) * sw_ref[...] * out_scale)
        o_ref[0] = jnp.concatenate(outs, axis=0).T.astype(BF16)

    @pl.when(i < CTX_LEN // QA_TILE)
    def _():
        for h in range(DIFF_HEADS):
            ctx_init(h, ctx_scores(h))
        finalize()

    @pl.when(i >= CTX_LEN // QA_TILE)
    def _():
        one_pass_latent()
        finalize()

        @pl.when(jnp.max(gap_ref[...]) > EXP_CAP)
        def _():
            two_pass_latent()
            finalize()


def _diff_attn_call(lam_init, lam_vecs, qa, ka, vat, subw):
    nb, rows, _ = qa.shape
    nt = rows // TILE
    return pl.pallas_call(
        _diff_attn_kernel,
        grid=(nb, rows // QA_TILE),
        in_specs=[
            pl.BlockSpec(memory_space=pltpu.SMEM),
            _resident((4, DIFF_QK_DIM)),
            pl.BlockSpec((1, QA_TILE, 256), lambda b, i: (b, i, 0)),
            pl.BlockSpec((1, rows, 256), lambda b, i: (b, 0, 0)),
            pl.BlockSpec((1, nt, 256, TILE), lambda b, i: (b, 0, 0, 0)),
            _resident((HEAD_DIM, QA_TILE)),
        ],
        out_specs=pl.BlockSpec((1, QA_TILE, 256), lambda b, i: (b, i, 0)),
        out_shape=jax.ShapeDtypeStruct((nb, rows, 256), BF16),
        scratch_shapes=[
            pltpu.VMEM((DIFF_HEADS, 256, 2 * QA_TILE), BF16),
            pltpu.VMEM((DIFF_HEADS, KV_BLOCK, 2 * QA_TILE), F32),
            pltpu.VMEM((DIFF_HEADS, KV_BLOCK, 2 * QA_TILE), F32),
            pltpu.VMEM((DIFF_HEADS, 1, 2 * QA_TILE), F32),
            pltpu.VMEM((DIFF_HEADS, 1, 2 * QA_TILE), F32),
            pltpu.VMEM((DIFF_HEADS, 1, 2 * QA_TILE), F32),
            pltpu.VMEM((DIFF_HEADS, PV_ROWS, 2 * QA_TILE), F32),
            pltpu.VMEM((DIFF_HEADS, 1, 2 * QA_TILE), F32),
        ],
        compiler_params=_params(2),
        name="diff_attn",
    )(lam_init, lam_vecs, qa, ka, vat, subw)


MIX_LOOKAHEAD = 2


def _head_rows(qt, k):
    row = lax.broadcasted_iota(jnp.int32, qt.shape, 0) // HEAD_DIM
    return jnp.where(row == k, qt, jnp.zeros_like(qt))


def _value_rows(vt_parts, rows):
    v = jnp.concatenate([part[rows] for part in vt_parts], axis=1)
    return jnp.concatenate([v, jnp.ones((HALO_BF16, v.shape[1]), BF16)], axis=0)


def _local_mix_kernel(sink_ref, bqt_ref, bkp_ref, bkc_ref, bkn_ref, bkx_ref,
                      bvtp_ref, bvtc_ref, bvtn_ref, bvtx_ref,
                      cqt_ref, ckp_ref, ckc_ref, ckn_ref, ckx_ref,
                      cvtp_ref, cvtc_ref, cvtn_ref, cvtx_ref, bmask_ref, ctab_ref,
                      dp_ref, dc_ref, dn_ref, cw_ref, o_ref, s_ref):
    j = pl.program_id(1)
    n_tiles = pl.num_programs(1)
    half = SWA_WINDOW

    k_b = jnp.concatenate([bkp_ref[0], bkc_ref[0], bkn_ref[0], bkx_ref[0]], axis=0)
    vt_b = [bvtp_ref[0, 0][:, half:], bvtc_ref[0, 0], bvtn_ref[0, 0][:, :half], bvtx_ref[0, 0]]
    vt_c = [cvtp_ref[0, 0], cvtc_ref[0, 0], cvtn_ref[0, 0], cvtx_ref[0, 0]]

    units = [(mixer, p, k) for mixer in "BC" for p in range(2) for k in range(2)]

    def scores(unit):
        mixer, p, k = unit
        rows = slice(p * LANES, (p + 1) * LANES)
        if mixer == "B":
            return _dot(k_b, _head_rows(bqt_ref[0, 0, rows, :], k))
        k_c = jnp.concatenate([ckp_ref[0, :, rows], ckc_ref[0, :, rows],
                               ckn_ref[0, :, rows], ckx_ref[0, :, rows]], axis=0)
        return _dot(k_c, _head_rows(cqt_ref[0, 0, rows, :], k))

    def attend(unit, s):
        mixer, p, k = unit
        feat = slice(k * HEAD_DIM, (k + 1) * HEAD_DIM)
        if mixer == "B":
            s = s + bmask_ref[0]
            sink = sink_ref[2 * k + p] * LOG2E
            m = jnp.maximum(jnp.max(s, axis=0, keepdims=True), sink)
            pv = _dot(_value_rows(vt_b, feat), jnp.exp2(s - m).astype(BF16))
            return pv[0:HEAD_DIM] / (pv[HEAD_DIM:HEAD_DIM + 1] + jnp.exp2(sink - m))
        s = s + ctab_ref[0, 2 * p + k]
        m = jnp.max(s, axis=0, keepdims=True)
        rows = slice(p * LANES + k * HEAD_DIM, p * LANES + (k + 1) * HEAD_DIM)
        pv = _dot(_value_rows(vt_c, rows), jnp.exp2(s - m).astype(BF16))
        return pv[0:HEAD_DIM] / pv[HEAD_DIM:HEAD_DIM + 1]

    def park(n):
        s = scores(units[n])
        s_ref[n % (MIX_LOOKAHEAD + 1), 0:s.shape[0], :] = s

    for n in range(MIX_LOOKAHEAD):
        park(n)
    outs = []
    for n, unit in enumerate(units):
        if n + MIX_LOOKAHEAD < len(units):
            park(n + MIX_LOOKAHEAD)
        n_keys = (TILE + 2 * SWA_WINDOW if unit[0] == "B" else 3 * TILE) + CTX_LEN
        outs.append(attend(unit, s_ref[n % (MIX_LOOKAHEAD + 1), 0:n_keys, :]))
    o_ref[0, :, 0:GROUP_WIDTH] = jnp.concatenate(outs[0:4], axis=0).T.astype(BF16)
    o_ref[0, :, GROUP_WIDTH:2 * GROUP_WIDTH] = jnp.concatenate(outs[4:8], axis=0).T.astype(BF16)

    has_prev = (j > 1).astype(F32)
    has_next = ((j > 0) & (j < n_tiles - 1)).astype(F32)
    dc = dc_ref[0].astype(F32)
    dp = dp_ref[0].astype(F32)[HALO_BF16 - HALO_F32:]
    dn = dn_ref[0].astype(F32)[:HALO_F32]
    gw = GROUP_WIDTH
    z = jnp.concatenate([dp[:, gw:2 * gw] * dp[:, 2 * gw:] * has_prev,
                         dc[:, gw:2 * gw] * dc[:, 2 * gw:],
                         dn[:, gw:2 * gw] * dn[:, 2 * gw:] * has_next], axis=0)
    cw = cw_ref[...]
    c0 = HALO_F32
    conv = (z[c0 - 1:c0 - 1 + TILE] * cw[0:1] + z[c0:c0 + TILE] * cw[1:2]
            + z[c0 + 1:c0 + 1 + TILE] * cw[2:3])
    o_ref[0, :, 4 * LANES:6 * LANES] = (dc[:, 0:gw] * conv).astype(BF16)


def _tile_variants(nt):
    return (0, 1, 2, nt - 1)


def _local_mix_call(sink, bqt, bk, bvt, cqt, ck, cvt, bmask, ctab, hd, conv_w):
    nb, rows, _ = bk.shape
    nt = rows // TILE
    variant = lambda j: jnp.where(j == 0, 0, jnp.where(j == 1, 1, jnp.where(j == nt - 1, 3, 2)))
    hb = TILE // SWA_WINDOW
    h16 = TILE // HALO_BF16
    prev_j = lambda j: jnp.maximum(j - 1, 0)
    next_j = lambda j: jnp.minimum(j + 1, nt - 1)
    cur = lambda w: pl.BlockSpec((1, TILE, w), lambda b, j: (b, j, 0))
    prev = lambda w: pl.BlockSpec((1, TILE, w), lambda b, j: (b, prev_j(j), 0))
    nxt = lambda w: pl.BlockSpec((1, TILE, w), lambda b, j: (b, next_j(j), 0))
    ctx = lambda w: pl.BlockSpec((1, TILE, w), lambda b, j: (b, 0, 0))
    tcur = lambda w: pl.BlockSpec((1, 1, w, TILE), lambda b, j: (b, j, 0, 0))
    tprev = lambda w: pl.BlockSpec((1, 1, w, TILE), lambda b, j: (b, prev_j(j), 0, 0))
    tnxt = lambda w: pl.BlockSpec((1, 1, w, TILE), lambda b, j: (b, next_j(j), 0, 0))
    tctx = lambda w: pl.BlockSpec((1, 1, w, TILE), lambda b, j: (b, 0, 0, 0))
    return pl.pallas_call(
        _local_mix_kernel,
        grid=(nb, nt),
        in_specs=[
            pl.BlockSpec(memory_space=pltpu.SMEM),
            tcur(256),
            pl.BlockSpec((1, SWA_WINDOW, 128), lambda b, j: (b, jnp.maximum(j * hb - 1, 0), 0)),
            cur(128),
            pl.BlockSpec((1, SWA_WINDOW, 128), lambda b, j: (b, jnp.minimum((j + 1) * hb, nt * hb - 1), 0)),
            ctx(128),
            tprev(128), tcur(128), tnxt(128), tctx(128),
            tcur(256),
            prev(256), cur(256), nxt(256), ctx(256),
            tprev(256), tcur(256), tnxt(256), tctx(256),
            pl.BlockSpec((1, TILE + 2 * SWA_WINDOW + CTX_LEN, TILE), lambda b, j: (variant(j), 0, 0)),
            pl.BlockSpec((1, NA_HEADS, 3 * TILE + CTX_LEN, TILE),
                         lambda b, j: (variant(j), 0, 0, 0)),
            pl.BlockSpec((1, HALO_BF16, 768), lambda b, j: (b, jnp.maximum(j * h16 - 1, 0), 0)),
            cur(768),
            pl.BlockSpec((1, HALO_BF16, 768), lambda b, j: (b, jnp.minimum((j + 1) * h16, nt * h16 - 1), 0)),
            _resident((3, GROUP_WIDTH)),
        ],
        out_specs=pl.BlockSpec((1, TILE, 768), lambda b, j: (b, j, 0)),
        out_shape=jax.ShapeDtypeStruct((nb, rows, 768), BF16),
        scratch_shapes=[pltpu.VMEM((MIX_LOOKAHEAD + 1, 3 * TILE + CTX_LEN, TILE), F32)],
        compiler_params=_params(2),
        name="local_mix",
    )(sink, bqt, bk, bk, bk, bk, bvt, bvt, bvt, bvt, cqt, ck, ck, ck, ck, cvt, cvt, cvt, cvt,
      bmask, ctab, hd, hd, hd, conv_w)


EXT = TILE + 2 * HALO_F32


def _halo_f32(prev_ref, next_ref):
    p = prev_ref[0].astype(F32)
    n = next_ref[0].astype(F32)
    return p[p.shape[0] - HALO_F32:], n[:HALO_F32]


def _ffn_tile(final, j, n_tiles, xp_ref, xc_ref, xn_ref, ap_ref, ac_ref, an_ref,
              mp_ref, mc_ref, mn_ref, mod_ref, wo_ref, nw_ref, wu_ref, cw_ref, cb_ref, wd_ref,
              fw_ref, u_ref):
    mod = mod_ref[0]
    g1 = mod[:, 2 * D_MODEL:3 * D_MODEL]
    sh2 = mod[:, 3 * D_MODEL:4 * D_MODEL]
    sc2 = mod[:, 4 * D_MODEL:5 * D_MODEL]
    g2 = mod[:, 5 * D_MODEL:6 * D_MODEL]

    ap, an = _halo_f32(ap_ref, an_ref)
    mp, mn = _halo_f32(mp_ref, mn_ref)
    mix = jnp.concatenate([
        jnp.concatenate([ap, mp], axis=1),
        jnp.concatenate([ac_ref[0], mc_ref[0]], axis=1).astype(F32),
        jnp.concatenate([an, mn], axis=1)], axis=0).astype(BF16)
    x = jnp.concatenate([xp_ref[0], xc_ref[0], xn_ref[0]], axis=0)
    x1 = x + g1 * _dot(mix, wo_ref[...])
    f = _rmsnorm_mod(x1, nw_ref[...], sh2, sc2).astype(BF16)

    has_prev = (j > 1).astype(F32)
    has_next = ((j > 0) & (j < n_tiles - 1)).astype(F32)
    lo = HALO_F32 - 1
    hi = HALO_F32 + TILE
    u = _dot(f, wu_ref[...])
    slabs = []
    for t in range(2 * D_FF // LANES):
        cols = slice(t * LANES, (t + 1) * LANES)
        u_ref[t] = u[:, cols]
        u_ref[t, lo:lo + 1, :] = u_ref[t, lo:lo + 1, :] * has_prev
        u_ref[t, hi:hi + 1, :] = u_ref[t, hi:hi + 1, :] * has_next
        cw = cw_ref[:, cols]
        slabs.append(u_ref[t, pl.ds(lo, TILE), :] * cw[0:1]
                     + u_ref[t, pl.ds(lo + 1, TILE), :] * cw[1:2]
                     + u_ref[t, pl.ds(lo + 2, TILE), :] * cw[2:3] + cb_ref[:, cols])
    g = jnp.concatenate(slabs[:D_FF // LANES], axis=1)
    v = jnp.concatenate(slabs[D_FF // LANES:], axis=1)
    act = (g * (1.0 / (1.0 + jnp.exp(-g))) * v).astype(BF16)
    x2 = x1[HALO_F32:HALO_F32 + TILE] + g2 * _dot(act, wd_ref[...])
    if final:
        ms = jnp.mean(x2 * x2, axis=-1, keepdims=True)
        x2 = x2 * lax.rsqrt(ms + EPS) * fw_ref[...]
    return x2


N_TILE_REFS = 10


def _out_ffn_kernel(final, *refs):
    tile_refs, weights, (o_ref, u_ref) = refs[:N_TILE_REFS], refs[N_TILE_REFS:-2], refs[-2:]
    o_ref[0] = _ffn_tile(final, pl.program_id(1), pl.num_programs(1), *tile_refs, *weights, u_ref)


def _out_ffn_call(xall, oa, obcd, mod, w_out, norm_w, w_up, conv_w, conv_b, w_down, final_w, final):
    nb, rows, _ = xall.shape
    nt = rows // TILE
    h8 = TILE // HALO_F32
    h16 = TILE // HALO_BF16

    def trio(w, halo, per_tile):
        return [
            pl.BlockSpec((1, halo, w), lambda b, j: (b, jnp.maximum(j * per_tile - 1, 0), 0)),
            pl.BlockSpec((1, TILE, w), lambda b, j: (b, j, 0)),
            pl.BlockSpec((1, halo, w), lambda b, j: (
                b, jnp.minimum((j + 1) * per_tile, nt * per_tile - 1), 0)),
        ]

    tile_specs = [*trio(D_MODEL, HALO_F32, h8), *trio(256, HALO_BF16, h16),
                  *trio(768, HALO_BF16, h16), _mod_spec(nb)]
    weight_specs = [
        _resident((D_MODEL, D_MODEL)),
        _resident((1, D_MODEL)),
        _resident((D_MODEL, 2 * D_FF)),
        _resident((3, 2 * D_FF)),
        _resident((1, 2 * D_FF)),
        _resident((D_FF, D_MODEL)),
        _resident((1, D_MODEL)),
    ]
    if final:
        out_spec = pl.BlockSpec((1, TILE, D_MODEL), lambda b, j: (b, jnp.maximum(j - 1, 0), 0))
        out_shape = jax.ShapeDtypeStruct((nb, rows - CTX_LEN, D_MODEL), F32)
    else:
        out_spec = pl.BlockSpec((1, TILE, D_MODEL), lambda b, j: (b, j, 0))
        out_shape = jax.ShapeDtypeStruct((nb, rows, D_MODEL), F32)
    return pl.pallas_call(
        functools.partial(_out_ffn_kernel, final),
        grid=(nb, nt),
        in_specs=[*tile_specs, *weight_specs],
        out_specs=out_spec,
        out_shape=out_shape,
        scratch_shapes=[pltpu.VMEM((2 * D_FF // LANES, EXT, LANES), F32)],
        compiler_params=_params(2),
        name="out_ffn_final" if final else "out_ffn",
    )(xall, xall, xall, oa, oa, oa, obcd, obcd, obcd, mod, w_out, norm_w, w_up, conv_w, conv_b,
      w_down, final_w)


def _rope_tables(n_lat, dim):
    quarter = dim // 4
    t = np.arange(n_lat)
    pos = np.stack([t // GRID_W, t % GRID_W], axis=1).astype(np.float32)
    freqs = jnp.asarray(ROPE_THETA, F32) ** (-(jnp.arange(quarter, dtype=F32) / quarter))
    lane = np.arange(LANES) % dim
    axis = lane // (dim // 2)
    fidx = lane % quarter
    sign = np.where((lane % (dim // 2)) < quarter, -1.0, 1.0).astype(np.float32)
    ang = jnp.asarray(pos)[:, axis] * freqs[fidx][None, :]
    cos = jnp.concatenate([jnp.ones((CTX_LEN, LANES), F32), jnp.cos(ang)], axis=0)
    sin = jnp.concatenate([jnp.zeros((CTX_LEN, LANES), F32), jnp.sin(ang) * sign[None, :]], axis=0)
    return cos, sin


def _na_bias_table(rpb):
    rpt = TILE // GRID_W
    n_heads, n_dr, n_dc = rpb.shape
    cq = np.arange(GRID_W)[:, None]
    ck = np.arange(GRID_W)[None, :]
    dc = np.clip(ck - cq, -(NA_COLS - 1), NA_COLS - 1) + NA_COLS - 1
    col_start = np.clip(cq - NA_COLS // 2, 0, GRID_W - NA_COLS)
    col_ok = (ck >= col_start) & (ck < col_start + NA_COLS)
    pick = dc[None, :, :] == np.arange(n_dc)[:, None, None]
    e = jnp.sum(jnp.where(pick[None, None], rpb.astype(F32)[:, :, :, None, None], 0.0), axis=2)
    e = jnp.where(col_ok[None, None], e, NEG_INF)
    blocks = []
    for rq in range(rpt):
        lo = NA_ROWS - 1 - rpt - rq
        blocks.append(e[:, lo:lo + 3 * rpt].transpose(0, 1, 3, 2))
    local = jnp.stack(blocks, axis=3).reshape(n_heads, 3 * TILE, TILE)
    return jnp.concatenate([local, jnp.zeros((n_heads, CTX_LEN, TILE), F32)], axis=1)


def _window_masks(n_lat):
    nt = 1 + n_lat // TILE
    half = SWA_WINDOW
    rpt = TILE // GRID_W
    n_rows = n_lat // GRID_W
    rq = np.arange(TILE)[None, :]
    kb = np.arange(TILE + 2 * half + CTX_LEN)[:, None]
    kc = np.arange(3 * TILE + CTX_LEN)[:, None]
    masks_b, masks_c = [], []
    for j in _tile_variants(nt):
        kpos = (j - 1) * TILE - half + kb
        band = (np.abs(kb - half - rq) <= SWA_WINDOW) & (kpos >= 0) & (kpos < n_lat) & (j > 0)
        masks_b.append(band | (kb >= TILE + 2 * half))
        r = (j - 1) * rpt + rq // GRID_W
        kr = (j - 2) * rpt + kc // GRID_W
        rs = np.clip(r - NA_ROWS // 2, 0, n_rows - NA_ROWS)
        masks_c.append(((kr >= rs) & (kr < rs + NA_ROWS) & (j > 0)) | (kc >= 3 * TILE))
    to_add = lambda ms: jnp.where(jnp.asarray(np.stack(ms)), 0.0, NEG_INF).astype(F32)
    return to_add(masks_b), to_add(masks_c)


def _swa_head_order(w, axis, start):
    def blk(lo, hi):
        return lax.slice_in_dim(w, lo, hi, axis=axis)
    heads = [blk(start + h * HEAD_DIM, start + (h + 1) * HEAD_DIM) for h in (0, 2, 1, 3)]
    return jnp.concatenate(
        [blk(0, start), *heads, blk(start + SWA_HEADS * HEAD_DIM, w.shape[axis])], axis=axis)


def kernel(x, c, ctx, c_ctx, norm_mix_w, norm_ffn_w, w_mod, b_mod, w_in, w_out, diff_lambda_q1, diff_lambda_k1, diff_lambda_q2, diff_lambda_k2, diff_subln_w, swa_sink, na_rpb, sconv_w, ffn_w_up, ffn_conv_w, ffn_conv_b, ffn_w_down, final_norm_w):
    nb, n_lat, d = x.shape
    depth = w_in.shape[0]
    assert d == D_MODEL and ctx.shape[1] == CTX_LEN and n_lat % TILE == 0 and nb < 8
    assert n_lat // GRID_W >= NA_ROWS and n_lat % (max(2, BLOCKS_PER_STEP) * KV_BLOCK) == 0

    xall = jnp.concatenate([ctx, x], axis=1)
    cin = jnp.concatenate([c, c_ctx[None, :], jnp.zeros((8 - nb - 1, d), F32)], axis=0)
    mods = _mod_call(cin, w_mod, b_mod)

    tabs = (*_rope_tables(n_lat, DIFF_QK_DIM), *_rope_tables(n_lat, HEAD_DIM))
    mask_b, mask_c = _window_masks(n_lat)
    for l in range(depth):
        mod = mods[l].reshape(8, 1, 6 * D_MODEL)
        w_in_l = _swa_head_order(w_in[l], 1, 3 * GROUP_WIDTH).astype(BF16)
        w_out_l = _swa_head_order(w_out[l], 0, GROUP_WIDTH).astype(BF16)
        qa, ka, vat, bqt, bk, bvt, cqt, ck, cvt, hd = _in_proj_call(
            xall, mod, norm_mix_w[l][None, :], w_in_l, tabs)

        lambda_init = 0.8 - 0.6 * math.exp(-0.3 * l)
        lam_init = jnp.asarray([lambda_init, 1.0 - lambda_init], F32)
        lam_vecs = jnp.stack([diff_lambda_q1[l], diff_lambda_k1[l],
                              diff_lambda_q2[l], diff_lambda_k2[l]])
        subw = jnp.broadcast_to(diff_subln_w[l][:, None], (HEAD_DIM, QA_TILE))
        oa = _diff_attn_call(lam_init, lam_vecs, qa, ka, vat, subw)

        ctab = (_na_bias_table(na_rpb[l])[None] + mask_c[:, None]) * LOG2E
        obcd = _local_mix_call(swa_sink[l], bqt, bk, bvt, cqt, ck, cvt, mask_b, ctab,
                               hd, sconv_w[l])

        xall = _out_ffn_call(
            xall, oa, obcd, mod, w_out_l, norm_ffn_w[l][None, :],
            ffn_w_up[l].astype(BF16), ffn_conv_w[l], ffn_conv_b[l][None, :],
            ffn_w_down[l].astype(BF16), final_norm_w[None, :], final=(l == depth - 1))
    return xall
```

```python
import functools
import math

import numpy as np
import jax
import jax.numpy as jnp
from jax import lax
from jax.experimental import pallas as pl
from jax.experimental.pallas import tpu as pltpu

D_MODEL = 1024
GRID_W = 64
CTX_LEN = 256
HEAD_DIM = 64
GROUP_WIDTH = 256
DIFF_HEADS = 4
DIFF_QK_DIM = 32
SWA_HEADS = 4
SWA_WINDOW = 128
NA_HEADS = 4
NA_ROWS = 8
NA_COLS = 16
D_FF = 2816
IN_COLS = 2816
ROPE_THETA = 10000.0
EPS = 1e-6
NEG_INF = -1e30
LOG2E = math.log2(math.e)

TILE = 256
HALO_F32 = 8
HALO_BF16 = 16
LANES = 128
VMEM_LIMIT = 56 * 1024 * 1024

F32 = jnp.float32
BF16 = jnp.bfloat16


def _params(n_grid):
    return pltpu.CompilerParams(
        dimension_semantics=("arbitrary",) * n_grid, vmem_limit_bytes=VMEM_LIMIT)


def _resident(shape):
    nd = len(shape)
    return pl.BlockSpec(shape, lambda *_: (0,) * nd, pipeline_mode=pl.Buffered(1))


def _mod_spec(n_batch):
    return pl.BlockSpec((1, 1, 6 * D_MODEL), lambda b, j: (jnp.where(j == 0, n_batch, b), 0, 0))


def _rmsnorm_mod(x, w, shift, scale):
    ms = jnp.mean(x * x, axis=-1, keepdims=True)
    return (x * lax.rsqrt(ms + EPS) * w) * (1.0 + scale) + shift


def _dot(a, b):
    return jnp.dot(a, b, preferred_element_type=F32)


MOD_BLOCK = 1536


def _mod_kernel(c_ref, w_ref, b_ref, o_ref):
    c = c_ref[...]
    a = c * (1.0 / (1.0 + jnp.exp(-c)))
    o_ref[0] = _dot(a.astype(BF16), w_ref[0].astype(BF16)) + b_ref[0]


def _mod_call(cin, w_mod, b_mod):
    depth = w_mod.shape[0]
    n = 6 * D_MODEL
    return pl.pallas_call(
        _mod_kernel,
        grid=(depth, n // MOD_BLOCK),
        in_specs=[
            pl.BlockSpec((8, D_MODEL), lambda l, k: (0, 0)),
            pl.BlockSpec((1, D_MODEL, MOD_BLOCK), lambda l, k: (l, 0, k)),
            pl.BlockSpec((1, 1, MOD_BLOCK), lambda l, k: (l, 0, k)),
        ],
        out_specs=pl.BlockSpec((1, 8, MOD_BLOCK), lambda l, k: (l, 0, k)),
        out_shape=jax.ShapeDtypeStruct((depth, 8, n), F32),
        compiler_params=_params(2),
        name="mod_proj",
    )(cin, w_mod, b_mod.reshape(depth, 1, n))


def _rope(x, cos, sin, half):
    lane = lax.broadcasted_iota(jnp.int32, x.shape, 1)
    first = (lane % (2 * half)) < half
    partner = jnp.where(first, pltpu.roll(x, LANES - half, 1), pltpu.roll(x, half, 1))
    return x * cos + partner * sin


def _in_proj_kernel(x_ref, mod_ref, nw_ref, w_ref, cosa_ref, sina_ref, cosb_ref, sinb_ref,
                    qa_ref, ka_ref, vat_ref, bqt_ref, bk_ref, bvt_ref, cqt_ref, ck_ref, cvt_ref,
                    hd_ref):
    mod = mod_ref[0]
    shift = mod[:, 0:D_MODEL]
    scale = mod[:, D_MODEL:2 * D_MODEL]
    z = _rmsnorm_mod(x_ref[0], nw_ref[...], shift, scale)
    h = _dot(z.astype(BF16), w_ref[...])

    def sec(k, n=1):
        return h[:, k * LANES:(k + n) * LANES]

    cosa, sina = cosa_ref[...], sina_ref[...]
    cosb, sinb = cosb_ref[...], sinb_ref[...]
    qa_scale = DIFF_QK_DIM ** -0.5 * LOG2E
    q_scale = HEAD_DIM ** -0.5 * LOG2E
    ha = DIFF_QK_DIM // 4
    hb = HEAD_DIM // 4
    for t in range(2):
        qa_ref[0, :, t * LANES:(t + 1) * LANES] = (
            _rope(sec(t), cosa, sina, ha) * qa_scale).astype(BF16)
        ka_ref[0, :, t * LANES:(t + 1) * LANES] = _rope(sec(2 + t), cosa, sina, ha).astype(BF16)
    vat_ref[0, 0] = sec(4, 2).T.astype(BF16)
    for t in range(2):
        bqt_ref[0, 0, t * LANES:(t + 1) * LANES, :] = (
            _rope(sec(6 + t), cosb, sinb, hb) * q_scale).T.astype(BF16)
    bk_ref[0] = _rope(sec(8), cosb, sinb, hb).astype(BF16)
    bvt_ref[0, 0] = sec(9).T.astype(BF16)
    cqt_ref[0, 0] = (sec(10, 2) * q_scale).T.astype(BF16)
    ck_ref[0] = sec(12, 2).astype(BF16)
    cvt_ref[0, 0] = sec(14, 2).T.astype(BF16)
    hd_ref[0] = sec(16, 6).astype(BF16)


def _in_proj_call(xall, mod, norm_w, w_in, tabs):
    nb, rows, _ = xall.shape
    nt = rows // TILE
    tile = lambda w: pl.BlockSpec((1, TILE, w), lambda b, j: (b, j, 0))
    ttile = lambda w: pl.BlockSpec((1, 1, w, TILE), lambda b, j: (b, j, 0, 0))
    tab = pl.BlockSpec((TILE, LANES), lambda b, j: (j, 0))
    sds = lambda w: jax.ShapeDtypeStruct((nb, rows, w), BF16)
    tsds = lambda w: jax.ShapeDtypeStruct((nb, nt, w, TILE), BF16)
    return pl.pallas_call(
        _in_proj_kernel,
        grid=(nb, nt),
        in_specs=[
            pl.BlockSpec((1, TILE, D_MODEL), lambda b, j: (b, j, 0)),
            _mod_spec(nb),
            _resident((1, D_MODEL)),
            _resident((D_MODEL, IN_COLS)),
            tab, tab, tab, tab,
        ],
        out_specs=[
            tile(256), tile(256), ttile(256),
            ttile(256), tile(128), ttile(128),
            ttile(256), tile(256), ttile(256),
            tile(768),
        ],
        out_shape=[
            sds(256), sds(256), tsds(256),
            tsds(256), sds(128), tsds(128),
            tsds(256), sds(256), tsds(256),
            sds(768),
        ],
        compiler_params=_params(2),
        name="in_proj",
    )(xall, mod, norm_w, w_in, *tabs)


QA_TILE = 256
KV_BLOCK = 512
BLOCKS_PER_STEP = 16
LOOKAHEAD = 2
EXP_CAP = 100.0
PV_ROWS = HEAD_DIM + HALO_BF16


def _diff_attn_kernel(li_ref, lv_ref, q_ref, k_ref, vt_ref, sw_ref, o_ref,
                      qst_ref, sa_ref, sb_ref, mca_ref, mcb_ref, m_ref, acc_ref, gap_ref):
    i = pl.program_id(1)
    s_refs = (sa_ref, sb_ref)
    mc_refs = (mca_ref, mcb_ref)
    q = q_ref[0].astype(F32)
    lane = lax.broadcasted_iota(jnp.int32, q.shape, 1) // DIFF_QK_DIM
    for h in range(DIFF_HEADS):
        qs = jnp.concatenate([jnp.where(lane == 2 * h + c, q, 0.0) for c in range(2)], axis=0)
        qst_ref[h] = qs.T.astype(BF16)
    n_cols = 2 * QA_TILE
    n_blk = (k_ref.shape[1] - CTX_LEN) // KV_BLOCK
    tiles_per_blk = KV_BLOCK // TILE

    def value_rows(vt, h):
        ones = jnp.ones((HALO_BF16, vt.shape[1]), BF16)
        return jnp.concatenate([vt[h * HEAD_DIM:(h + 1) * HEAD_DIM], ones], axis=0)

    def ctx_scores(h):
        return _dot(k_ref[0, 0:CTX_LEN, :], qst_ref[h])

    def ctx_init(h, s):
        m0 = jnp.max(s, axis=0, keepdims=True)
        acc_ref[h] = _dot(value_rows(vt_ref[0, 0], h), jnp.exp2(s - m0).astype(BF16))
        m_ref[h] = m0

    def score_head(blk, slot, h):
        row0 = pl.multiple_of(CTX_LEN + (blk - 1) * KV_BLOCK, TILE)
        s = _dot(k_ref[0, pl.ds(row0, KV_BLOCK), :], qst_ref[h])
        s_refs[slot][h] = s
        mc_refs[slot][h] = jnp.max(s, axis=0, keepdims=True)

    def block_step(score_blk, acc_blk):
        if acc_blk is not None:
            ablk, aslot = acc_blk
            t0 = 1 + (ablk - 1) * tiles_per_blk
            vt = jnp.concatenate([vt_ref[0, t0 + u] for u in range(tiles_per_blk)], axis=1)
        for h in range(DIFF_HEADS):
            if score_blk is not None:
                score_head(*score_blk, h)
            if acc_blk is not None:
                m_old = m_ref[h]
                m_new = jnp.maximum(m_old, mc_refs[aslot][h])
                p = jnp.exp2(s_refs[aslot][h] - m_new).astype(BF16)
                acc_ref[h] = acc_ref[h] * jnp.exp2(m_old - m_new) + _dot(value_rows(vt, h), p)
                m_ref[h] = m_new

    def two_pass_latent():
        scores0 = [ctx_scores(h) for h in range(DIFF_HEADS)]
        for h in range(DIFF_HEADS):
            score_head(1, 0, h)
            ctx_init(h, scores0[h])

        def step(it, carry):
            b = 2 * it + 1
            block_step((b + 1, 1), (b, 0))
            block_step((b + 2, 0), (b + 1, 1))
            return carry

        lax.fori_loop(0, n_blk // 2 - 1, step, 0)
        block_step((n_blk, 1), (n_blk - 1, 0))
        block_step(None, (n_blk, 1))

    def one_pass_latent():
        scores0 = [ctx_scores(h) for h in range(DIFF_HEADS)]
        for h in range(DIFF_HEADS):
            ctx_init(h, scores0[h])
            gap_ref[h] = jnp.zeros(gap_ref.shape[1:], F32)

        def scores(unit):
            blk, h, c = unit
            row0 = pl.multiple_of(CTX_LEN + (blk - 1) * KV_BLOCK, TILE)
            return _dot(k_ref[0, pl.ds(row0, KV_BLOCK), :],
                        qst_ref[h, :, c * QA_TILE:(c + 1) * QA_TILE])

        def consume(unit, s, vt):
            _, h, c = unit
            cols = slice(c * QA_TILE, (c + 1) * QA_TILE)
            m_old = m_ref[h, :, cols]
            m_cur = jnp.max(s, axis=0, keepdims=True)
            pv = _dot(value_rows(vt, h), jnp.exp2(s - m_old).astype(BF16))
            m_new = jnp.maximum(m_old, m_cur)
            acc_ref[h, :, cols] = (acc_ref[h, :, cols] + pv) * jnp.exp2(m_old - m_new)
            m_ref[h, :, cols] = m_new
            gap_ref[h, :, cols] = jnp.maximum(gap_ref[h, :, cols], m_cur - m_old)

        def step(it, carry):
            b0 = it * BLOCKS_PER_STEP + 1
            units = [(b0 + u, h, c) for u in range(BLOCKS_PER_STEP)
                     for h in range(DIFF_HEADS) for c in range(2)]
            pending = [scores(u) for u in units[:LOOKAHEAD]]
            for n, unit in enumerate(units):
                if n + LOOKAHEAD < len(units):
                    pending.append(scores(units[n + LOOKAHEAD]))
                if unit[1] == 0 and unit[2] == 0:
                    t0 = 1 + (unit[0] - 1) * tiles_per_blk
                    vt = jnp.concatenate(
                        [vt_ref[0, t0 + u] for u in range(tiles_per_blk)], axis=1)
                consume(unit, pending.pop(0), vt)
            return carry

        lax.fori_loop(0, n_blk // BLOCKS_PER_STEP, step, 0)

    def finalize():
        lv = lv_ref[...]
        lam = (jnp.exp(jnp.sum(lv[0:1] * lv[1:2], axis=1, keepdims=True))
               - jnp.exp(jnp.sum(lv[2:3] * lv[3:4], axis=1, keepdims=True)) + li_ref[0])
        out_scale = li_ref[1]
        outs = []
        for h in range(DIFF_HEADS):
            a = acc_ref[h]
            o1 = a[0:HEAD_DIM, 0:QA_TILE] / a[HEAD_DIM:HEAD_DIM + 1, 0:QA_TILE]
            o2 = a[0:HEAD_DIM, QA_TILE:n_cols] / a[HEAD_DIM:HEAD_DIM + 1, QA_TILE:n_cols]
            o = o1 - lam * o2
            ms = jnp.mean(o * o, axis=0, keepdims=True)
            outs.append(o * lax.rsqrt(ms + EPS) * sw_ref[...] * out_scale)
        o_ref[0] = jnp.concatenate(outs, axis=0).T.astype(BF16)

    @pl.when(i < CTX_LEN // QA_TILE)
    def _():
        for h in range(DIFF_HEADS):
            ctx_init(h, ctx_scores(h))
        finalize()

    @pl.when(i >= CTX_LEN // QA_TILE)
    def _():
        one_pass_latent()
        finalize()

        @pl.when(jnp.max(gap_ref[...]) > EXP_CAP)
        def _():
            two_pass_latent()
            finalize()


def _diff_attn_call(lam_init, lam_vecs, qa, ka, vat, subw):
    nb, rows, _ = qa.shape
    nt = rows // TILE
    return pl.pallas_call(
        _diff_attn_kernel,
        grid=(nb, rows // QA_TILE),
        in_specs=[
            pl.BlockSpec(memory_space=pltpu.SMEM),
            _resident((4, DIFF_QK_DIM)),
            pl.BlockSpec((1, QA_TILE, 256), lambda b, i: (b, i, 0)),
            pl.BlockSpec((1, rows, 256), lambda b, i: (b, 0, 0)),
            pl.BlockSpec((1, nt, 256, TILE), lambda b, i: (b, 0, 0, 0)),
            _resident((HEAD_DIM, QA_TILE)),
        ],
        out_specs=pl.BlockSpec((1, QA_TILE, 256), lambda b, i: (b, i, 0)),
        out_shape=jax.ShapeDtypeStruct((nb, rows, 256), BF16),
        scratch_shapes=[
            pltpu.VMEM((DIFF_HEADS, 256, 2 * QA_TILE), BF16),
            pltpu.VMEM((DIFF_HEADS, KV_BLOCK, 2 * QA_TILE), F32),
            pltpu.VMEM((DIFF_HEADS, KV_BLOCK, 2 * QA_TILE), F32),
            pltpu.VMEM((DIFF_HEADS, 1, 2 * QA_TILE), F32),
            pltpu.VMEM((DIFF_HEADS, 1, 2 * QA_TILE), F32),
            pltpu.VMEM((DIFF_HEADS, 1, 2 * QA_TILE), F32),
            pltpu.VMEM((DIFF_HEADS, PV_ROWS, 2 * QA_TILE), F32),
            pltpu.VMEM((DIFF_HEADS, 1, 2 * QA_TILE), F32),
        ],
        compiler_params=_params(2),
        name="diff_attn",
    )(lam_init, lam_vecs, qa, ka, vat, subw)


MIX_LOOKAHEAD = 2


def _head_rows(qt, k):
    row = lax.broadcasted_iota(jnp.int32, qt.shape, 0) // HEAD_DIM
    return jnp.where(row == k, qt, jnp.zeros_like(qt))


def _value_rows(vt_parts, rows):
    v = jnp.concatenate([part[rows] for part in vt_parts], axis=1)
    return jnp.concatenate([v, jnp.ones((HALO_BF16, v.shape[1]), BF16)], axis=0)


def _local_mix_kernel(sink_ref, bqt_ref, bkp_ref, bkc_ref, bkn_ref, bkx_ref,
                      bvtp_ref, bvtc_ref, bvtn_ref, bvtx_ref,
                      cqt_ref, ckp_ref, ckc_ref, ckn_ref, ckx_ref,
                      cvtp_ref, cvtc_ref, cvtn_ref, cvtx_ref, bmask_ref, ctab_ref,
                      dp_ref, dc_ref, dn_ref, cw_ref, o_ref, s_ref):
    j = pl.program_id(1)
    n_tiles = pl.num_programs(1)
    half = SWA_WINDOW

    k_b = jnp.concatenate([bkp_ref[0], bkc_ref[0], bkn_ref[0], bkx_ref[0]], axis=0)
    vt_b = [bvtp_ref[0, 0][:, half:], bvtc_ref[0, 0], bvtn_ref[0, 0][:, :half], bvtx_ref[0, 0]]
    vt_c = [cvtp_ref[0, 0], cvtc_ref[0, 0], cvtn_ref[0, 0], cvtx_ref[0, 0]]

    units = [(mixer, p, k) for mixer in "BC" for p in range(2) for k in range(2)]

    def scores(unit):
        mixer, p, k = unit
        rows = slice(p * LANES, (p + 1) * LANES)
        if mixer == "B":
            return _dot(k_b, _head_rows(bqt_ref[0, 0, rows, :], k))
        k_c = jnp.concatenate([ckp_ref[0, :, rows], ckc_ref[0, :, rows],
                               ckn_ref[0, :, rows], ckx_ref[0, :, rows]], axis=0)
        return _dot(k_c, _head_rows(cqt_ref[0, 0, rows, :], k))

    def attend(unit, s):
        mixer, p, k = unit
        feat = slice(k * HEAD_DIM, (k + 1) * HEAD_DIM)
        if mixer == "B":
            s = s + bmask_ref[0]
            sink = sink_ref[2 * k + p] * LOG2E
            m = jnp.maximum(jnp.max(s, axis=0, keepdims=True), sink)
            pv = _dot(_value_rows(vt_b, feat), jnp.exp2(s - m).astype(BF16))
            return pv[0:HEAD_DIM] / (pv[HEAD_DIM:HEAD_DIM + 1] + jnp.exp2(sink - m))
        s = s + ctab_ref[0, 2 * p + k]
        m = jnp.max(s, axis=0, keepdims=True)
        rows = slice(p * LANES + k * HEAD_DIM, p * LANES + (k + 1) * HEAD_DIM)
        pv = _dot(_value_rows(vt_c, rows), jnp.exp2(s - m).astype(BF16))
        return pv[0:HEAD_DIM] / pv[HEAD_DIM:HEAD_DIM + 1]

    def write_out(outs):
        o_ref[0, :, 0:GROUP_WIDTH] = jnp.concatenate(outs[0:4], axis=0).T.astype(BF16)
        o_ref[0, :, GROUP_WIDTH:2 * GROUP_WIDTH] = jnp.concatenate(outs[4:8], axis=0).T.astype(BF16)

    def two_pass():
        def park(n):
            s = scores(units[n])
            s_ref[n % (MIX_LOOKAHEAD + 1), 0:s.shape[0], :] = s

        for n in range(MIX_LOOKAHEAD):
            park(n)
        outs = []
        for n, unit in enumerate(units):
            if n + MIX_LOOKAHEAD < len(units):
                park(n + MIX_LOOKAHEAD)
            n_keys = (TILE + 2 * SWA_WINDOW if unit[0] == "B" else 3 * TILE) + CTX_LEN
            outs.append(attend(unit, s_ref[n % (MIX_LOOKAHEAD + 1), 0:n_keys, :]))
        write_out(outs)

    def unit_operands(unit):
        mixer, p, k = unit
        rows = slice(p * LANES, (p + 1) * LANES)
        if mixer == "B":
            n_loc = TILE + 2 * half
            return (_head_rows(bqt_ref[0, 0, rows, :], k), k_b[0:n_loc], k_b[n_loc:],
                    bmask_ref[0, 0:n_loc, :], vt_b, slice(k * HEAD_DIM, (k + 1) * HEAD_DIM))
        k_loc = jnp.concatenate([ckp_ref[0, :, rows], ckc_ref[0, :, rows], ckn_ref[0, :, rows]],
                                axis=0)
        feat = slice(p * LANES + k * HEAD_DIM, p * LANES + (k + 1) * HEAD_DIM)
        return (_head_rows(cqt_ref[0, 0, rows, :], k), k_loc, ckx_ref[0, :, rows],
                ctab_ref[0, 2 * p + k, 0:3 * TILE, :], vt_c, feat)

    def one_pass_scores(unit):
        qt, k_loc, k_ctx, _, _, _ = unit_operands(unit)
        return _dot(k_ctx, qt), _dot(k_loc, qt)

    def one_pass_attend(unit, s_ctx, s_loc):
        mixer, p, k = unit
        _, _, _, table, vt, feat = unit_operands(unit)
        m = jnp.max(s_ctx, axis=0, keepdims=True)
        if mixer == "B":
            sink = sink_ref[2 * k + p] * LOG2E
            m = jnp.maximum(m, sink)
        s_loc = s_loc + table
        excess = jnp.max(s_loc, axis=0, keepdims=True) - m
        vx = _value_rows(vt, feat)
        n_loc = s_loc.shape[0]
        pv = (_dot(vx[:, 0:n_loc], jnp.exp2(s_loc - m).astype(BF16))
              + _dot(vx[:, n_loc:], jnp.exp2(s_ctx - m).astype(BF16)))
        l = pv[HEAD_DIM:HEAD_DIM + 1]
        if mixer == "B":
            l = l + jnp.exp2(sink - m)
        return pv[0:HEAD_DIM] / l, excess

    pending = [one_pass_scores(units[0])]
    outs, worst = [], None
    for n, unit in enumerate(units):
        if n + 1 < len(units):
            pending.append(one_pass_scores(units[n + 1]))
        o, excess = one_pass_attend(unit, *pending.pop(0))
        outs.append(o)
        worst = excess if worst is None else jnp.maximum(worst, excess)
    write_out(outs)

    @pl.when(jnp.max(worst) > EXP_CAP)
    def _():
        two_pass()

    has_prev = (j > 1).astype(F32)
    has_next = ((j > 0) & (j < n_tiles - 1)).astype(F32)
    dc = dc_ref[0].astype(F32)
    dp = dp_ref[0].astype(F32)[HALO_BF16 - HALO_F32:]
    dn = dn_ref[0].astype(F32)[:HALO_F32]
    gw = GROUP_WIDTH
    z = jnp.concatenate([dp[:, gw:2 * gw] * dp[:, 2 * gw:] * has_prev,
                         dc[:, gw:2 * gw] * dc[:, 2 * gw:],
                         dn[:, gw:2 * gw] * dn[:, 2 * gw:] * has_next], axis=0)
    cw = cw_ref[...]
    c0 = HALO_F32
    conv = (z[c0 - 1:c0 - 1 + TILE] * cw[0:1] + z[c0:c0 + TILE] * cw[1:2]
            + z[c0 + 1:c0 + 1 + TILE] * cw[2:3])
    o_ref[0, :, 4 * LANES:6 * LANES] = (dc[:, 0:gw] * conv).astype(BF16)


def _tile_variants(nt):
    return (0, 1, 2, nt - 1)


def _local_mix_call(sink, bqt, bk, bvt, cqt, ck, cvt, bmask, ctab, hd, conv_w):
    nb, rows, _ = bk.shape
    nt = rows // TILE
    variant = lambda j: jnp.where(j == 0, 0, jnp.where(j == 1, 1, jnp.where(j == nt - 1, 3, 2)))
    hb = TILE // SWA_WINDOW
    h16 = TILE // HALO_BF16
    prev_j = lambda j: jnp.maximum(j - 1, 0)
    next_j = lambda j: jnp.minimum(j + 1, nt - 1)
    cur = lambda w: pl.BlockSpec((1, TILE, w), lambda b, j: (b, j, 0))
    prev = lambda w: pl.BlockSpec((1, TILE, w), lambda b, j: (b, prev_j(j), 0))
    nxt = lambda w: pl.BlockSpec((1, TILE, w), lambda b, j: (b, next_j(j), 0))
    ctx = lambda w: pl.BlockSpec((1, TILE, w), lambda b, j: (b, 0, 0))
    tcur = lambda w: pl.BlockSpec((1, 1, w, TILE), lambda b, j: (b, j, 0, 0))
    tprev = lambda w: pl.BlockSpec((1, 1, w, TILE), lambda b, j: (b, prev_j(j), 0, 0))
    tnxt = lambda w: pl.BlockSpec((1, 1, w, TILE), lambda b, j: (b, next_j(j), 0, 0))
    tctx = lambda w: pl.BlockSpec((1, 1, w, TILE), lambda b, j: (b, 0, 0, 0))
    return pl.pallas_call(
        _local_mix_kernel,
        grid=(nb, nt),
        in_specs=[
            pl.BlockSpec(memory_space=pltpu.SMEM),
            tcur(256),
            pl.BlockSpec((1, SWA_WINDOW, 128), lambda b, j: (b, jnp.maximum(j * hb - 1, 0), 0)),
            cur(128),
            pl.BlockSpec((1, SWA_WINDOW, 128), lambda b, j: (b, jnp.minimum((j + 1) * hb, nt * hb - 1), 0)),
            ctx(128),
            tprev(128), tcur(128), tnxt(128), tctx(128),
            tcur(256),
            prev(256), cur(256), nxt(256), ctx(256),
            tprev(256), tcur(256), tnxt(256), tctx(256),
            pl.BlockSpec((1, TILE + 2 * SWA_WINDOW + CTX_LEN, TILE), lambda b, j: (variant(j), 0, 0)),
            pl.BlockSpec((1, NA_HEADS, 3 * TILE + CTX_LEN, TILE),
                         lambda b, j: (variant(j), 0, 0, 0)),
            pl.BlockSpec((1, HALO_BF16, 768), lambda b, j: (b, jnp.maximum(j * h16 - 1, 0), 0)),
            cur(768),
            pl.BlockSpec((1, HALO_BF16, 768), lambda b, j: (b, jnp.minimum((j + 1) * h16, nt * h16 - 1), 0)),
            _resident((3, GROUP_WIDTH)),
        ],
        out_specs=pl.BlockSpec((1, TILE, 768), lambda b, j: (b, j, 0)),
        out_shape=jax.ShapeDtypeStruct((nb, rows, 768), BF16),
        scratch_shapes=[pltpu.VMEM((MIX_LOOKAHEAD + 1, 3 * TILE + CTX_LEN, TILE), F32)],
        compiler_params=_params(2),
        name="local_mix",
    )(sink, bqt, bk, bk, bk, bk, bvt, bvt, bvt, bvt, cqt, ck, ck, ck, ck, cvt, cvt, cvt, cvt,
      bmask, ctab, hd, hd, hd, conv_w)


EXT = TILE + 2 * HALO_F32


def _halo_f32(prev_ref, next_ref):
    p = prev_ref[0].astype(F32)
    n = next_ref[0].astype(F32)
    return p[p.shape[0] - HALO_F32:], n[:HALO_F32]


def _ffn_tile(final, j, n_tiles, xp_ref, xc_ref, xn_ref, ap_ref, ac_ref, an_ref,
              mp_ref, mc_ref, mn_ref, mod_ref, wo_ref, nw_ref, wu_ref, cw_ref, cb_ref, wd_ref,
              fw_ref, u_ref):
    mod = mod_ref[0]
    g1 = mod[:, 2 * D_MODEL:3 * D_MODEL]
    sh2 = mod[:, 3 * D_MODEL:4 * D_MODEL]
    sc2 = mod[:, 4 * D_MODEL:5 * D_MODEL]
    g2 = mod[:, 5 * D_MODEL:6 * D_MODEL]

    ap, an = _halo_f32(ap_ref, an_ref)
    mp, mn = _halo_f32(mp_ref, mn_ref)
    mix = jnp.concatenate([
        jnp.concatenate([ap, mp], axis=1),
        jnp.concatenate([ac_ref[0], mc_ref[0]], axis=1).astype(F32),
        jnp.concatenate([an, mn], axis=1)], axis=0).astype(BF16)
    x = jnp.concatenate([xp_ref[0], xc_ref[0], xn_ref[0]], axis=0)
    x1 = x + g1 * _dot(mix, wo_ref[...])
    f = _rmsnorm_mod(x1, nw_ref[...], sh2, sc2).astype(BF16)

    has_prev = (j > 1).astype(F32)
    has_next = ((j > 0) & (j < n_tiles - 1)).astype(F32)
    lo = HALO_F32 - 1
    hi = HALO_F32 + TILE
    u = _dot(f, wu_ref[...])
    slabs = []
    for t in range(2 * D_FF // LANES):
        cols = slice(t * LANES, (t + 1) * LANES)
        u_ref[t] = u[:, cols]
        u_ref[t, lo:lo + 1, :] = u_ref[t, lo:lo + 1, :] * has_prev
        u_ref[t, hi:hi + 1, :] = u_ref[t, hi:hi + 1, :] * has_next
        cw = cw_ref[:, cols]
        slabs.append(u_ref[t, pl.ds(lo, TILE), :] * cw[0:1]
                     + u_ref[t, pl.ds(lo + 1, TILE), :] * cw[1:2]
                     + u_ref[t, pl.ds(lo + 2, TILE), :] * cw[2:3] + cb_ref[:, cols])
    g = jnp.concatenate(slabs[:D_FF // LANES], axis=1)
    v = jnp.concatenate(slabs[D_FF // LANES:], axis=1)
    act = (g * (1.0 / (1.0 + jnp.exp(-g))) * v).astype(BF16)
    x2 = x1[HALO_F32:HALO_F32 + TILE] + g2 * _dot(act, wd_ref[...])
    if final:
        ms = jnp.mean(x2 * x2, axis=-1, keepdims=True)
        x2 = x2 * lax.rsqrt(ms + EPS) * fw_ref[...]
    return x2


N_TILE_REFS = 10


def _out_ffn_kernel(final, *refs):
    tile_refs, weights, (o_ref, u_ref) = refs[:N_TILE_REFS], refs[N_TILE_REFS:-2], refs[-2:]
    o_ref[0] = _ffn_tile(final, pl.program_id(1), pl.num_programs(1), *tile_refs, *weights, u_ref)


def _out_ffn_call(xall, oa, obcd, mod, w_out, norm_w, w_up, conv_w, conv_b, w_down, final_w, final):
    nb, rows, _ = xall.shape
    nt = rows // TILE
    h8 = TILE // HALO_F32
    h16 = TILE // HALO_BF16

    def trio(w, halo, per_tile):
        return [
            pl.BlockSpec((1, halo, w), lambda b, j: (b, jnp.maximum(j * per_tile - 1, 0), 0)),
            pl.BlockSpec((1, TILE, w), lambda b, j: (b, j, 0)),
            pl.BlockSpec((1, halo, w), lambda b, j: (
                b, jnp.minimum((j + 1) * per_tile, nt * per_tile - 1), 0)),
        ]

    tile_specs = [*trio(D_MODEL, HALO_F32, h8), *trio(256, HALO_BF16, h16),
                  *trio(768, HALO_BF16, h16), _mod_spec(nb)]
    weight_specs = [
        _resident((D_MODEL, D_MODEL)),
        _resident((1, D_MODEL)),
        _resident((D_MODEL, 2 * D_FF)),
        _resident((3, 2 * D_FF)),
        _resident((1, 2 * D_FF)),
        _resident((D_FF, D_MODEL)),
        _resident((1, D_MODEL)),
    ]
    if final:
        out_spec = pl.BlockSpec((1, TILE, D_MODEL), lambda b, j: (b, jnp.maximum(j - 1, 0), 0))
        out_shape = jax.ShapeDtypeStruct((nb, rows - CTX_LEN, D_MODEL), F32)
    else:
        out_spec = pl.BlockSpec((1, TILE, D_MODEL), lambda b, j: (b, j, 0))
        out_shape = jax.ShapeDtypeStruct((nb, rows, D_MODEL), F32)
    return pl.pallas_call(
        functools.partial(_out_ffn_kernel, final),
        grid=(nb, nt),
        in_specs=[*tile_specs, *weight_specs],
        out_specs=out_spec,
        out_shape=out_shape,
        scratch_shapes=[pltpu.VMEM((2 * D_FF // LANES, EXT, LANES), F32)],
        compiler_params=_params(2),
        name="out_ffn_final" if final else "out_ffn",
    )(xall, xall, xall, oa, oa, oa, obcd, obcd, obcd, mod, w_out, norm_w, w_up, conv_w, conv_b,
      w_down, final_w)


def _rope_tables(n_lat, dim):
    quarter = dim // 4
    t = np.arange(n_lat)
    pos = np.stack([t // GRID_W, t % GRID_W], axis=1).astype(np.float32)
    freqs = jnp.asarray(ROPE_THETA, F32) ** (-(jnp.arange(quarter, dtype=F32) / quarter))
    lane = np.arange(LANES) % dim
    axis = lane // (dim // 2)
    fidx = lane % quarter
    sign = np.where((lane % (dim // 2)) < quarter, -1.0, 1.0).astype(np.float32)
    ang = jnp.asarray(pos)[:, axis] * freqs[fidx][None, :]
    cos = jnp.concatenate([jnp.ones((CTX_LEN, LANES), F32), jnp.cos(ang)], axis=0)
    sin = jnp.concatenate([jnp.zeros((CTX_LEN, LANES), F32), jnp.sin(ang) * sign[None, :]], axis=0)
    return cos, sin


def _na_bias_table(rpb):
    rpt = TILE // GRID_W
    n_heads, n_dr, n_dc = rpb.shape
    cq = np.arange(GRID_W)[:, None]
    ck = np.arange(GRID_W)[None, :]
    dc = np.clip(ck - cq, -(NA_COLS - 1), NA_COLS - 1) + NA_COLS - 1
    col_start = np.clip(cq - NA_COLS // 2, 0, GRID_W - NA_COLS)
    col_ok = (ck >= col_start) & (ck < col_start + NA_COLS)
    pick = dc[None, :, :] == np.arange(n_dc)[:, None, None]
    e = jnp.sum(jnp.where(pick[None, None], rpb.astype(F32)[:, :, :, None, None], 0.0), axis=2)
    e = jnp.where(col_ok[None, None], e, NEG_INF)
    blocks = []
    for rq in range(rpt):
        lo = NA_ROWS - 1 - rpt - rq
        blocks.append(e[:, lo:lo + 3 * rpt].transpose(0, 1, 3, 2))
    local = jnp.stack(blocks, axis=3).reshape(n_heads, 3 * TILE, TILE)
    return jnp.concatenate([local, jnp.zeros((n_heads, CTX_LEN, TILE), F32)], axis=1)


def _window_masks(n_lat):
    nt = 1 + n_lat // TILE
    half = SWA_WINDOW
    rpt = TILE // GRID_W
    n_rows = n_lat // GRID_W
    rq = np.arange(TILE)[None, :]
    kb = np.arange(TILE + 2 * half + CTX_LEN)[:, None]
    kc = np.arange(3 * TILE + CTX_LEN)[:, None]
    masks_b, masks_c = [], []
    for j in _tile_variants(nt):
        kpos = (j - 1) * TILE - half + kb
        band = (np.abs(kb - half - rq) <= SWA_WINDOW) & (kpos >= 0) & (kpos < n_lat) & (j > 0)
        masks_b.append(band | (kb >= TILE + 2 * half))
        r = (j - 1) * rpt + rq // GRID_W
        kr = (j - 2) * rpt + kc // GRID_W
        rs = np.clip(r - NA_ROWS // 2, 0, n_rows - NA_ROWS)
        masks_c.append(((kr >= rs) & (kr < rs + NA_ROWS) & (j > 0)) | (kc >= 3 * TILE))
    to_add = lambda ms: jnp.where(jnp.asarray(np.stack(ms)), 0.0, NEG_INF).astype(F32)
    return to_add(masks_b), to_add(masks_c)


def _swa_head_order(w, axis, start):
    def blk(lo, hi):
        return lax.slice_in_dim(w, lo, hi, axis=axis)
    heads = [blk(start + h * HEAD_DIM, start + (h + 1) * HEAD_DIM) for h in (0, 2, 1, 3)]
    return jnp.concatenate(
        [blk(0, start), *heads, blk(start + SWA_HEADS * HEAD_DIM, w.shape[axis])], axis=axis)


def kernel(x, c, ctx, c_ctx, norm_mix_w, norm_ffn_w, w_mod, b_mod, w_in, w_out, diff_lambda_q1, diff_lambda_k1, diff_lambda_q2, diff_lambda_k2, diff_subln_w, swa_sink, na_rpb, sconv_w, ffn_w_up, ffn_conv_w, ffn_conv_b, ffn_w_down, final_norm_w):
    nb, n_lat, d = x.shape
    depth = w_in.shape[0]
    assert d == D_MODEL and ctx.shape[1] == CTX_LEN and n_lat % TILE == 0 and nb < 8
    assert n_lat // GRID_W >= NA_ROWS and n_lat % (max(2, BLOCKS_PER_STEP) * KV_BLOCK) == 0

    xall = jnp.concatenate([ctx, x], axis=1)
    cin = jnp.concatenate([c, c_ctx[None, :], jnp.zeros((8 - nb - 1, d), F32)], axis=0)
    mods = _mod_call(cin, w_mod, b_mod)

    tabs = (*_rope_tables(n_lat, DIFF_QK_DIM), *_rope_tables(n_lat, HEAD_DIM))
    mask_b, mask_c = _window_masks(n_lat)
    for l in range(depth):
        mod = mods[l].reshape(8, 1, 6 * D_MODEL)
        w_in_l = _swa_head_order(w_in[l], 1, 3 * GROUP_WIDTH).astype(BF16)
        w_out_l = _swa_head_order(w_out[l], 0, GROUP_WIDTH).astype(BF16)
        qa, ka, vat, bqt, bk, bvt, cqt, ck, cvt, hd = _in_proj_call(
            xall, mod, norm_mix_w[l][None, :], w_in_l, tabs)

        lambda_init = 0.8 - 0.6 * math.exp(-0.3 * l)
        lam_init = jnp.asarray([lambda_init, 1.0 - lambda_init], F32)
        lam_vecs = jnp.stack([diff_lambda_q1[l], diff_lambda_k1[l],
                              diff_lambda_q2[l], diff_lambda_k2[l]])
        subw = jnp.broadcast_to(diff_subln_w[l][:, None], (HEAD_DIM, QA_TILE))
        oa = _diff_attn_call(lam_init, lam_vecs, qa, ka, vat, subw)

        ctab = (_na_bias_table(na_rpb[l])[None] + mask_c[:, None]) * LOG2E
        obcd = _local_mix_call(swa_sink[l], bqt, bk, bvt, cqt, ck, cvt, mask_b, ctab,
                               hd, sconv_w[l])

        xall = _out_ffn_call(
            xall, oa, obcd, mod, w_out_l, norm_ffn_w[l][None, :],
            ffn_w_up[l].astype(BF16), ffn_conv_w[l], ffn_conv_b[l][None, :],
            ffn_w_down[l].astype(BF16), final_norm_w[None, :], final=(l == depth - 1))
    return xall
```

```python
import functools
import math

import numpy as np
import jax
import jax.numpy as jnp
from jax import lax
from jax.experimental import pallas as pl
from jax.experimental.pallas import tpu as pltpu

D_MODEL = 1024
GRID_W = 64
CTX_LEN = 256
HEAD_DIM = 64
GROUP_WIDTH = 256
DIFF_HEADS = 4
DIFF_QK_DIM = 32
SWA_HEADS = 4
SWA_WINDOW = 128
NA_HEADS = 4
NA_ROWS = 8
NA_COLS = 16
D_FF = 2816
IN_COLS = 2816
ROPE_THETA = 10000.0
EPS = 1e-6
NEG_INF = -1e30
LOG2E = math.log2(math.e)

TILE = 256
HALO_F32 = 8
HALO_BF16 = 16
LANES = 128
VMEM_LIMIT = 56 * 1024 * 1024

F32 = jnp.float32
BF16 = jnp.bfloat16


def _params(n_grid):
    return pltpu.CompilerParams(
        dimension_semantics=("arbitrary",) * n_grid, vmem_limit_bytes=VMEM_LIMIT)


def _resident(shape):
    nd = len(shape)
    return pl.BlockSpec(shape, lambda *_: (0,) * nd, pipeline_mode=pl.Buffered(1))


def _mod_spec(n_batch):
    return pl.BlockSpec((1, 1, 6 * D_MODEL), lambda b, j: (jnp.where(j == 0, n_batch, b), 0, 0))


def _rmsnorm_mod(x, w, shift, scale):
    ms = jnp.mean(x * x, axis=-1, keepdims=True)
    return (x * lax.rsqrt(ms + EPS) * w) * (1.0 + scale) + shift


def _dot(a, b):
    return jnp.dot(a, b, preferred_element_type=F32)


MOD_BLOCK = 1536


def _mod_kernel(c_ref, w_ref, b_ref, o_ref):
    c = c_ref[...]
    a = c * (1.0 / (1.0 + jnp.exp(-c)))
    o_ref[0] = _dot(a.astype(BF16), w_ref[0].astype(BF16)) + b_ref[0]


def _mod_call(cin, w_mod, b_mod):
    depth = w_mod.shape[0]
    n = 6 * D_MODEL
    return pl.pallas_call(
        _mod_kernel,
        grid=(depth, n // MOD_BLOCK),
        in_specs=[
            pl.BlockSpec((8, D_MODEL), lambda l, k: (0, 0)),
            pl.BlockSpec((1, D_MODEL, MOD_BLOCK), lambda l, k: (l, 0, k)),
            pl.BlockSpec((1, 1, MOD_BLOCK), lambda l, k: (l, 0, k)),
        ],
        out_specs=pl.BlockSpec((1, 8, MOD_BLOCK), lambda l, k: (l, 0, k)),
        out_shape=jax.ShapeDtypeStruct((depth, 8, n), F32),
        compiler_params=_params(2),
        name="mod_proj",
    )(cin, w_mod, b_mod.reshape(depth, 1, n))


def _rope(x, cos, sin, half):
    lane = lax.broadcasted_iota(jnp.int32, x.shape, 1)
    first = (lane % (2 * half)) < half
    partner = jnp.where(first, pltpu.roll(x, LANES - half, 1), pltpu.roll(x, half, 1))
    return x * cos + partner * sin


def _in_proj_kernel(first, x_ref, ctx_ref, mod_ref, nw_ref, w_ref,
                    cosa_ref, sina_ref, cosb_ref, sinb_ref,
                    qa_ref, ka_ref, vat_ref, bqt_ref, bk_ref, bvt_ref, cqt_ref, ck_ref, cvt_ref,
                    hd_ref):
    mod = mod_ref[0]
    shift = mod[:, 0:D_MODEL]
    scale = mod[:, D_MODEL:2 * D_MODEL]
    x = x_ref[0]
    if first:
        x = jnp.where(pl.program_id(1) == 0, ctx_ref[0], x)
    z = _rmsnorm_mod(x, nw_ref[...], shift, scale)
    h = _dot(z.astype(BF16), w_ref[...])

    def sec(k, n=1):
        return h[:, k * LANES:(k + n) * LANES]

    cosa, sina = cosa_ref[...], sina_ref[...]
    cosb, sinb = cosb_ref[...], sinb_ref[...]
    qa_scale = DIFF_QK_DIM ** -0.5 * LOG2E
    q_scale = HEAD_DIM ** -0.5 * LOG2E
    ha = DIFF_QK_DIM // 4
    hb = HEAD_DIM // 4
    for t in range(2):
        qa_ref[0, :, t * LANES:(t + 1) * LANES] = (
            _rope(sec(t), cosa, sina, ha) * qa_scale).astype(BF16)
        ka_ref[0, :, t * LANES:(t + 1) * LANES] = _rope(sec(2 + t), cosa, sina, ha).astype(BF16)
    vat_ref[0, 0] = sec(4, 2).T.astype(BF16)
    for t in range(2):
        bqt_ref[0, 0, t * LANES:(t + 1) * LANES, :] = (
            _rope(sec(6 + t), cosb, sinb, hb) * q_scale).T.astype(BF16)
    bk_ref[0] = _rope(sec(8), cosb, sinb, hb).astype(BF16)
    bvt_ref[0, 0] = sec(9).T.astype(BF16)
    cqt_ref[0, 0] = (sec(10, 2) * q_scale).T.astype(BF16)
    ck_ref[0] = sec(12, 2).astype(BF16)
    cvt_ref[0, 0] = sec(14, 2).T.astype(BF16)
    hd_ref[0] = sec(16, 6).astype(BF16)


def _latent_tile(first):
    return (lambda j: jnp.maximum(j - 1, 0)) if first else (lambda j: j)


def _in_proj_call(xrows, ctx, mod, norm_w, w_in, tabs, first):
    nb = xrows.shape[0]
    rows = xrows.shape[1] + (CTX_LEN if first else 0)
    nt = rows // TILE
    xtile = _latent_tile(first)
    tile = lambda w: pl.BlockSpec((1, TILE, w), lambda b, j: (b, j, 0))
    ttile = lambda w: pl.BlockSpec((1, 1, w, TILE), lambda b, j: (b, j, 0, 0))
    tab = pl.BlockSpec((TILE, LANES), lambda b, j: (j, 0))
    sds = lambda w: jax.ShapeDtypeStruct((nb, rows, w), BF16)
    tsds = lambda w: jax.ShapeDtypeStruct((nb, nt, w, TILE), BF16)
    return pl.pallas_call(
        functools.partial(_in_proj_kernel, first),
        grid=(nb, nt),
        in_specs=[
            pl.BlockSpec((1, TILE, D_MODEL), lambda b, j: (b, xtile(j), 0)),
            pl.BlockSpec((1, CTX_LEN, D_MODEL), lambda b, j: (b, 0, 0)),
            _mod_spec(nb),
            _resident((1, D_MODEL)),
            _resident((D_MODEL, IN_COLS)),
            tab, tab, tab, tab,
        ],
        out_specs=[
            tile(256), tile(256), ttile(256),
            ttile(256), tile(128), ttile(128),
            ttile(256), tile(256), ttile(256),
            tile(768),
        ],
        out_shape=[
            sds(256), sds(256), tsds(256),
            tsds(256), sds(128), tsds(128),
            tsds(256), sds(256), tsds(256),
            sds(768),
        ],
        compiler_params=_params(2),
        name="in_proj",
    )(xrows, ctx, mod, norm_w, w_in, *tabs)


QA_TILE = 256
KV_BLOCK = 512
BLOCKS_PER_STEP = 16
LOOKAHEAD = 2
EXP_CAP = 100.0
PV_ROWS = HEAD_DIM + HALO_BF16


def _diff_attn_kernel(li_ref, lv_ref, q_ref, k_ref, vt_ref, sw_ref, o_ref,
                      qst_ref, sa_ref, sb_ref, mca_ref, mcb_ref, m_ref, acc_ref, gap_ref):
    i = pl.program_id(1)
    s_refs = (sa_ref, sb_ref)
    mc_refs = (mca_ref, mcb_ref)
    q = q_ref[0].astype(F32)
    lane = lax.broadcasted_iota(jnp.int32, q.shape, 1) // DIFF_QK_DIM
    for h in range(DIFF_HEADS):
        qs = jnp.concatenate([jnp.where(lane == 2 * h + c, q, 0.0) for c in range(2)], axis=0)
        qst_ref[h] = qs.T.astype(BF16)
    n_cols = 2 * QA_TILE
    n_blk = (k_ref.shape[1] - CTX_LEN) // KV_BLOCK
    tiles_per_blk = KV_BLOCK // TILE

    def weighted(vt, h, e):
        num = _dot(vt[h * HEAD_DIM:(h + 1) * HEAD_DIM], e.astype(BF16))
        den = jnp.broadcast_to(jnp.sum(e, axis=0, keepdims=True), (HALO_BF16, e.shape[1]))
        return jnp.concatenate([num, den], axis=0)

    def ctx_scores(h):
        return _dot(k_ref[0, 0:CTX_LEN, :], qst_ref[h])

    def ctx_init(h, s):
        m0 = jnp.max(s, axis=0, keepdims=True)
        acc_ref[h] = weighted(vt_ref[0, 0], h, jnp.exp2(s - m0))
        m_ref[h] = m0

    def score_head(blk, slot, h):
        row0 = pl.multiple_of(CTX_LEN + (blk - 1) * KV_BLOCK, TILE)
        s = _dot(k_ref[0, pl.ds(row0, KV_BLOCK), :], qst_ref[h])
        s_refs[slot][h] = s
        mc_refs[slot][h] = jnp.max(s, axis=0, keepdims=True)

    def block_step(score_blk, acc_blk):
        if acc_blk is not None:
            ablk, aslot = acc_blk
            t0 = 1 + (ablk - 1) * tiles_per_blk
            vt = jnp.concatenate([vt_ref[0, t0 + u] for u in range(tiles_per_blk)], axis=1)
        for h in range(DIFF_HEADS):
            if score_blk is not None:
                score_head(*score_blk, h)
            if acc_blk is not None:
                m_old = m_ref[h]
                m_new = jnp.maximum(m_old, mc_refs[aslot][h])
                p = jnp.exp2(s_refs[aslot][h] - m_new)
                acc_ref[h] = acc_ref[h] * jnp.exp2(m_old - m_new) + weighted(vt, h, p)
                m_ref[h] = m_new

    def two_pass_latent():
        scores0 = [ctx_scores(h) for h in range(DIFF_HEADS)]
        for h in range(DIFF_HEADS):
            score_head(1, 0, h)
            ctx_init(h, scores0[h])

        def step(it, carry):
            b = 2 * it + 1
            block_step((b + 1, 1), (b, 0))
            block_step((b + 2, 0), (b + 1, 1))
            return carry

        lax.fori_loop(0, n_blk // 2 - 1, step, 0)
        block_step((n_blk, 1), (n_blk - 1, 0))
        block_step(None, (n_blk, 1))

    def one_pass_latent():
        scores0 = [ctx_scores(h) for h in range(DIFF_HEADS)]
        for h in range(DIFF_HEADS):
            ctx_init(h, scores0[h])
            gap_ref[h] = jnp.zeros(gap_ref.shape[1:], F32)

        def scores(unit):
            blk, h, c = unit
            row0 = pl.multiple_of(CTX_LEN + (blk - 1) * KV_BLOCK, TILE)
            return _dot(k_ref[0, pl.ds(row0, KV_BLOCK), :],
                        qst_ref[h, :, c * QA_TILE:(c + 1) * QA_TILE])

        def consume(unit, s, vt):
            _, h, c = unit
            cols = slice(c * QA_TILE, (c + 1) * QA_TILE)
            m_old = m_ref[h, :, cols]
            m_cur = jnp.max(s, axis=0, keepdims=True)
            pv = weighted(vt, h, jnp.exp2(s - m_old))
            m_new = jnp.maximum(m_old, m_cur)
            acc_ref[h, :, cols] = (acc_ref[h, :, cols] + pv) * jnp.exp2(m_old - m_new)
            m_ref[h, :, cols] = m_new
            gap_ref[h, :, cols] = jnp.maximum(gap_ref[h, :, cols], m_cur - m_old)

        def step(it, carry):
            b0 = it * BLOCKS_PER_STEP + 1
            units = [(b0 + u, h, c) for u in range(BLOCKS_PER_STEP)
                     for h in range(DIFF_HEADS) for c in range(2)]
            pending = [scores(u) for u in units[:LOOKAHEAD]]
            for n, unit in enumerate(units):
                if n + LOOKAHEAD < len(units):
                    pending.append(scores(units[n + LOOKAHEAD]))
                if unit[1] == 0 and unit[2] == 0:
                    t0 = 1 + (unit[0] - 1) * tiles_per_blk
                    vt = jnp.concatenate(
                        [vt_ref[0, t0 + u] for u in range(tiles_per_blk)], axis=1)
                consume(unit, pending.pop(0), vt)
            return carry

        lax.fori_loop(0, n_blk // BLOCKS_PER_STEP, step, 0)

    def finalize():
        lv = lv_ref[...]
        lam = (jnp.exp(jnp.sum(lv[0:1] * lv[1:2], axis=1, keepdims=True))
               - jnp.exp(jnp.sum(lv[2:3] * lv[3:4], axis=1, keepdims=True)) + li_ref[0])
        out_scale = li_ref[1]
        outs = []
        for h in range(DIFF_HEADS):
            a = acc_ref[h]
            o1 = a[0:HEAD_DIM, 0:QA_TILE] / a[HEAD_DIM:HEAD_DIM + 1, 0:QA_TILE]
            o2 = a[0:HEAD_DIM, QA_TILE:n_cols] / a[HEAD_DIM:HEAD_DIM + 1, QA_TILE:n_cols]
            o = o1 - lam * o2
            ms = jnp.mean(o * o, axis=0, keepdims=True)
            outs.append(o * lax.rsqrt(ms + EPS) * sw_ref[...] * out_scale)
        o_ref[0] = jnp.concatenate(outs, axis=0).T.astype(BF16)

    @pl.when(i < CTX_LEN // QA_TILE)
    def _():
        for h in range(DIFF_HEADS):
            ctx_init(h, ctx_scores(h))
        finalize()

    @pl.when(i >= CTX_LEN // QA_TILE)
    def _():
        one_pass_latent()
        finalize()

        @pl.when(jnp.max(gap_ref[...]) > EXP_CAP)
        def _():
            two_pass_latent()
            finalize()


def _diff_attn_call(lam_init, lam_vecs, qa, ka, vat, subw):
    nb, rows, _ = qa.shape
    nt = rows // TILE
    return pl.pallas_call(
        _diff_attn_kernel,
        grid=(nb, rows // QA_TILE),
        in_specs=[
            pl.BlockSpec(memory_space=pltpu.SMEM),
            _resident((4, DIFF_QK_DIM)),
            pl.BlockSpec((1, QA_TILE, 256), lambda b, i: (b, i, 0)),
            pl.BlockSpec((1, rows, 256), lambda b, i: (b, 0, 0)),
            pl.BlockSpec((1, nt, 256, TILE), lambda b, i: (b, 0, 0, 0)),
            _resident((HEAD_DIM, QA_TILE)),
        ],
        out_specs=pl.BlockSpec((1, QA_TILE, 256), lambda b, i: (b, i, 0)),
        out_shape=jax.ShapeDtypeStruct((nb, rows, 256), BF16),
        scratch_shapes=[
            pltpu.VMEM((DIFF_HEADS, 256, 2 * QA_TILE), BF16),
            pltpu.VMEM((DIFF_HEADS, KV_BLOCK, 2 * QA_TILE), F32),
            pltpu.VMEM((DIFF_HEADS, KV_BLOCK, 2 * QA_TILE), F32),
            pltpu.VMEM((DIFF_HEADS, 1, 2 * QA_TILE), F32),
            pltpu.VMEM((DIFF_HEADS, 1, 2 * QA_TILE), F32),
            pltpu.VMEM((DIFF_HEADS, 1, 2 * QA_TILE), F32),
            pltpu.VMEM((DIFF_HEADS, PV_ROWS, 2 * QA_TILE), F32),
            pltpu.VMEM((DIFF_HEADS, 1, 2 * QA_TILE), F32),
        ],
        compiler_params=_params(2),
        name="diff_attn",
    )(lam_init, lam_vecs, qa, ka, vat, subw)


MIX_LOOKAHEAD = 2


def _head_rows(qt, k):
    row = lax.broadcasted_iota(jnp.int32, qt.shape, 0) // HEAD_DIM
    return jnp.where(row == k, qt, jnp.zeros_like(qt))


def _value_rows(vt_parts, rows):
    v = jnp.concatenate([part[rows] for part in vt_parts], axis=1)
    return jnp.concatenate([v, jnp.ones((HALO_BF16, v.shape[1]), BF16)], axis=0)


def _local_mix_kernel(sink_ref, bqt_ref, bkp_ref, bkc_ref, bkn_ref, bkx_ref,
                      bvtp_ref, bvtc_ref, bvtn_ref, bvtx_ref,
                      cqt_ref, ckp_ref, ckc_ref, ckn_ref, ckx_ref,
                      cvtp_ref, cvtc_ref, cvtn_ref, cvtx_ref, bmask_ref, ctab_ref,
                      dp_ref, dc_ref, dn_ref, cw_ref, o_ref, s_ref):
    j = pl.program_id(1)
    n_tiles = pl.num_programs(1)
    half = SWA_WINDOW

    k_b = jnp.concatenate([bkp_ref[0], bkc_ref[0], bkn_ref[0], bkx_ref[0]], axis=0)
    vt_b = [bvtp_ref[0, 0][:, half:], bvtc_ref[0, 0], bvtn_ref[0, 0][:, :half], bvtx_ref[0, 0]]
    vt_c = [cvtp_ref[0, 0], cvtc_ref[0, 0], cvtn_ref[0, 0], cvtx_ref[0, 0]]

    units = [(mixer, p, k) for mixer in "BC" for p in range(2) for k in range(2)]

    def scores(unit):
        mixer, p, k = unit
        rows = slice(p * LANES, (p + 1) * LANES)
        if mixer == "B":
            return _dot(k_b, _head_rows(bqt_ref[0, 0, rows, :], k))
        k_c = jnp.concatenate([ckp_ref[0, :, rows], ckc_ref[0, :, rows],
                               ckn_ref[0, :, rows], ckx_ref[0, :, rows]], axis=0)
        return _dot(k_c, _head_rows(cqt_ref[0, 0, rows, :], k))

    def attend(unit, s):
        mixer, p, k = unit
        feat = slice(k * HEAD_DIM, (k + 1) * HEAD_DIM)
        if mixer == "B":
            s = s + bmask_ref[0]
            sink = sink_ref[2 * k + p] * LOG2E
            m = jnp.maximum(jnp.max(s, axis=0, keepdims=True), sink)
            pv = _dot(_value_rows(vt_b, feat), jnp.exp2(s - m).astype(BF16))
            return pv[0:HEAD_DIM] / (pv[HEAD_DIM:HEAD_DIM + 1] + jnp.exp2(sink - m))
        s = s + ctab_ref[0, 2 * p + k]
        m = jnp.max(s, axis=0, keepdims=True)
        rows = slice(p * LANES + k * HEAD_DIM, p * LANES + (k + 1) * HEAD_DIM)
        pv = _dot(_value_rows(vt_c, rows), jnp.exp2(s - m).astype(BF16))
        return pv[0:HEAD_DIM] / pv[HEAD_DIM:HEAD_DIM + 1]

    def write_out(outs):
        o_ref[0, :, 0:GROUP_WIDTH] = jnp.concatenate(outs[0:4], axis=0).T.astype(BF16)
        o_ref[0, :, GROUP_WIDTH:2 * GROUP_WIDTH] = jnp.concatenate(outs[4:8], axis=0).T.astype(BF16)

    def two_pass():
        def park(n):
            s = scores(units[n])
            s_ref[n % (MIX_LOOKAHEAD + 1), 0:s.shape[0], :] = s

        for n in range(MIX_LOOKAHEAD):
            park(n)
        outs = []
        for n, unit in enumerate(units):
            if n + MIX_LOOKAHEAD < len(units):
                park(n + MIX_LOOKAHEAD)
            n_keys = (TILE + 2 * SWA_WINDOW if unit[0] == "B" else 3 * TILE) + CTX_LEN
            outs.append(attend(unit, s_ref[n % (MIX_LOOKAHEAD + 1), 0:n_keys, :]))
        write_out(outs)

    def unit_operands(unit):
        mixer, p, k = unit
        rows = slice(p * LANES, (p + 1) * LANES)
        if mixer == "B":
            n_loc = TILE + 2 * half
            return (_head_rows(bqt_ref[0, 0, rows, :], k), k_b[0:n_loc], k_b[n_loc:],
                    bmask_ref[0, 0:n_loc, :], vt_b, slice(k * HEAD_DIM, (k + 1) * HEAD_DIM))
        k_loc = jnp.concatenate([ckp_ref[0, :, rows], ckc_ref[0, :, rows], ckn_ref[0, :, rows]],
                                axis=0)
        feat = slice(p * LANES + k * HEAD_DIM, p * LANES + (k + 1) * HEAD_DIM)
        return (_head_rows(cqt_ref[0, 0, rows, :], k), k_loc, ckx_ref[0, :, rows],
                ctab_ref[0, 2 * p + k, 0:3 * TILE, :], vt_c, feat)

    def one_pass_scores(unit):
        qt, k_loc, k_ctx, _, _, _ = unit_operands(unit)
        return _dot(k_ctx, qt), _dot(k_loc, qt)

    def one_pass_attend(unit, s_ctx, s_loc):
        mixer, p, k = unit
        _, _, _, table, vt, feat = unit_operands(unit)
        m = jnp.max(s_ctx, axis=0, keepdims=True)
        if mixer == "B":
            sink = sink_ref[2 * k + p] * LOG2E
            m = jnp.maximum(m, sink)
        s_loc = s_loc + table
        excess = jnp.max(s_loc, axis=0, keepdims=True) - m
        vx = _value_rows(vt, feat)
        n_loc = s_loc.shape[0]
        pv = (_dot(vx[:, 0:n_loc], jnp.exp2(s_loc - m).astype(BF16))
              + _dot(vx[:, n_loc:], jnp.exp2(s_ctx - m).astype(BF16)))
        l = pv[HEAD_DIM:HEAD_DIM + 1]
        if mixer == "B":
            l = l + jnp.exp2(sink - m)
        return pv[0:HEAD_DIM] / l, excess

    pending = [one_pass_scores(units[0])]
    outs, worst = [], None
    for n, unit in enumerate(units):
        if n + 1 < len(units):
            pending.append(one_pass_scores(units[n + 1]))
        o, excess = one_pass_attend(unit, *pending.pop(0))
        outs.append(o)
        worst = excess if worst is None else jnp.maximum(worst, excess)
    write_out(outs)

    @pl.when(jnp.max(worst) > EXP_CAP)
    def _():
        two_pass()

    has_prev = (j > 1).astype(F32)
    has_next = ((j > 0) & (j < n_tiles - 1)).astype(F32)
    dc = dc_ref[0].astype(F32)
    dp = dp_ref[0].astype(F32)[HALO_BF16 - HALO_F32:]
    dn = dn_ref[0].astype(F32)[:HALO_F32]
    gw = GROUP_WIDTH
    z = jnp.concatenate([dp[:, gw:2 * gw] * dp[:, 2 * gw:] * has_prev,
                         dc[:, gw:2 * gw] * dc[:, 2 * gw:],
                         dn[:, gw:2 * gw] * dn[:, 2 * gw:] * has_next], axis=0)
    cw = cw_ref[...]
    c0 = HALO_F32
    conv = (z[c0 - 1:c0 - 1 + TILE] * cw[0:1] + z[c0:c0 + TILE] * cw[1:2]
            + z[c0 + 1:c0 + 1 + TILE] * cw[2:3])
    o_ref[0, :, 4 * LANES:6 * LANES] = (dc[:, 0:gw] * conv).astype(BF16)


def _tile_variants(nt):
    return (0, 1, 2, nt - 1)


def _local_mix_call(sink, bqt, bk, bvt, cqt, ck, cvt, bmask, ctab, hd, conv_w):
    nb, rows, _ = bk.shape
    nt = rows // TILE
    variant = lambda j: jnp.where(j == 0, 0, jnp.where(j == 1, 1, jnp.where(j == nt - 1, 3, 2)))
    hb = TILE // SWA_WINDOW
    h16 = TILE // HALO_BF16
    prev_j = lambda j: jnp.maximum(j - 1, 0)
    next_j = lambda j: jnp.minimum(j + 1, nt - 1)
    cur = lambda w: pl.BlockSpec((1, TILE, w), lambda b, j: (b, j, 0))
    prev = lambda w: pl.BlockSpec((1, TILE, w), lambda b, j: (b, prev_j(j), 0))
    nxt = lambda w: pl.BlockSpec((1, TILE, w), lambda b, j: (b, next_j(j), 0))
    ctx = lambda w: pl.BlockSpec((1, TILE, w), lambda b, j: (b, 0, 0))
    tcur = lambda w: pl.BlockSpec((1, 1, w, TILE), lambda b, j: (b, j, 0, 0))
    tprev = lambda w: pl.BlockSpec((1, 1, w, TILE), lambda b, j: (b, prev_j(j), 0, 0))
    tnxt = lambda w: pl.BlockSpec((1, 1, w, TILE), lambda b, j: (b, next_j(j), 0, 0))
    tctx = lambda w: pl.BlockSpec((1, 1, w, TILE), lambda b, j: (b, 0, 0, 0))
    return pl.pallas_call(
        _local_mix_kernel,
        grid=(nb, nt),
        in_specs=[
            pl.BlockSpec(memory_space=pltpu.SMEM),
            tcur(256),
            pl.BlockSpec((1, SWA_WINDOW, 128), lambda b, j: (b, jnp.maximum(j * hb - 1, 0), 0)),
            cur(128),
            pl.BlockSpec((1, SWA_WINDOW, 128), lambda b, j: (b, jnp.minimum((j + 1) * hb, nt * hb - 1), 0)),
            ctx(128),
            tprev(128), tcur(128), tnxt(128), tctx(128),
            tcur(256),
            prev(256), cur(256), nxt(256), ctx(256),
            tprev(256), tcur(256), tnxt(256), tctx(256),
            pl.BlockSpec((1, TILE + 2 * SWA_WINDOW + CTX_LEN, TILE), lambda b, j: (variant(j), 0, 0)),
            pl.BlockSpec((1, NA_HEADS, 3 * TILE + CTX_LEN, TILE),
                         lambda b, j: (variant(j), 0, 0, 0)),
            pl.BlockSpec((1, HALO_BF16, 768), lambda b, j: (b, jnp.maximum(j * h16 - 1, 0), 0)),
            cur(768),
            pl.BlockSpec((1, HALO_BF16, 768), lambda b, j: (b, jnp.minimum((j + 1) * h16, nt * h16 - 1), 0)),
            _resident((3, GROUP_WIDTH)),
        ],
        out_specs=pl.BlockSpec((1, TILE, 768), lambda b, j: (b, j, 0)),
        out_shape=jax.ShapeDtypeStruct((nb, rows, 768), BF16),
        scratch_shapes=[pltpu.VMEM((MIX_LOOKAHEAD + 1, 3 * TILE + CTX_LEN, TILE), F32)],
        compiler_params=_params(2),
        name="local_mix",
    )(sink, bqt, bk, bk, bk, bk, bvt, bvt, bvt, bvt, cqt, ck, ck, ck, ck, cvt, cvt, cvt, cvt,
      bmask, ctab, hd, hd, hd, conv_w)


EXT = TILE + 2 * HALO_F32


def _halo_f32(prev_ref, next_ref):
    p = prev_ref[0].astype(F32)
    n = next_ref[0].astype(F32)
    return p[p.shape[0] - HALO_F32:], n[:HALO_F32]


def _ffn_tile(final, first, j, n_tiles, xp_ref, xc_ref, xn_ref, ctx_ref, ap_ref, ac_ref, an_ref,
              mp_ref, mc_ref, mn_ref, mod_ref, wo_ref, nw_ref, wu_ref, cw_ref, cb_ref, wd_ref,
              fw_ref, u_ref):
    mod = mod_ref[0]
    g1 = mod[:, 2 * D_MODEL:3 * D_MODEL]
    sh2 = mod[:, 3 * D_MODEL:4 * D_MODEL]
    sc2 = mod[:, 4 * D_MODEL:5 * D_MODEL]
    g2 = mod[:, 5 * D_MODEL:6 * D_MODEL]

    ap, an = _halo_f32(ap_ref, an_ref)
    mp, mn = _halo_f32(mp_ref, mn_ref)
    mix = jnp.concatenate([
        jnp.concatenate([ap, mp], axis=1),
        jnp.concatenate([ac_ref[0], mc_ref[0]], axis=1).astype(F32),
        jnp.concatenate([an, mn], axis=1)], axis=0).astype(BF16)
    xc = xc_ref[0]
    if first:
        xc = jnp.where(j == 0, ctx_ref[0], xc)
    x = jnp.concatenate([xp_ref[0], xc, xn_ref[0]], axis=0)
    x1 = x + g1 * _dot(mix, wo_ref[...])
    f = _rmsnorm_mod(x1, nw_ref[...], sh2, sc2).astype(BF16)

    has_prev = (j > 1).astype(F32)
    has_next = ((j > 0) & (j < n_tiles - 1)).astype(F32)
    lo = HALO_F32 - 1
    hi = HALO_F32 + TILE
    u = _dot(f, wu_ref[...])
    slabs = []
    for t in range(2 * D_FF // LANES):
        cols = slice(t * LANES, (t + 1) * LANES)
        u_ref[t] = u[:, cols]
        u_ref[t, lo:lo + 1, :] = u_ref[t, lo:lo + 1, :] * has_prev
        u_ref[t, hi:hi + 1, :] = u_ref[t, hi:hi + 1, :] * has_next
        cw = cw_ref[:, cols]
        slabs.append(u_ref[t, pl.ds(lo, TILE), :] * cw[0:1]
                     + u_ref[t, pl.ds(lo + 1, TILE), :] * cw[1:2]
                     + u_ref[t, pl.ds(lo + 2, TILE), :] * cw[2:3] + cb_ref[:, cols])
    g = jnp.concatenate(slabs[:D_FF // LANES], axis=1)
    v = jnp.concatenate(slabs[D_FF // LANES:], axis=1)
    act = (g * (1.0 / (1.0 + jnp.exp(-g))) * v).astype(BF16)
    x2 = x1[HALO_F32:HALO_F32 + TILE] + g2 * _dot(act, wd_ref[...])
    if final:
        ms = jnp.mean(x2 * x2, axis=-1, keepdims=True)
        x2 = x2 * lax.rsqrt(ms + EPS) * fw_ref[...]
    return x2


N_TILE_REFS = 11


def _out_ffn_kernel(final, first, *refs):
    tile_refs, weights, (o_ref, u_ref) = refs[:N_TILE_REFS], refs[N_TILE_REFS:-2], refs[-2:]
    o_ref[0] = _ffn_tile(final, first, pl.program_id(1), pl.num_programs(1),
                         *tile_refs, *weights, u_ref)


def _out_ffn_call(xrows, ctx, oa, obcd, mod, w_out, norm_w, w_up, conv_w, conv_b, w_down, final_w,
                  final, first):
    nb, rows, _ = oa.shape
    nt = rows // TILE
    h8 = TILE // HALO_F32
    h16 = TILE // HALO_BF16

    def trio(w, halo, per_tile, tile_of, n_arr_tiles):
        return [
            pl.BlockSpec((1, halo, w),
                         lambda b, j: (b, jnp.maximum(tile_of(j) * per_tile - 1, 0), 0)),
            pl.BlockSpec((1, TILE, w), lambda b, j: (b, tile_of(j), 0)),
            pl.BlockSpec((1, halo, w), lambda b, j: (
                b, jnp.minimum((tile_of(j) + 1) * per_tile, n_arr_tiles * per_tile - 1), 0)),
        ]

    same = lambda j: j
    tile_specs = [*trio(D_MODEL, HALO_F32, h8, _latent_tile(first), xrows.shape[1] // TILE),
                  pl.BlockSpec((1, CTX_LEN, D_MODEL), lambda b, j: (b, 0, 0)),
                  *trio(256, HALO_BF16, h16, same, nt), *trio(768, HALO_BF16, h16, same, nt),
                  _mod_spec(nb)]
    weight_specs = [
        _resident((D_MODEL, D_MODEL)),
        _resident((1, D_MODEL)),
        _resident((D_MODEL, 2 * D_FF)),
        _resident((3, 2 * D_FF)),
        _resident((1, 2 * D_FF)),
        _resident((D_FF, D_MODEL)),
        _resident((1, D_MODEL)),
    ]
    if final:
        out_spec = pl.BlockSpec((1, TILE, D_MODEL), lambda b, j: (b, jnp.maximum(j - 1, 0), 0))
        out_shape = jax.ShapeDtypeStruct((nb, rows - CTX_LEN, D_MODEL), F32)
    else:
        out_spec = pl.BlockSpec((1, TILE, D_MODEL), lambda b, j: (b, j, 0))
        out_shape = jax.ShapeDtypeStruct((nb, rows, D_MODEL), F32)
    return pl.pallas_call(
        functools.partial(_out_ffn_kernel, final, first),
        grid=(nb, nt),
        in_specs=[*tile_specs, *weight_specs],
        out_specs=out_spec,
        out_shape=out_shape,
        scratch_shapes=[pltpu.VMEM((2 * D_FF // LANES, EXT, LANES), F32)],
        compiler_params=_params(2),
        name="out_ffn_final" if final else "out_ffn",
    )(xrows, xrows, xrows, ctx, oa, oa, oa, obcd, obcd, obcd, mod, w_out, norm_w, w_up, conv_w,
      conv_b, w_down, final_w)


def _rope_tables(n_lat, dim):
    quarter = dim // 4
    t = np.arange(n_lat)
    pos = np.stack([t // GRID_W, t % GRID_W], axis=1).astype(np.float32)
    freqs = jnp.asarray(ROPE_THETA, F32) ** (-(jnp.arange(quarter, dtype=F32) / quarter))
    lane = np.arange(LANES) % dim
    axis = lane // (dim // 2)
    fidx = lane % quarter
    sign = np.where((lane % (dim // 2)) < quarter, -1.0, 1.0).astype(np.float32)
    ang = jnp.asarray(pos)[:, axis] * freqs[fidx][None, :]
    cos = jnp.concatenate([jnp.ones((CTX_LEN, LANES), F32), jnp.cos(ang)], axis=0)
    sin = jnp.concatenate([jnp.zeros((CTX_LEN, LANES), F32), jnp.sin(ang) * sign[None, :]], axis=0)
    return cos, sin


def _na_bias_table(rpb):
    rpt = TILE // GRID_W
    n_heads, n_dr, n_dc = rpb.shape
    cq = np.arange(GRID_W)[:, None]
    ck = np.arange(GRID_W)[None, :]
    dc = np.clip(ck - cq, -(NA_COLS - 1), NA_COLS - 1) + NA_COLS - 1
    col_start = np.clip(cq - NA_COLS // 2, 0, GRID_W - NA_COLS)
    col_ok = (ck >= col_start) & (ck < col_start + NA_COLS)
    pick = dc[None, :, :] == np.arange(n_dc)[:, None, None]
    e = jnp.sum(jnp.where(pick[None, None], rpb.astype(F32)[:, :, :, None, None], 0.0), axis=2)
    e = jnp.where(col_ok[None, None], e, NEG_INF)
    blocks = []
    for rq in range(rpt):
        lo = NA_ROWS - 1 - rpt - rq
        blocks.append(e[:, lo:lo + 3 * rpt].transpose(0, 1, 3, 2))
    local = jnp.stack(blocks, axis=3).reshape(n_heads, 3 * TILE, TILE)
    return jnp.concatenate([local, jnp.zeros((n_heads, CTX_LEN, TILE), F32)], axis=1)


def _window_masks(n_lat):
    nt = 1 + n_lat // TILE
    half = SWA_WINDOW
    rpt = TILE // GRID_W
    n_rows = n_lat // GRID_W
    rq = np.arange(TILE)[None, :]
    kb = np.arange(TILE + 2 * half + CTX_LEN)[:, None]
    kc = np.arange(3 * TILE + CTX_LEN)[:, None]
    masks_b, masks_c = [], []
    for j in _tile_variants(nt):
        kpos = (j - 1) * TILE - half + kb
        band = (np.abs(kb - half - rq) <= SWA_WINDOW) & (kpos >= 0) & (kpos < n_lat) & (j > 0)
        masks_b.append(band | (kb >= TILE + 2 * half))
        r = (j - 1) * rpt + rq // GRID_W
        kr = (j - 2) * rpt + kc // GRID_W
        rs = np.clip(r - NA_ROWS // 2, 0, n_rows - NA_ROWS)
        masks_c.append(((kr >= rs) & (kr < rs + NA_ROWS) & (j > 0)) | (kc >= 3 * TILE))
    to_add = lambda ms: jnp.where(jnp.asarray(np.stack(ms)), 0.0, NEG_INF).astype(F32)
    return to_add(masks_b), to_add(masks_c)


def _swa_head_order(w, axis, start):
    def blk(lo, hi):
        return lax.slice_in_dim(w, lo, hi, axis=axis)
    heads = [blk(start + h * HEAD_DIM, start + (h + 1) * HEAD_DIM) for h in (0, 2, 1, 3)]
    return jnp.concatenate(
        [blk(0, start), *heads, blk(start + SWA_HEADS * HEAD_DIM, w.shape[axis])], axis=axis)


def kernel(x, c, ctx, c_ctx, norm_mix_w, norm_ffn_w, w_mod, b_mod, w_in, w_out, diff_lambda_q1, diff_lambda_k1, diff_lambda_q2, diff_lambda_k2, diff_subln_w, swa_sink, na_rpb, sconv_w, ffn_w_up, ffn_conv_w, ffn_conv_b, ffn_w_down, final_norm_w):
    nb, n_lat, d = x.shape
    depth = w_in.shape[0]
    assert d == D_MODEL and ctx.shape[1] == CTX_LEN and n_lat % TILE == 0 and nb < 8
    assert n_lat // GRID_W >= NA_ROWS and n_lat % (max(2, BLOCKS_PER_STEP) * KV_BLOCK) == 0

    xrows = x
    cin = jnp.concatenate([c, c_ctx[None, :], jnp.zeros((8 - nb - 1, d), F32)], axis=0)
    mods = _mod_call(cin, w_mod, b_mod)

    tabs = (*_rope_tables(n_lat, DIFF_QK_DIM), *_rope_tables(n_lat, HEAD_DIM))
    mask_b, mask_c = _window_masks(n_lat)
    for l in range(depth):
        mod = mods[l].reshape(8, 1, 6 * D_MODEL)
        w_in_l = _swa_head_order(w_in[l], 1, 3 * GROUP_WIDTH).astype(BF16)
        w_out_l = _swa_head_order(w_out[l], 0, GROUP_WIDTH).astype(BF16)
        qa, ka, vat, bqt, bk, bvt, cqt, ck, cvt, hd = _in_proj_call(
            xrows, ctx, mod, norm_mix_w[l][None, :], w_in_l, tabs, first=(l == 0))

        lambda_init = 0.8 - 0.6 * math.exp(-0.3 * l)
        lam_init = jnp.asarray([lambda_init, 1.0 - lambda_init], F32)
        lam_vecs = jnp.stack([diff_lambda_q1[l], diff_lambda_k1[l],
                              diff_lambda_q2[l], diff_lambda_k2[l]])
        subw = jnp.broadcast_to(diff_subln_w[l][:, None], (HEAD_DIM, QA_TILE))
        oa = _diff_attn_call(lam_init, lam_vecs, qa, ka, vat, subw)

        ctab = (_na_bias_table(na_rpb[l])[None] + mask_c[:, None]) * LOG2E
        obcd = _local_mix_call(swa_sink[l], bqt, bk, bvt, cqt, ck, cvt, mask_b, ctab,
                               hd, sconv_w[l])

        xrows = _out_ffn_call(
            xrows, ctx, oa, obcd, mod, w_out_l, norm_ffn_w[l][None, :],
            ffn_w_up[l].astype(BF16), ffn_conv_w[l], ffn_conv_b[l][None, :],
            ffn_w_down[l].astype(BF16), final_norm_w[None, :],
            final=(l == depth - 1), first=(l == 0))
    return xrows
```

```python
import functools
import math

import numpy as np
import jax
import jax.numpy as jnp
from jax import lax
from jax.experimental import pallas as pl
from jax.experimental.pallas import tpu as pltpu

D_MODEL = 1024
GRID_W = 64
CTX_LEN = 256
HEAD_DIM = 64
GROUP_WIDTH = 256
DIFF_HEADS = 4
DIFF_QK_DIM = 32
SWA_HEADS = 4
SWA_WINDOW = 128
NA_HEADS = 4
NA_ROWS = 8
NA_COLS = 16
D_FF = 2816
IN_COLS = 2816
ROPE_THETA = 10000.0
EPS = 1e-6
NEG_INF = -1e30
LOG2E = math.log2(math.e)

TILE = 256
HALO_F32 = 8
HALO_BF16 = 16
LANES = 128
VMEM_LIMIT = 56 * 1024 * 1024

F32 = jnp.float32
BF16 = jnp.bfloat16


def _params(n_grid):
    return pltpu.CompilerParams(
        dimension_semantics=("arbitrary",) * n_grid, vmem_limit_bytes=VMEM_LIMIT)


def _resident(shape):
    nd = len(shape)
    return pl.BlockSpec(shape, lambda *_: (0,) * nd, pipeline_mode=pl.Buffered(1))


def _mod_spec(n_batch):
    return pl.BlockSpec((1, 1, 6 * D_MODEL), lambda b, j: (jnp.where(j == 0, n_batch, b), 0, 0))


def _rmsnorm_mod(x, w, shift, scale):
    ms = jnp.mean(x * x, axis=-1, keepdims=True)
    return (x * lax.rsqrt(ms + EPS) * w) * (1.0 + scale) + shift


def _dot(a, b):
    return jnp.dot(a, b, preferred_element_type=F32)


MOD_BLOCK = 1536


def _mod_kernel(c_ref, w_ref, b_ref, o_ref):
    c = c_ref[...]
    a = c * (1.0 / (1.0 + jnp.exp(-c)))
    o_ref[0] = _dot(a.astype(BF16), w_ref[0].astype(BF16)) + b_ref[0]


def _mod_call(cin, w_mod, b_mod):
    depth = w_mod.shape[0]
    n = 6 * D_MODEL
    return pl.pallas_call(
        _mod_kernel,
        grid=(depth, n // MOD_BLOCK),
        in_specs=[
            pl.BlockSpec((8, D_MODEL), lambda l, k: (0, 0)),
            pl.BlockSpec((1, D_MODEL, MOD_BLOCK), lambda l, k: (l, 0, k)),
            pl.BlockSpec((1, 1, MOD_BLOCK), lambda l, k: (l, 0, k)),
        ],
        out_specs=pl.BlockSpec((1, 8, MOD_BLOCK), lambda l, k: (l, 0, k)),
        out_shape=jax.ShapeDtypeStruct((depth, 8, n), F32),
        compiler_params=_params(2),
        name="mod_proj",
    )(cin, w_mod, b_mod.reshape(depth, 1, n))


def _rope(x, cos, sin, half):
    lane = lax.broadcasted_iota(jnp.int32, x.shape, 1)
    first = (lane % (2 * half)) < half
    partner = jnp.where(first, pltpu.roll(x, LANES - half, 1), pltpu.roll(x, half, 1))
    return x * cos + partner * sin


def _in_proj_kernel(first, x_ref, ctx_ref, mod_ref, nw_ref, w_ref,
                    cosa_ref, sina_ref, cosb_ref, sinb_ref,
                    qa_ref, ka_ref, vat_ref, bqt_ref, bk_ref, bvt_ref, cqt_ref, ck_ref, cvt_ref,
                    hd_ref):
    mod = mod_ref[0]
    shift = mod[:, 0:D_MODEL]
    scale = mod[:, D_MODEL:2 * D_MODEL]
    x = x_ref[0]
    if first:
        x = jnp.where(pl.program_id(1) == 0, ctx_ref[0], x)
    z = _rmsnorm_mod(x, nw_ref[...], shift, scale)
    h = _dot(z.astype(BF16), w_ref[...])

    def sec(k, n=1):
        return h[:, k * LANES:(k + n) * LANES]

    cosa, sina = cosa_ref[...], sina_ref[...]
    cosb, sinb = cosb_ref[...], sinb_ref[...]
    qa_scale = DIFF_QK_DIM ** -0.5 * LOG2E
    q_scale = HEAD_DIM ** -0.5 * LOG2E
    ha = DIFF_QK_DIM // 4
    hb = HEAD_DIM // 4
    for t in range(2):
        qa_ref[0, :, t * LANES:(t + 1) * LANES] = (
            _rope(sec(t), cosa, sina, ha) * qa_scale).astype(BF16)
        ka_ref[0, :, t * LANES:(t + 1) * LANES] = _rope(sec(2 + t), cosa, sina, ha).astype(BF16)
    vat_ref[0, 0] = sec(4, 2).T.astype(BF16)
    for t in range(2):
        bqt_ref[0, 0, t * LANES:(t + 1) * LANES, :] = (
            _rope(sec(6 + t), cosb, sinb, hb) * q_scale).T.astype(BF16)
    bk_ref[0] = _rope(sec(8), cosb, sinb, hb).astype(BF16)
    bvt_ref[0, 0] = sec(9).T.astype(BF16)
    cqt_ref[0, 0] = (sec(10, 2) * q_scale).T.astype(BF16)
    ck_ref[0] = sec(12, 2).astype(BF16)
    cvt_ref[0, 0] = sec(14, 2).T.astype(BF16)
    hd_ref[0] = sec(16, 6).astype(BF16)


def _latent_tile(first):
    return (lambda j: jnp.maximum(j - 1, 0)) if first else (lambda j: j)


def _in_proj_call(xrows, ctx, mod, norm_w, w_in, tabs, first):
    nb = xrows.shape[0]
    rows = xrows.shape[1] + (CTX_LEN if first else 0)
    nt = rows // TILE
    xtile = _latent_tile(first)
    tile = lambda w: pl.BlockSpec((1, TILE, w), lambda b, j: (b, j, 0))
    ttile = lambda w: pl.BlockSpec((1, 1, w, TILE), lambda b, j: (b, j, 0, 0))
    tab = pl.BlockSpec((TILE, LANES), lambda b, j: (j, 0))
    sds = lambda w: jax.ShapeDtypeStruct((nb, rows, w), BF16)
    tsds = lambda w: jax.ShapeDtypeStruct((nb, nt, w, TILE), BF16)
    return pl.pallas_call(
        functools.partial(_in_proj_kernel, first),
        grid=(nb, nt),
        in_specs=[
            pl.BlockSpec((1, TILE, D_MODEL), lambda b, j: (b, xtile(j), 0)),
            pl.BlockSpec((1, CTX_LEN, D_MODEL), lambda b, j: (b, 0, 0)),
            _mod_spec(nb),
            _resident((1, D_MODEL)),
            _resident((D_MODEL, IN_COLS)),
            tab, tab, tab, tab,
        ],
        out_specs=[
            tile(256), tile(256), ttile(256),
            ttile(256), tile(128), ttile(128),
            ttile(256), tile(256), ttile(256),
            tile(768),
        ],
        out_shape=[
            sds(256), sds(256), tsds(256),
            tsds(256), sds(128), tsds(128),
            tsds(256), sds(256), tsds(256),
            sds(768),
        ],
        compiler_params=_params(2),
        name="in_proj",
    )(xrows, ctx, mod, norm_w, w_in, *tabs)


QA_TILE = 256
KV_BLOCK = 512
BLOCKS_PER_STEP = 16
LOOKAHEAD = 2
EXP_CAP = 100.0
PV_ROWS = HEAD_DIM + HALO_BF16


def _diff_attn_kernel(li_ref, lv_ref, q_ref, k_ref, vt_ref, sw_ref, o_ref,
                      qst_ref, sa_ref, sb_ref, mca_ref, mcb_ref, m_ref, acc_ref, gap_ref):
    i = pl.program_id(1)
    s_refs = (sa_ref, sb_ref)
    mc_refs = (mca_ref, mcb_ref)
    q = q_ref[0].astype(F32)
    lane = lax.broadcasted_iota(jnp.int32, q.shape, 1) // DIFF_QK_DIM
    for h in range(DIFF_HEADS):
        qs = jnp.concatenate([jnp.where(lane == 2 * h + c, q, 0.0) for c in range(2)], axis=0)
        qst_ref[h] = qs.T.astype(BF16)
    n_cols = 2 * QA_TILE
    n_blk = (k_ref.shape[1] - CTX_LEN) // KV_BLOCK
    tiles_per_blk = KV_BLOCK // TILE

    def value_rows(vt, h):
        ones = jnp.ones((HALO_BF16, vt.shape[1]), BF16)
        return jnp.concatenate([vt[h * HEAD_DIM:(h + 1) * HEAD_DIM], ones], axis=0)

    def ctx_scores(h):
        return _dot(k_ref[0, 0:CTX_LEN, :], qst_ref[h])

    def ctx_init(h, s):
        m0 = jnp.max(s, axis=0, keepdims=True)
        acc_ref[h] = _dot(value_rows(vt_ref[0, 0], h), jnp.exp2(s - m0).astype(BF16))
        m_ref[h] = m0

    def score_head(blk, slot, h):
        row0 = pl.multiple_of(CTX_LEN + (blk - 1) * KV_BLOCK, TILE)
        s = _dot(k_ref[0, pl.ds(row0, KV_BLOCK), :], qst_ref[h])
        s_refs[slot][h] = s
        mc_refs[slot][h] = jnp.max(s, axis=0, keepdims=True)

    def block_step(score_blk, acc_blk):
        if acc_blk is not None:
            ablk, aslot = acc_blk
            t0 = 1 + (ablk - 1) * tiles_per_blk
            vt = jnp.concatenate([vt_ref[0, t0 + u] for u in range(tiles_per_blk)], axis=1)
        for h in range(DIFF_HEADS):
            if score_blk is not None:
                score_head(*score_blk, h)
            if acc_blk is not None:
                m_old = m_ref[h]
                m_new = jnp.maximum(m_old, mc_refs[aslot][h])
                p = jnp.exp2(s_refs[aslot][h] - m_new).astype(BF16)
                acc_ref[h] = acc_ref[h] * jnp.exp2(m_old - m_new) + _dot(value_rows(vt, h), p)
                m_ref[h] = m_new

    def two_pass_latent():
        scores0 = [ctx_scores(h) for h in range(DIFF_HEADS)]
        for h in range(DIFF_HEADS):
            score_head(1, 0, h)
            ctx_init(h, scores0[h])

        def step(it, carry):
            b = 2 * it + 1
            block_step((b + 1, 1), (b, 0))
            block_step((b + 2, 0), (b + 1, 1))
            return carry

        lax.fori_loop(0, n_blk // 2 - 1, step, 0)
        block_step((n_blk, 1), (n_blk - 1, 0))
        block_step(None, (n_blk, 1))

    def one_pass_latent():
        scores0 = [ctx_scores(h) for h in range(DIFF_HEADS)]
        for h in range(DIFF_HEADS):
            ctx_init(h, scores0[h])
            gap_ref[h] = jnp.zeros(gap_ref.shape[1:], F32)

        def scores(unit):
            blk, h, c = unit
            row0 = pl.multiple_of(CTX_LEN + (blk - 1) * KV_BLOCK, TILE)
            return _dot(k_ref[0, pl.ds(row0, KV_BLOCK), :],
                        qst_ref[h, :, c * QA_TILE:(c + 1) * QA_TILE])

        def consume(unit, s, vt):
            _, h, c = unit
            cols = slice(c * QA_TILE, (c + 1) * QA_TILE)
            m_old = m_ref[h, :, cols]
            m_cur = jnp.max(s, axis=0, keepdims=True)
            pv = _dot(value_rows(vt, h), jnp.exp2((s - m_old).astype(BF16)))
            m_new = jnp.maximum(m_old, m_cur)
            acc_ref[h, :, cols] = (acc_ref[h, :, cols] + pv) * jnp.exp2(m_old - m_new)
            m_ref[h, :, cols] = m_new
            gap_ref[h, :, cols] = jnp.maximum(gap_ref[h, :, cols], m_cur - m_old)

        def step(it, carry):
            b0 = it * BLOCKS_PER_STEP + 1
            units = [(b0 + u, h, c) for u in range(BLOCKS_PER_STEP)
                     for h in range(DIFF_HEADS) for c in range(2)]
            pending = [scores(u) for u in units[:LOOKAHEAD]]
            for n, unit in enumerate(units):
                if n + LOOKAHEAD < len(units):
                    pending.append(scores(units[n + LOOKAHEAD]))
                if unit[1] == 0 and unit[2] == 0:
                    t0 = 1 + (unit[0] - 1) * tiles_per_blk
                    vt = jnp.concatenate(
                        [vt_ref[0, t0 + u] for u in range(tiles_per_blk)], axis=1)
                consume(unit, pending.pop(0), vt)
            return carry

        lax.fori_loop(0, n_blk // BLOCKS_PER_STEP, step, 0)

    def finalize():
        lv = lv_ref[...]
        lam = (jnp.exp(jnp.sum(lv[0:1] * lv[1:2], axis=1, keepdims=True))
               - jnp.exp(jnp.sum(lv[2:3] * lv[3:4], axis=1, keepdims=True)) + li_ref[0])
        out_scale = li_ref[1]
        outs = []
        for h in range(DIFF_HEADS):
            a = acc_ref[h]
            o1 = a[0:HEAD_DIM, 0:QA_TILE] / a[HEAD_DIM:HEAD_DIM + 1, 0:QA_TILE]
            o2 = a[0:HEAD_DIM, QA_TILE:n_cols] / a[HEAD_DIM:HEAD_DIM + 1, QA_TILE:n_cols]
            o = o1 - lam * o2
            ms = jnp.mean(o * o, axis=0, keepdims=True)
            outs.append(o * lax.rsqrt(ms + EPS) * sw_ref[...] * out_scale)
        o_ref[0] = jnp.concatenate(outs, axis=0).T.astype(BF16)

    @pl.when(i < CTX_LEN // QA_TILE)
    def _():
        for h in range(DIFF_HEADS):
            ctx_init(h, ctx_scores(h))
        finalize()

    @pl.when(i >= CTX_LEN // QA_TILE)
    def _():
        one_pass_latent()
        finalize()

        @pl.when(jnp.max(gap_ref[...]) > EXP_CAP)
        def _():
            two_pass_latent()
            finalize()


def _diff_attn_call(lam_init, lam_vecs, qa, ka, vat, subw):
    nb, rows, _ = qa.shape
    nt = rows // TILE
    return pl.pallas_call(
        _diff_attn_kernel,
        grid=(nb, rows // QA_TILE),
        in_specs=[
            pl.BlockSpec(memory_space=pltpu.SMEM),
            _resident((4, DIFF_QK_DIM)),
            pl.BlockSpec((1, QA_TILE, 256), lambda b, i: (b, i, 0)),
            pl.BlockSpec((1, rows, 256), lambda b, i: (b, 0, 0)),
            pl.BlockSpec((1, nt, 256, TILE), lambda b, i: (b, 0, 0, 0)),
            _resident((HEAD_DIM, QA_TILE)),
        ],
        out_specs=pl.BlockSpec((1, QA_TILE, 256), lambda b, i: (b, i, 0)),
        out_shape=jax.ShapeDtypeStruct((nb, rows, 256), BF16),
        scratch_shapes=[
            pltpu.VMEM((DIFF_HEADS, 256, 2 * QA_TILE), BF16),
            pltpu.VMEM((DIFF_HEADS, KV_BLOCK, 2 * QA_TILE), F32),
            pltpu.VMEM((DIFF_HEADS, KV_BLOCK, 2 * QA_TILE), F32),
            pltpu.VMEM((DIFF_HEADS, 1, 2 * QA_TILE), F32),
            pltpu.VMEM((DIFF_HEADS, 1, 2 * QA_TILE), F32),
            pltpu.VMEM((DIFF_HEADS, 1, 2 * QA_TILE), F32),
            pltpu.VMEM((DIFF_HEADS, PV_ROWS, 2 * QA_TILE), F32),
            pltpu.VMEM((DIFF_HEADS, 1, 2 * QA_TILE), F32),
        ],
        compiler_params=_params(2),
        name="diff_attn",
    )(lam_init, lam_vecs, qa, ka, vat, subw)


MIX_LOOKAHEAD = 2


def _head_rows(qt, k):
    row = lax.broadcasted_iota(jnp.int32, qt.shape, 0) // HEAD_DIM
    return jnp.where(row == k, qt, jnp.zeros_like(qt))


def _value_rows(vt_parts, rows):
    v = jnp.concatenate([part[rows] for part in vt_parts], axis=1)
    return jnp.concatenate([v, jnp.ones((HALO_BF16, v.shape[1]), BF16)], axis=0)


def _local_mix_kernel(sink_ref, bqt_ref, bkp_ref, bkc_ref, bkn_ref, bkx_ref,
                      bvtp_ref, bvtc_ref, bvtn_ref, bvtx_ref,
                      cqt_ref, ckp_ref, ckc_ref, ckn_ref, ckx_ref,
                      cvtp_ref, cvtc_ref, cvtn_ref, cvtx_ref, bmask_ref, ctab_ref,
                      dp_ref, dc_ref, dn_ref, cw_ref, o_ref, s_ref):
    j = pl.program_id(1)
    n_tiles = pl.num_programs(1)
    half = SWA_WINDOW

    k_b = jnp.concatenate([bkp_ref[0], bkc_ref[0], bkn_ref[0], bkx_ref[0]], axis=0)
    vt_b = [bvtp_ref[0, 0][:, half:], bvtc_ref[0, 0], bvtn_ref[0, 0][:, :half], bvtx_ref[0, 0]]
    vt_c = [cvtp_ref[0, 0], cvtc_ref[0, 0], cvtn_ref[0, 0], cvtx_ref[0, 0]]

    units = [(mixer, p, k) for mixer in "BC" for p in range(2) for k in range(2)]

    def scores(unit):
        mixer, p, k = unit
        rows = slice(p * LANES, (p + 1) * LANES)
        if mixer == "B":
            return _dot(k_b, _head_rows(bqt_ref[0, 0, rows, :], k))
        k_c = jnp.concatenate([ckp_ref[0, :, rows], ckc_ref[0, :, rows],
                               ckn_ref[0, :, rows], ckx_ref[0, :, rows]], axis=0)
        return _dot(k_c, _head_rows(cqt_ref[0, 0, rows, :], k))

    def attend(unit, s):
        mixer, p, k = unit
        feat = slice(k * HEAD_DIM, (k + 1) * HEAD_DIM)
        if mixer == "B":
            s = s + bmask_ref[0]
            sink = sink_ref[2 * k + p] * LOG2E
            m = jnp.maximum(jnp.max(s, axis=0, keepdims=True), sink)
            pv = _dot(_value_rows(vt_b, feat), jnp.exp2(s - m).astype(BF16))
            return pv[0:HEAD_DIM] / (pv[HEAD_DIM:HEAD_DIM + 1] + jnp.exp2(sink - m))
        s = s + ctab_ref[0, 2 * p + k]
        m = jnp.max(s, axis=0, keepdims=True)
        rows = slice(p * LANES + k * HEAD_DIM, p * LANES + (k + 1) * HEAD_DIM)
        pv = _dot(_value_rows(vt_c, rows), jnp.exp2(s - m).astype(BF16))
        return pv[0:HEAD_DIM] / pv[HEAD_DIM:HEAD_DIM + 1]

    def write_out(outs):
        o_ref[0, :, 0:GROUP_WIDTH] = jnp.concatenate(outs[0:4], axis=0).T.astype(BF16)
        o_ref[0, :, GROUP_WIDTH:2 * GROUP_WIDTH] = jnp.concatenate(outs[4:8], axis=0).T.astype(BF16)

    def two_pass():
        def park(n):
            s = scores(units[n])
            s_ref[n % (MIX_LOOKAHEAD + 1), 0:s.shape[0], :] = s

        for n in range(MIX_LOOKAHEAD):
            park(n)
        outs = []
        for n, unit in enumerate(units):
            if n + MIX_LOOKAHEAD < len(units):
                park(n + MIX_LOOKAHEAD)
            n_keys = (TILE + 2 * SWA_WINDOW if unit[0] == "B" else 3 * TILE) + CTX_LEN
            outs.append(attend(unit, s_ref[n % (MIX_LOOKAHEAD + 1), 0:n_keys, :]))
        write_out(outs)

    def unit_operands(unit):
        mixer, p, k = unit
        rows = slice(p * LANES, (p + 1) * LANES)
        if mixer == "B":
            n_loc = TILE + 2 * half
            return (_head_rows(bqt_ref[0, 0, rows, :], k), k_b[0:n_loc], k_b[n_loc:],
                    bmask_ref[0, 0:n_loc, :], vt_b, slice(k * HEAD_DIM, (k + 1) * HEAD_DIM))
        k_loc = jnp.concatenate([ckp_ref[0, :, rows], ckc_ref[0, :, rows], ckn_ref[0, :, rows]],
                                axis=0)
        feat = slice(p * LANES + k * HEAD_DIM, p * LANES + (k + 1) * HEAD_DIM)
        return (_head_rows(cqt_ref[0, 0, rows, :], k), k_loc, ckx_ref[0, :, rows],
                ctab_ref[0, 2 * p + k, 0:3 * TILE, :], vt_c, feat)

    def one_pass_scores(unit):
        qt, k_loc, k_ctx, _, _, _ = unit_operands(unit)
        return _dot(k_ctx, qt), _dot(k_loc, qt)

    def one_pass_attend(unit, s_ctx, s_loc):
        mixer, p, k = unit
        _, _, _, table, vt, feat = unit_operands(unit)
        m = jnp.max(s_ctx, axis=0, keepdims=True)
        if mixer == "B":
            sink = sink_ref[2 * k + p] * LOG2E
            m = jnp.maximum(m, sink)
        s_loc = s_loc + table
        excess = jnp.max(s_loc, axis=0, keepdims=True) - m
        vx = _value_rows(vt, feat)
        n_loc = s_loc.shape[0]
        pv = (_dot(vx[:, 0:n_loc], jnp.exp2(s_loc - m).astype(BF16))
              + _dot(vx[:, n_loc:], jnp.exp2(s_ctx - m).astype(BF16)))
        l = pv[HEAD_DIM:HEAD_DIM + 1]
        if mixer == "B":
            l = l + jnp.exp2(sink - m)
        return pv[0:HEAD_DIM] / l, excess

    pending = [one_pass_scores(units[0])]
    outs, worst = [], None
    for n, unit in enumerate(units):
        if n + 1 < len(units):
            pending.append(one_pass_scores(units[n + 1]))
        o, excess = one_pass_attend(unit, *pending.pop(0))
        outs.append(o)
        worst = excess if worst is None else jnp.maximum(worst, excess)
    write_out(outs)

    @pl.when(jnp.max(worst) > EXP_CAP)
    def _():
        two_pass()

    has_prev = (j > 1).astype(F32)
    has_next = ((j > 0) & (j < n_tiles - 1)).astype(F32)
    dc = dc_ref[0].astype(F32)
    dp = dp_ref[0].astype(F32)[HALO_BF16 - HALO_F32:]
    dn = dn_ref[0].astype(F32)[:HALO_F32]
    gw = GROUP_WIDTH
    z = jnp.concatenate([dp[:, gw:2 * gw] * dp[:, 2 * gw:] * has_prev,
                         dc[:, gw:2 * gw] * dc[:, 2 * gw:],
                         dn[:, gw:2 * gw] * dn[:, 2 * gw:] * has_next], axis=0)
    cw = cw_ref[...]
    c0 = HALO_F32
    conv = (z[c0 - 1:c0 - 1 + TILE] * cw[0:1] + z[c0:c0 + TILE] * cw[1:2]
            + z[c0 + 1:c0 + 1 + TILE] * cw[2:3])
    o_ref[0, :, 4 * LANES:6 * LANES] = (dc[:, 0:gw] * conv).astype(BF16)


def _tile_variants(nt):
    return (0, 1, 2, nt - 1)


def _local_mix_call(sink, bqt, bk, bvt, cqt, ck, cvt, bmask, ctab, hd, conv_w):
    nb, rows, _ = bk.shape
    nt = rows // TILE
    variant = lambda j: jnp.where(j == 0, 0, jnp.where(j == 1, 1, jnp.where(j == nt - 1, 3, 2)))
    hb = TILE // SWA_WINDOW
    h16 = TILE // HALO_BF16
    prev_j = lambda j: jnp.maximum(j - 1, 0)
    next_j = lambda j: jnp.minimum(j + 1, nt - 1)
    cur = lambda w: pl.BlockSpec((1, TILE, w), lambda b, j: (b, j, 0))
    prev = lambda w: pl.BlockSpec((1, TILE, w), lambda b, j: (b, prev_j(j), 0))
    nxt = lambda w: pl.BlockSpec((1, TILE, w), lambda b, j: (b, next_j(j), 0))
    ctx = lambda w: pl.BlockSpec((1, TILE, w), lambda b, j: (b, 0, 0))
    tcur = lambda w: pl.BlockSpec((1, 1, w, TILE), lambda b, j: (b, j, 0, 0))
    tprev = lambda w: pl.BlockSpec((1, 1, w, TILE), lambda b, j: (b, prev_j(j), 0, 0))
    tnxt = lambda w: pl.BlockSpec((1, 1, w, TILE), lambda b, j: (b, next_j(j), 0, 0))
    tctx = lambda w: pl.BlockSpec((1, 1, w, TILE), lambda b, j: (b, 0, 0, 0))
    return pl.pallas_call(
        _local_mix_kernel,
        grid=(nb, nt),
        in_specs=[
            pl.BlockSpec(memory_space=pltpu.SMEM),
            tcur(256),
            pl.BlockSpec((1, SWA_WINDOW, 128), lambda b, j: (b, jnp.maximum(j * hb - 1, 0), 0)),
            cur(128),
            pl.BlockSpec((1, SWA_WINDOW, 128), lambda b, j: (b, jnp.minimum((j + 1) * hb, nt * hb - 1), 0)),
            ctx(128),
            tprev(128), tcur(128), tnxt(128), tctx(128),
            tcur(256),
            prev(256), cur(256), nxt(256), ctx(256),
            tprev(256), tcur(256), tnxt(256), tctx(256),
            pl.BlockSpec((1, TILE + 2 * SWA_WINDOW + CTX_LEN, TILE), lambda b, j: (variant(j), 0, 0)),
            pl.BlockSpec((1, NA_HEADS, 3 * TILE + CTX_LEN, TILE),
                         lambda b, j: (variant(j), 0, 0, 0)),
            pl.BlockSpec((1, HALO_BF16, 768), lambda b, j: (b, jnp.maximum(j * h16 - 1, 0), 0)),
            cur(768),
            pl.BlockSpec((1, HALO_BF16, 768), lambda b, j: (b, jnp.minimum((j + 1) * h16, nt * h16 - 1), 0)),
            _resident((3, GROUP_WIDTH)),
        ],
        out_specs=pl.BlockSpec((1, TILE, 768), lambda b, j: (b, j, 0)),
        out_shape=jax.ShapeDtypeStruct((nb, rows, 768), BF16),
        scratch_shapes=[pltpu.VMEM((MIX_LOOKAHEAD + 1, 3 * TILE + CTX_LEN, TILE), F32)],
        compiler_params=_params(2),
        name="local_mix",
    )(sink, bqt, bk, bk, bk, bk, bvt, bvt, bvt, bvt, cqt, ck, ck, ck, ck, cvt, cvt, cvt, cvt,
      bmask, ctab, hd, hd, hd, conv_w)


EXT = TILE + 2 * HALO_F32


def _halo_f32(prev_ref, next_ref):
    p = prev_ref[0].astype(F32)
    n = next_ref[0].astype(F32)
    return p[p.shape[0] - HALO_F32:], n[:HALO_F32]


def _ffn_tile(final, first, j, n_tiles, xp_ref, xc_ref, xn_ref, ctx_ref, ap_ref, ac_ref, an_ref,
              mp_ref, mc_ref, mn_ref, mod_ref, wo_ref, nw_ref, wu_ref, cw_ref, cb_ref, wd_ref,
              fw_ref, u_ref):
    mod = mod_ref[0]
    g1 = mod[:, 2 * D_MODEL:3 * D_MODEL]
    sh2 = mod[:, 3 * D_MODEL:4 * D_MODEL]
    sc2 = mod[:, 4 * D_MODEL:5 * D_MODEL]
    g2 = mod[:, 5 * D_MODEL:6 * D_MODEL]

    ap, an = _halo_f32(ap_ref, an_ref)
    mp, mn = _halo_f32(mp_ref, mn_ref)
    mix = jnp.concatenate([
        jnp.concatenate([ap, mp], axis=1),
        jnp.concatenate([ac_ref[0], mc_ref[0]], axis=1).astype(F32),
        jnp.concatenate([an, mn], axis=1)], axis=0).astype(BF16)
    xc = xc_ref[0]
    if first:
        xc = jnp.where(j == 0, ctx_ref[0], xc)
    x = jnp.concatenate([xp_ref[0], xc, xn_ref[0]], axis=0)
    x1 = x + g1 * _dot(mix, wo_ref[...])
    f = _rmsnorm_mod(x1, nw_ref[...], sh2, sc2).astype(BF16)

    has_prev = (j > 1).astype(F32)
    has_next = ((j > 0) & (j < n_tiles - 1)).astype(F32)
    lo = HALO_F32 - 1
    hi = HALO_F32 + TILE
    u = _dot(f, wu_ref[...])
    slabs = []
    for t in range(2 * D_FF // LANES):
        cols = slice(t * LANES, (t + 1) * LANES)
        u_ref[t] = u[:, cols]
        u_ref[t, lo:lo + 1, :] = u_ref[t, lo:lo + 1, :] * has_prev
        u_ref[t, hi:hi + 1, :] = u_ref[t, hi:hi + 1, :] * has_next
        cw = cw_ref[:, cols]
        slabs.append(u_ref[t, pl.ds(lo, TILE), :] * cw[0:1]
                     + u_ref[t, pl.ds(lo + 1, TILE), :] * cw[1:2]
                     + u_ref[t, pl.ds(lo + 2, TILE), :] * cw[2:3] + cb_ref[:, cols])
    g = jnp.concatenate(slabs[:D_FF // LANES], axis=1)
    v = jnp.concatenate(slabs[D_FF // LANES:], axis=1)
    act = (g * (1.0 / (1.0 + jnp.exp(-g))) * v).astype(BF16)
    x2 = x1[HALO_F32:HALO_F32 + TILE] + g2 * _dot(act, wd_ref[...])
    if final:
        ms = jnp.mean(x2 * x2, axis=-1, keepdims=True)
        x2 = x2 * lax.rsqrt(ms + EPS) * fw_ref[...]
    return x2


N_TILE_REFS = 11


def _out_ffn_kernel(final, first, *refs):
    tile_refs, weights, (o_ref, u_ref) = refs[:N_TILE_REFS], refs[N_TILE_REFS:-2], refs[-2:]
    o_ref[0] = _ffn_tile(final, first, pl.program_id(1), pl.num_programs(1),
                         *tile_refs, *weights, u_ref)


def _out_ffn_call(xrows, ctx, oa, obcd, mod, w_out, norm_w, w_up, conv_w, conv_b, w_down, final_w,
                  final, first):
    nb, rows, _ = oa.shape
    nt = rows // TILE
    h8 = TILE // HALO_F32
    h16 = TILE // HALO_BF16

    def trio(w, halo, per_tile, tile_of, n_arr_tiles):
        return [
            pl.BlockSpec((1, halo, w),
                         lambda b, j: (b, jnp.maximum(tile_of(j) * per_tile - 1, 0), 0)),
            pl.BlockSpec((1, TILE, w), lambda b, j: (b, tile_of(j), 0)),
            pl.BlockSpec((1, halo, w), lambda b, j: (
                b, jnp.minimum((tile_of(j) + 1) * per_tile, n_arr_tiles * per_tile - 1), 0)),
        ]

    same = lambda j: j
    tile_specs = [*trio(D_MODEL, HALO_F32, h8, _latent_tile(first), xrows.shape[1] // TILE),
                  pl.BlockSpec((1, CTX_LEN, D_MODEL), lambda b, j: (b, 0, 0)),
                  *trio(256, HALO_BF16, h16, same, nt), *trio(768, HALO_BF16, h16, same, nt),
                  _mod_spec(nb)]
    weight_specs = [
        _resident((D_MODEL, D_MODEL)),
        _resident((1, D_MODEL)),
        _resident((D_MODEL, 2 * D_FF)),
        _resident((3, 2 * D_FF)),
        _resident((1, 2 * D_FF)),
        _resident((D_FF, D_MODEL)),
        _resident((1, D_MODEL)),
    ]
    if final:
        out_spec = pl.BlockSpec((1, TILE, D_MODEL), lambda b, j: (b, jnp.maximum(j - 1, 0), 0))
        out_shape = jax.ShapeDtypeStruct((nb, rows - CTX_LEN, D_MODEL), F32)
    else:
        out_spec = pl.BlockSpec((1, TILE, D_MODEL), lambda b, j: (b, j, 0))
        out_shape = jax.ShapeDtypeStruct((nb, rows, D_MODEL), F32)
    return pl.pallas_call(
        functools.partial(_out_ffn_kernel, final, first),
        grid=(nb, nt),
        in_specs=[*tile_specs, *weight_specs],
        out_specs=out_spec,
        out_shape=out_shape,
        scratch_shapes=[pltpu.VMEM((2 * D_FF // LANES, EXT, LANES), F32)],
        compiler_params=_params(2),
        name="out_ffn_final" if final else "out_ffn",
    )(xrows, xrows, xrows, ctx, oa, oa, oa, obcd, obcd, obcd, mod, w_out, norm_w, w_up, conv_w,
      conv_b, w_down, final_w)


def _rope_tables(n_lat, dim):
    quarter = dim // 4
    t = np.arange(n_lat)
    pos = np.stack([t // GRID_W, t % GRID_W], axis=1).astype(np.float32)
    freqs = jnp.asarray(ROPE_THETA, F32) ** (-(jnp.arange(quarter, dtype=F32) / quarter))
    lane = np.arange(LANES) % dim
    axis = lane // (dim // 2)
    fidx = lane % quarter
    sign = np.where((lane % (dim // 2)) < quarter, -1.0, 1.0).astype(np.float32)
    ang = jnp.asarray(pos)[:, axis] * freqs[fidx][None, :]
    cos = jnp.concatenate([jnp.ones((CTX_LEN, LANES), F32), jnp.cos(ang)], axis=0)
    sin = jnp.concatenate([jnp.zeros((CTX_LEN, LANES), F32), jnp.sin(ang) * sign[None, :]], axis=0)
    return cos, sin


def _na_bias_table(rpb):
    rpt = TILE // GRID_W
    n_heads, n_dr, n_dc = rpb.shape
    cq = np.arange(GRID_W)[:, None]
    ck = np.arange(GRID_W)[None, :]
    dc = np.clip(ck - cq, -(NA_COLS - 1), NA_COLS - 1) + NA_COLS - 1
    col_start = np.clip(cq - NA_COLS // 2, 0, GRID_W - NA_COLS)
    col_ok = (ck >= col_start) & (ck < col_start + NA_COLS)
    pick = dc[None, :, :] == np.arange(n_dc)[:, None, None]
    e = jnp.sum(jnp.where(pick[None, None], rpb.astype(F32)[:, :, :, None, None], 0.0), axis=2)
    e = jnp.where(col_ok[None, None], e, NEG_INF)
    blocks = []
    for rq in range(rpt):
        lo = NA_ROWS - 1 - rpt - rq
        blocks.append(e[:, lo:lo + 3 * rpt].transpose(0, 1, 3, 2))
    local = jnp.stack(blocks, axis=3).reshape(n_heads, 3 * TILE, TILE)
    return jnp.concatenate([local, jnp.zeros((n_heads, CTX_LEN, TILE), F32)], axis=1)


def _window_masks(n_lat):
    nt = 1 + n_lat // TILE
    half = SWA_WINDOW
    rpt = TILE // GRID_W
    n_rows = n_lat // GRID_W
    rq = np.arange(TILE)[None, :]
    kb = np.arange(TILE + 2 * half + CTX_LEN)[:, None]
    kc = np.arange(3 * TILE + CTX_LEN)[:, None]
    masks_b, masks_c = [], []
    for j in _tile_variants(nt):
        kpos = (j - 1) * TILE - half + kb
        band = (np.abs(kb - half - rq) <= SWA_WINDOW) & (kpos >= 0) & (kpos < n_lat) & (j > 0)
        masks_b.append(band | (kb >= TILE + 2 * half))
        r = (j - 1) * rpt + rq // GRID_W
        kr = (j - 2) * rpt + kc // GRID_W
        rs = np.clip(r - NA_ROWS // 2, 0, n_rows - NA_ROWS)
        masks_c.append(((kr >= rs) & (kr < rs + NA_ROWS) & (j > 0)) | (kc >= 3 * TILE))
    to_add = lambda ms: jnp.where(jnp.asarray(np.stack(ms)), 0.0, NEG_INF).astype(F32)
    return to_add(masks_b), to_add(masks_c)


def _swa_head_order(w, axis, start):
    def blk(lo, hi):
        return lax.slice_in_dim(w, lo, hi, axis=axis)
    heads = [blk(start + h * HEAD_DIM, start + (h + 1) * HEAD_DIM) for h in (0, 2, 1, 3)]
    return jnp.concatenate(
        [blk(0, start), *heads, blk(start + SWA_HEADS * HEAD_DIM, w.shape[axis])], axis=axis)


def kernel(x, c, ctx, c_ctx, norm_mix_w, norm_ffn_w, w_mod, b_mod, w_in, w_out, diff_lambda_q1, diff_lambda_k1, diff_lambda_q2, diff_lambda_k2, diff_subln_w, swa_sink, na_rpb, sconv_w, ffn_w_up, ffn_conv_w, ffn_conv_b, ffn_w_down, final_norm_w):
    nb, n_lat, d = x.shape
    depth = w_in.shape[0]
    assert d == D_MODEL and ctx.shape[1] == CTX_LEN and n_lat % TILE == 0 and nb < 8
    assert n_lat // GRID_W >= NA_ROWS and n_lat % (max(2, BLOCKS_PER_STEP) * KV_BLOCK) == 0

    xrows = x
    cin = jnp.concatenate([c, c_ctx[None, :], jnp.zeros((8 - nb - 1, d), F32)], axis=0)
    mods = _mod_call(cin, w_mod, b_mod)

    tabs = (*_rope_tables(n_lat, DIFF_QK_DIM), *_rope_tables(n_lat, HEAD_DIM))
    mask_b, mask_c = _window_masks(n_lat)
    for l in range(depth):
        mod = mods[l].reshape(8, 1, 6 * D_MODEL)
        w_in_l = _swa_head_order(w_in[l], 1, 3 * GROUP_WIDTH).astype(BF16)
        w_out_l = _swa_head_order(w_out[l], 0, GROUP_WIDTH).astype(BF16)
        qa, ka, vat, bqt, bk, bvt, cqt, ck, cvt, hd = _in_proj_call(
            xrows, ctx, mod, norm_mix_w[l][None, :], w_in_l, tabs, first=(l == 0))

        lambda_init = 0.8 - 0.6 * math.exp(-0.3 * l)
        lam_init = jnp.asarray([lambda_init, 1.0 - lambda_init], F32)
        lam_vecs = jnp.stack([diff_lambda_q1[l], diff_lambda_k1[l],
                              diff_lambda_q2[l], diff_lambda_k2[l]])
        subw = jnp.broadcast_to(diff_subln_w[l][:, None], (HEAD_DIM, QA_TILE))
        oa = _diff_attn_call(lam_init, lam_vecs, qa, ka, vat, subw)

        ctab = (_na_bias_table(na_rpb[l])[None] + mask_c[:, None]) * LOG2E
        obcd = _local_mix_call(swa_sink[l], bqt, bk, bvt, cqt, ck, cvt, mask_b, ctab,
                               hd, sconv_w[l])

        xrows = _out_ffn_call(
            xrows, ctx, oa, obcd, mod, w_out_l, norm_ffn_w[l][None, :],
            ffn_w_up[l].astype(BF16), ffn_conv_w[l], ffn_conv_b[l][None, :],
            ffn_w_down[l].astype(BF16), final_norm_w[None, :],
            final=(l == depth - 1), first=(l == 0))
    return xrows
```

```python
import functools
import math

import numpy as np
import jax
import jax.numpy as jnp
from jax import lax
from jax.experimental import pallas as pl
from jax.experimental.pallas import tpu as pltpu

D_MODEL = 1024
GRID_W = 64
CTX_LEN = 256
HEAD_DIM = 64
GROUP_WIDTH = 256
DIFF_HEADS = 4
DIFF_QK_DIM = 32
SWA_HEADS = 4
SWA_WINDOW = 128
NA_HEADS = 4
NA_ROWS = 8
NA_COLS = 16
D_FF = 2816
IN_COLS = 2816
ROPE_THETA = 10000.0
EPS = 1e-6
NEG_INF = -1e30
LOG2E = math.log2(math.e)

TILE = 256
HALO_F32 = 8
HALO_BF16 = 16
LANES = 128
VMEM_LIMIT = 56 * 1024 * 1024

F32 = jnp.float32
BF16 = jnp.bfloat16


def _params(n_grid, fuse_inputs=None):
    return pltpu.CompilerParams(
        dimension_semantics=("arbitrary",) * n_grid, vmem_limit_bytes=VMEM_LIMIT,
        allow_input_fusion=fuse_inputs)


def _resident(shape):
    nd = len(shape)
    return pl.BlockSpec(shape, lambda *_: (0,) * nd, pipeline_mode=pl.Buffered(1))


def _mod_spec(n_batch):
    return pl.BlockSpec((1, 1, 6 * D_MODEL), lambda b, j: (jnp.where(j == 0, n_batch, b), 0, 0))


def _rmsnorm_mod(x, w, shift, scale):
    ms = jnp.mean(x * x, axis=-1, keepdims=True)
    return (x * lax.rsqrt(ms + EPS) * w) * (1.0 + scale) + shift


def _dot(a, b):
    return jnp.dot(a, b, preferred_element_type=F32)


MOD_BLOCK = 1536


def _mod_kernel(c_ref, w_ref, b_ref, o_ref):
    c = c_ref[...]
    a = c * (1.0 / (1.0 + jnp.exp(-c)))
    o_ref[0] = _dot(a.astype(BF16), w_ref[0].astype(BF16)) + b_ref[0]


def _mod_call(cin, w_mod, b_mod):
    depth = w_mod.shape[0]
    n = 6 * D_MODEL
    return pl.pallas_call(
        _mod_kernel,
        grid=(depth, n // MOD_BLOCK),
        in_specs=[
            pl.BlockSpec((8, D_MODEL), lambda l, k: (0, 0)),
            pl.BlockSpec((1, D_MODEL, MOD_BLOCK), lambda l, k: (l, 0, k)),
            pl.BlockSpec((1, 1, MOD_BLOCK), lambda l, k: (l, 0, k)),
        ],
        out_specs=pl.BlockSpec((1, 8, MOD_BLOCK), lambda l, k: (l, 0, k)),
        out_shape=jax.ShapeDtypeStruct((depth, 8, n), F32),
        compiler_params=_params(2),
        name="mod_proj",
    )(cin, w_mod, b_mod.reshape(depth, 1, n))


def _rope(x, cos, sin, half):
    lane = lax.broadcasted_iota(jnp.int32, x.shape, 1)
    first = (lane % (2 * half)) < half
    partner = jnp.where(first, pltpu.roll(x, LANES - half, 1), pltpu.roll(x, half, 1))
    return x * cos + partner * sin


def _in_proj_kernel(first, x_ref, ctx_ref, mod_ref, nw_ref, w_ref,
                    cosa_ref, sina_ref, cosb_ref, sinb_ref,
                    qa_ref, ka_ref, vat_ref, bqt_ref, bk_ref, bvt_ref, cqt_ref, ck_ref, cvt_ref,
                    hd_ref):
    mod = mod_ref[0]
    shift = mod[:, 0:D_MODEL]
    scale = mod[:, D_MODEL:2 * D_MODEL]
    x = x_ref[0]
    if first:
        x = jnp.where(pl.program_id(1) == 0, ctx_ref[0], x)
    z = _rmsnorm_mod(x, nw_ref[...], shift, scale)
    h = _dot(z.astype(BF16), w_ref[...])

    def sec(k, n=1):
        return h[:, k * LANES:(k + n) * LANES]

    cosa, sina = cosa_ref[...], sina_ref[...]
    cosb, sinb = cosb_ref[...], sinb_ref[...]
    qa_scale = DIFF_QK_DIM ** -0.5 * LOG2E
    q_scale = HEAD_DIM ** -0.5 * LOG2E
    ha = DIFF_QK_DIM // 4
    hb = HEAD_DIM // 4
    for t in range(2):
        qa_ref[0, :, t * LANES:(t + 1) * LANES] = (
            _rope(sec(t), cosa, sina, ha) * qa_scale).astype(BF16)
        ka_ref[0, :, t * LANES:(t + 1) * LANES] = _rope(sec(2 + t), cosa, sina, ha).astype(BF16)
    vat_ref[0, 0] = sec(4, 2).T.astype(BF16)
    for t in range(2):
        bqt_ref[0, 0, t * LANES:(t + 1) * LANES, :] = (
            _rope(sec(6 + t), cosb, sinb, hb) * q_scale).T.astype(BF16)
    bk_ref[0] = _rope(sec(8), cosb, sinb, hb).astype(BF16)
    bvt_ref[0, 0] = sec(9).T.astype(BF16)
    cqt_ref[0, 0] = (sec(10, 2) * q_scale).T.astype(BF16)
    ck_ref[0] = sec(12, 2).astype(BF16)
    cvt_ref[0, 0] = sec(14, 2).T.astype(BF16)
    hd_ref[0] = sec(16, 6).astype(BF16)


def _latent_tile(first):
    return (lambda j: jnp.maximum(j - 1, 0)) if first else (lambda j: j)


def _in_proj_call(xrows, ctx, mod, norm_w, w_in, tabs, first):
    nb = xrows.shape[0]
    rows = xrows.shape[1] + (CTX_LEN if first else 0)
    nt = rows // TILE
    xtile = _latent_tile(first)
    tile = lambda w: pl.BlockSpec((1, TILE, w), lambda b, j: (b, j, 0))
    ttile = lambda w: pl.BlockSpec((1, 1, w, TILE), lambda b, j: (b, j, 0, 0))
    tab = pl.BlockSpec((TILE, LANES), lambda b, j: (j, 0))
    sds = lambda w: jax.ShapeDtypeStruct((nb, rows, w), BF16)
    tsds = lambda w: jax.ShapeDtypeStruct((nb, nt, w, TILE), BF16)
    return pl.pallas_call(
        functools.partial(_in_proj_kernel, first),
        grid=(nb, nt),
        in_specs=[
            pl.BlockSpec((1, TILE, D_MODEL), lambda b, j: (b, xtile(j), 0)),
            pl.BlockSpec((1, CTX_LEN, D_MODEL), lambda b, j: (b, 0, 0)),
            _mod_spec(nb),
            _resident((1, D_MODEL)),
            _resident((D_MODEL, IN_COLS)),
            tab, tab, tab, tab,
        ],
        out_specs=[
            tile(256), tile(256), ttile(256),
            ttile(256), tile(128), ttile(128),
            ttile(256), tile(256), ttile(256),
            tile(768),
        ],
        out_shape=[
            sds(256), sds(256), tsds(256),
            tsds(256), sds(128), tsds(128),
            tsds(256), sds(256), tsds(256),
            sds(768),
        ],
        compiler_params=_params(2),
        name="in_proj",
    )(xrows, ctx, mod, norm_w, w_in, *tabs)


QA_TILE = 256
KV_BLOCK = 512
BLOCKS_PER_STEP = 16
LOOKAHEAD = 2
EXP_CAP = 100.0
PV_ROWS = HEAD_DIM + HALO_BF16


def _diff_attn_kernel(li_ref, lv_ref, q_ref, k_ref, vt_ref, sw_ref, o_ref,
                      qst_ref, sa_ref, sb_ref, mca_ref, mcb_ref, m_ref, acc_ref, gap_ref):
    i = pl.program_id(1)
    s_refs = (sa_ref, sb_ref)
    mc_refs = (mca_ref, mcb_ref)
    q = q_ref[0].astype(F32)
    lane = lax.broadcasted_iota(jnp.int32, q.shape, 1) // DIFF_QK_DIM
    for h in range(DIFF_HEADS):
        qs = jnp.concatenate([jnp.where(lane == 2 * h + c, q, 0.0) for c in range(2)], axis=0)
        qst_ref[h] = qs.T.astype(BF16)
    n_cols = 2 * QA_TILE
    n_blk = (k_ref.shape[1] - CTX_LEN) // KV_BLOCK
    tiles_per_blk = KV_BLOCK // TILE

    def value_rows(vt, h):
        ones = jnp.ones((HALO_BF16, vt.shape[1]), BF16)
        return jnp.concatenate([vt[h * HEAD_DIM:(h + 1) * HEAD_DIM], ones], axis=0)

    def ctx_scores(h):
        return _dot(k_ref[0, 0:CTX_LEN, :], qst_ref[h])

    def ctx_init(h, s):
        m0 = jnp.max(s, axis=0, keepdims=True)
        acc_ref[h] = _dot(value_rows(vt_ref[0, 0], h), jnp.exp2(s - m0).astype(BF16))
        m_ref[h] = m0

    def score_head(blk, slot, h):
        row0 = pl.multiple_of(CTX_LEN + (blk - 1) * KV_BLOCK, TILE)
        s = _dot(k_ref[0, pl.ds(row0, KV_BLOCK), :], qst_ref[h])
        s_refs[slot][h] = s
        mc_refs[slot][h] = jnp.max(s, axis=0, keepdims=True)

    def block_step(score_blk, acc_blk):
        if acc_blk is not None:
            ablk, aslot = acc_blk
            t0 = 1 + (ablk - 1) * tiles_per_blk
            vt = jnp.concatenate([vt_ref[0, t0 + u] for u in range(tiles_per_blk)], axis=1)
        for h in range(DIFF_HEADS):
            if score_blk is not None:
                score_head(*score_blk, h)
            if acc_blk is not None:
                m_old = m_ref[h]
                m_new = jnp.maximum(m_old, mc_refs[aslot][h])
                p = jnp.exp2(s_refs[aslot][h] - m_new).astype(BF16)
                acc_ref[h] = acc_ref[h] * jnp.exp2(m_old - m_new) + _dot(value_rows(vt, h), p)
                m_ref[h] = m_new

    def two_pass_latent():
        scores0 = [ctx_scores(h) for h in range(DIFF_HEADS)]
        for h in range(DIFF_HEADS):
            score_head(1, 0, h)
            ctx_init(h, scores0[h])

        def step(it, carry):
            b = 2 * it + 1
            block_step((b + 1, 1), (b, 0))
            block_step((b + 2, 0), (b + 1, 1))
            return carry

        lax.fori_loop(0, n_blk // 2 - 1, step, 0)
        block_step((n_blk, 1), (n_blk - 1, 0))
        block_step(None, (n_blk, 1))

    def one_pass_latent():
        scores0 = [ctx_scores(h) for h in range(DIFF_HEADS)]
        for h in range(DIFF_HEADS):
            ctx_init(h, scores0[h])
            gap_ref[h] = jnp.zeros(gap_ref.shape[1:], F32)

        def scores(unit):
            blk, h, c = unit
            row0 = pl.multiple_of(CTX_LEN + (blk - 1) * KV_BLOCK, TILE)
            return _dot(k_ref[0, pl.ds(row0, KV_BLOCK), :],
                        qst_ref[h, :, c * QA_TILE:(c + 1) * QA_TILE])

        def consume(unit, s, vt):
            _, h, c = unit
            cols = slice(c * QA_TILE, (c + 1) * QA_TILE)
            m_old = m_ref[h, :, cols]
            m_cur = jnp.max(s, axis=0, keepdims=True)
            pv = _dot(value_rows(vt, h), jnp.exp2(s - m_old).astype(BF16))
            m_new = jnp.maximum(m_old, m_cur)
            acc_ref[h, :, cols] = (acc_ref[h, :, cols] + pv) * jnp.exp2(m_old - m_new)
            m_ref[h, :, cols] = m_new
            gap_ref[h, :, cols] = jnp.maximum(gap_ref[h, :, cols], m_cur - m_old)

        def step(it, carry):
            b0 = it * BLOCKS_PER_STEP + 1
            units = [(b0 + u, h, c) for u in range(BLOCKS_PER_STEP)
                     for h in range(DIFF_HEADS) for c in range(2)]
            pending = [scores(u) for u in units[:LOOKAHEAD]]
            for n, unit in enumerate(units):
                if n + LOOKAHEAD < len(units):
                    pending.append(scores(units[n + LOOKAHEAD]))
                if unit[1] == 0 and unit[2] == 0:
                    t0 = 1 + (unit[0] - 1) * tiles_per_blk
                    vt = jnp.concatenate(
                        [vt_ref[0, t0 + u] for u in range(tiles_per_blk)], axis=1)
                consume(unit, pending.pop(0), vt)
            return carry

        lax.fori_loop(0, n_blk // BLOCKS_PER_STEP, step, 0)

    def finalize():
        lv = lv_ref[...]
        lam = (jnp.exp(jnp.sum(lv[0:1] * lv[1:2], axis=1, keepdims=True))
               - jnp.exp(jnp.sum(lv[2:3] * lv[3:4], axis=1, keepdims=True)) + li_ref[0])
        out_scale = li_ref[1]
        outs = []
        for h in range(DIFF_HEADS):
            a = acc_ref[h]
            o1 = a[0:HEAD_DIM, 0:QA_TILE] / a[HEAD_DIM:HEAD_DIM + 1, 0:QA_TILE]
            o2 = a[0:HEAD_DIM, QA_TILE:n_cols] / a[HEAD_DIM:HEAD_DIM + 1, QA_TILE:n_cols]
            o = o1 - lam * o2
            ms = jnp.mean(o * o, axis=0, keepdims=True)
            outs.append(o * lax.rsqrt(ms + EPS) * sw_ref[...] * out_scale)
        o_ref[0] = jnp.concatenate(outs, axis=0).T.astype(BF16)

    @pl.when(i < CTX_LEN // QA_TILE)
    def _():
        for h in range(DIFF_HEADS):
            ctx_init(h, ctx_scores(h))
        finalize()

    @pl.when(i >= CTX_LEN // QA_TILE)
    def _():
        one_pass_latent()
        finalize()

        @pl.when(jnp.max(gap_ref[...]) > EXP_CAP)
        def _():
            two_pass_latent()
            finalize()


def _diff_attn_call(lam_init, lam_vecs, qa, ka, vat, subw):
    nb, rows, _ = qa.shape
    nt = rows // TILE
    return pl.pallas_call(
        _diff_attn_kernel,
        grid=(nb, rows // QA_TILE),
        in_specs=[
            pl.BlockSpec(memory_space=pltpu.SMEM),
            _resident((4, DIFF_QK_DIM)),
            pl.BlockSpec((1, QA_TILE, 256), lambda b, i: (b, i, 0)),
            pl.BlockSpec((1, rows, 256), lambda b, i: (b, 0, 0)),
            pl.BlockSpec((1, nt, 256, TILE), lambda b, i: (b, 0, 0, 0)),
            _resident((HEAD_DIM, QA_TILE)),
        ],
        out_specs=pl.BlockSpec((1, QA_TILE, 256), lambda b, i: (b, i, 0)),
        out_shape=jax.ShapeDtypeStruct((nb, rows, 256), BF16),
        scratch_shapes=[
            pltpu.VMEM((DIFF_HEADS, 256, 2 * QA_TILE), BF16),
            pltpu.VMEM((DIFF_HEADS, KV_BLOCK, 2 * QA_TILE), F32),
            pltpu.VMEM((DIFF_HEADS, KV_BLOCK, 2 * QA_TILE), F32),
            pltpu.VMEM((DIFF_HEADS, 1, 2 * QA_TILE), F32),
            pltpu.VMEM((DIFF_HEADS, 1, 2 * QA_TILE), F32),
            pltpu.VMEM((DIFF_HEADS, 1, 2 * QA_TILE), F32),
            pltpu.VMEM((DIFF_HEADS, PV_ROWS, 2 * QA_TILE), F32),
            pltpu.VMEM((DIFF_HEADS, 1, 2 * QA_TILE), F32),
        ],
        compiler_params=_params(2),
        name="diff_attn",
    )(lam_init, lam_vecs, qa, ka, vat, subw)


MIX_LOOKAHEAD = 2


def _head_rows(qt, k):
    row = lax.broadcasted_iota(jnp.int32, qt.shape, 0) // HEAD_DIM
    return jnp.where(row == k, qt, jnp.zeros_like(qt))


def _value_rows(vt_parts, rows):
    v = jnp.concatenate([part[rows] for part in vt_parts], axis=1)
    return jnp.concatenate([v, jnp.ones((HALO_BF16, v.shape[1]), BF16)], axis=0)


def _local_mix_kernel(sink_ref, bqt_ref, bkp_ref, bkc_ref, bkn_ref, bkx_ref,
                      bvtp_ref, bvtc_ref, bvtn_ref, bvtx_ref,
                      cqt_ref, ckp_ref, ckc_ref, ckn_ref, ckx_ref,
                      cvtp_ref, cvtc_ref, cvtn_ref, cvtx_ref, bmask_ref, ctab_ref,
                      dp_ref, dc_ref, dn_ref, cw_ref, o_ref, s_ref):
    j = pl.program_id(1)
    n_tiles = pl.num_programs(1)
    half = SWA_WINDOW

    k_b = jnp.concatenate([bkp_ref[0], bkc_ref[0], bkn_ref[0], bkx_ref[0]], axis=0)
    vt_b = [bvtp_ref[0, 0][:, half:], bvtc_ref[0, 0], bvtn_ref[0, 0][:, :half], bvtx_ref[0, 0]]
    vt_c = [cvtp_ref[0, 0], cvtc_ref[0, 0], cvtn_ref[0, 0], cvtx_ref[0, 0]]

    units = [(mixer, p, k) for mixer in "BC" for p in range(2) for k in range(2)]

    def scores(unit):
        mixer, p, k = unit
        rows = slice(p * LANES, (p + 1) * LANES)
        if mixer == "B":
            return _dot(k_b, _head_rows(bqt_ref[0, 0, rows, :], k))
        k_c = jnp.concatenate([ckp_ref[0, :, rows], ckc_ref[0, :, rows],
                               ckn_ref[0, :, rows], ckx_ref[0, :, rows]], axis=0)
        return _dot(k_c, _head_rows(cqt_ref[0, 0, rows, :], k))

    def attend(unit, s):
        mixer, p, k = unit
        feat = slice(k * HEAD_DIM, (k + 1) * HEAD_DIM)
        if mixer == "B":
            s = s + bmask_ref[0]
            sink = sink_ref[2 * k + p] * LOG2E
            m = jnp.maximum(jnp.max(s, axis=0, keepdims=True), sink)
            pv = _dot(_value_rows(vt_b, feat), jnp.exp2(s - m).astype(BF16))
            return pv[0:HEAD_DIM] / (pv[HEAD_DIM:HEAD_DIM + 1] + jnp.exp2(sink - m))
        s = s + ctab_ref[0, 2 * p + k]
        m = jnp.max(s, axis=0, keepdims=True)
        rows = slice(p * LANES + k * HEAD_DIM, p * LANES + (k + 1) * HEAD_DIM)
        pv = _dot(_value_rows(vt_c, rows), jnp.exp2(s - m).astype(BF16))
        return pv[0:HEAD_DIM] / pv[HEAD_DIM:HEAD_DIM + 1]

    def write_out(outs):
        o_ref[0, :, 0:GROUP_WIDTH] = jnp.concatenate(outs[0:4], axis=0).T.astype(BF16)
        o_ref[0, :, GROUP_WIDTH:2 * GROUP_WIDTH] = jnp.concatenate(outs[4:8], axis=0).T.astype(BF16)

    def two_pass():
        def park(n):
            s = scores(units[n])
            s_ref[n % (MIX_LOOKAHEAD + 1), 0:s.shape[0], :] = s

        for n in range(MIX_LOOKAHEAD):
            park(n)
        outs = []
        for n, unit in enumerate(units):
            if n + MIX_LOOKAHEAD < len(units):
                park(n + MIX_LOOKAHEAD)
            n_keys = (TILE + 2 * SWA_WINDOW if unit[0] == "B" else 3 * TILE) + CTX_LEN
            outs.append(attend(unit, s_ref[n % (MIX_LOOKAHEAD + 1), 0:n_keys, :]))
        write_out(outs)

    def unit_operands(unit):
        mixer, p, k = unit
        rows = slice(p * LANES, (p + 1) * LANES)
        if mixer == "B":
            n_loc = TILE + 2 * half
            return (_head_rows(bqt_ref[0, 0, rows, :], k), k_b[0:n_loc], k_b[n_loc:],
                    bmask_ref[0, 0:n_loc, :], vt_b, slice(k * HEAD_DIM, (k + 1) * HEAD_DIM))
        k_loc = jnp.concatenate([ckp_ref[0, :, rows], ckc_ref[0, :, rows], ckn_ref[0, :, rows]],
                                axis=0)
        feat = slice(p * LANES + k * HEAD_DIM, p * LANES + (k + 1) * HEAD_DIM)
        return (_head_rows(cqt_ref[0, 0, rows, :], k), k_loc, ckx_ref[0, :, rows],
                ctab_ref[0, 2 * p + k, 0:3 * TILE, :], vt_c, feat)

    def one_pass_scores(unit):
        qt, k_loc, k_ctx, _, _, _ = unit_operands(unit)
        return _dot(k_ctx, qt), _dot(k_loc, qt)

    def one_pass_attend(unit, s_ctx, s_loc):
        mixer, p, k = unit
        _, _, _, table, vt, feat = unit_operands(unit)
        m = jnp.max(s_ctx, axis=0, keepdims=True)
        if mixer == "B":
            sink = sink_ref[2 * k + p] * LOG2E
            m = jnp.maximum(m, sink)
        s_loc = s_loc + table
        excess = jnp.max(s_loc, axis=0, keepdims=True) - m
        vx = _value_rows(vt, feat)
        n_loc = s_loc.shape[0]
        pv = (_dot(vx[:, 0:n_loc], jnp.exp2(s_loc - m).astype(BF16))
              + _dot(vx[:, n_loc:], jnp.exp2(s_ctx - m).astype(BF16)))
        l = pv[HEAD_DIM:HEAD_DIM + 1]
        if mixer == "B":
            l = l + jnp.exp2(sink - m)
        return pv[0:HEAD_DIM] / l, excess

    pending = [one_pass_scores(units[0])]
    outs, worst = [], None
    for n, unit in enumerate(units):
        if n + 1 < len(units):
            pending.append(one_pass_scores(units[n + 1]))
        o, excess = one_pass_attend(unit, *pending.pop(0))
        outs.append(o)
        worst = excess if worst is None else jnp.maximum(worst, excess)
    write_out(outs)

    @pl.when(jnp.max(worst) > EXP_CAP)
    def _():
        two_pass()

    has_prev = (j > 1).astype(F32)
    has_next = ((j > 0) & (j < n_tiles - 1)).astype(F32)
    dc = dc_ref[0].astype(F32)
    dp = dp_ref[0].astype(F32)[HALO_BF16 - HALO_F32:]
    dn = dn_ref[0].astype(F32)[:HALO_F32]
    gw = GROUP_WIDTH
    z = jnp.concatenate([dp[:, gw:2 * gw] * dp[:, 2 * gw:] * has_prev,
                         dc[:, gw:2 * gw] * dc[:, 2 * gw:],
                         dn[:, gw:2 * gw] * dn[:, 2 * gw:] * has_next], axis=0)
    cw = cw_ref[...]
    c0 = HALO_F32
    conv = (z[c0 - 1:c0 - 1 + TILE] * cw[0:1] + z[c0:c0 + TILE] * cw[1:2]
            + z[c0 + 1:c0 + 1 + TILE] * cw[2:3])
    o_ref[0, :, 4 * LANES:6 * LANES] = (dc[:, 0:gw] * conv).astype(BF16)


def _tile_variants(nt):
    return (0, 1, 2, nt - 1)


def _local_mix_call(sink, bqt, bk, bvt, cqt, ck, cvt, bmask, ctab, hd, conv_w):
    nb, rows, _ = bk.shape
    nt = rows // TILE
    variant = lambda j: jnp.where(j == 0, 0, jnp.where(j == 1, 1, jnp.where(j == nt - 1, 3, 2)))
    hb = TILE // SWA_WINDOW
    h16 = TILE // HALO_BF16
    prev_j = lambda j: jnp.maximum(j - 1, 0)
    next_j = lambda j: jnp.minimum(j + 1, nt - 1)
    cur = lambda w: pl.BlockSpec((1, TILE, w), lambda b, j: (b, j, 0))
    prev = lambda w: pl.BlockSpec((1, TILE, w), lambda b, j: (b, prev_j(j), 0))
    nxt = lambda w: pl.BlockSpec((1, TILE, w), lambda b, j: (b, next_j(j), 0))
    ctx = lambda w: pl.BlockSpec((1, TILE, w), lambda b, j: (b, 0, 0))
    tcur = lambda w: pl.BlockSpec((1, 1, w, TILE), lambda b, j: (b, j, 0, 0))
    tprev = lambda w: pl.BlockSpec((1, 1, w, TILE), lambda b, j: (b, prev_j(j), 0, 0))
    tnxt = lambda w: pl.BlockSpec((1, 1, w, TILE), lambda b, j: (b, next_j(j), 0, 0))
    tctx = lambda w: pl.BlockSpec((1, 1, w, TILE), lambda b, j: (b, 0, 0, 0))
    return pl.pallas_call(
        _local_mix_kernel,
        grid=(nb, nt),
        in_specs=[
            pl.BlockSpec(memory_space=pltpu.SMEM),
            tcur(256),
            pl.BlockSpec((1, SWA_WINDOW, 128), lambda b, j: (b, jnp.maximum(j * hb - 1, 0), 0)),
            cur(128),
            pl.BlockSpec((1, SWA_WINDOW, 128), lambda b, j: (b, jnp.minimum((j + 1) * hb, nt * hb - 1), 0)),
            ctx(128),
            tprev(128), tcur(128), tnxt(128), tctx(128),
            tcur(256),
            prev(256), cur(256), nxt(256), ctx(256),
            tprev(256), tcur(256), tnxt(256), tctx(256),
            pl.BlockSpec((1, TILE + 2 * SWA_WINDOW + CTX_LEN, TILE), lambda b, j: (variant(j), 0, 0)),
            pl.BlockSpec((1, NA_HEADS, 3 * TILE + CTX_LEN, TILE),
                         lambda b, j: (variant(j), 0, 0, 0)),
            pl.BlockSpec((1, HALO_BF16, 768), lambda b, j: (b, jnp.maximum(j * h16 - 1, 0), 0)),
            cur(768),
            pl.BlockSpec((1, HALO_BF16, 768), lambda b, j: (b, jnp.minimum((j + 1) * h16, nt * h16 - 1), 0)),
            _resident((3, GROUP_WIDTH)),
        ],
        out_specs=pl.BlockSpec((1, TILE, 768), lambda b, j: (b, j, 0)),
        out_shape=jax.ShapeDtypeStruct((nb, rows, 768), BF16),
        scratch_shapes=[pltpu.VMEM((MIX_LOOKAHEAD + 1, 3 * TILE + CTX_LEN, TILE), F32)],
        compiler_params=_params(2),
        name="local_mix",
    )(sink, bqt, bk, bk, bk, bk, bvt, bvt, bvt, bvt, cqt, ck, ck, ck, ck, cvt, cvt, cvt, cvt,
      bmask, ctab, hd, hd, hd, conv_w)


EXT = TILE + 2 * HALO_F32


def _halo_f32(prev_ref, next_ref):
    p = prev_ref[0].astype(F32)
    n = next_ref[0].astype(F32)
    return p[p.shape[0] - HALO_F32:], n[:HALO_F32]


def _ffn_tile(final, first, j, n_tiles, xp_ref, xc_ref, xn_ref, ctx_ref, ap_ref, ac_ref, an_ref,
              mp_ref, mc_ref, mn_ref, mod_ref, wo_ref, nw_ref, wu_ref, cw_ref, cb_ref, wd_ref,
              fw_ref, u_ref):
    mod = mod_ref[0]
    g1 = mod[:, 2 * D_MODEL:3 * D_MODEL]
    sh2 = mod[:, 3 * D_MODEL:4 * D_MODEL]
    sc2 = mod[:, 4 * D_MODEL:5 * D_MODEL]
    g2 = mod[:, 5 * D_MODEL:6 * D_MODEL]

    ap, an = _halo_f32(ap_ref, an_ref)
    mp, mn = _halo_f32(mp_ref, mn_ref)
    mix = jnp.concatenate([
        jnp.concatenate([ap, mp], axis=1),
        jnp.concatenate([ac_ref[0], mc_ref[0]], axis=1).astype(F32),
        jnp.concatenate([an, mn], axis=1)], axis=0).astype(BF16)
    xc = xc_ref[0]
    if first:
        xc = jnp.where(j == 0, ctx_ref[0], xc)
    x = jnp.concatenate([xp_ref[0], xc, xn_ref[0]], axis=0)
    x1 = x + g1 * _dot(mix, wo_ref[...])
    f = _rmsnorm_mod(x1, nw_ref[...], sh2, sc2).astype(BF16)

    has_prev = (j > 1).astype(F32)
    has_next = ((j > 0) & (j < n_tiles - 1)).astype(F32)
    lo = HALO_F32 - 1
    hi = HALO_F32 + TILE
    u = _dot(f, wu_ref[...])
    slabs = []
    for t in range(2 * D_FF // LANES):
        cols = slice(t * LANES, (t + 1) * LANES)
        u_ref[t] = u[:, cols]
        u_ref[t, lo:lo + 1, :] = u_ref[t, lo:lo + 1, :] * has_prev
        u_ref[t, hi:hi + 1, :] = u_ref[t, hi:hi + 1, :] * has_next
        cw = cw_ref[:, cols]
        slabs.append(u_ref[t, pl.ds(lo, TILE), :] * cw[0:1]
                     + u_ref[t, pl.ds(lo + 1, TILE), :] * cw[1:2]
                     + u_ref[t, pl.ds(lo + 2, TILE), :] * cw[2:3] + cb_ref[:, cols])
    g = jnp.concatenate(slabs[:D_FF // LANES], axis=1)
    v = jnp.concatenate(slabs[D_FF // LANES:], axis=1)
    act = (g * (1.0 / (1.0 + jnp.exp(-g))) * v).astype(BF16)
    x2 = x1[HALO_F32:HALO_F32 + TILE] + g2 * _dot(act, wd_ref[...])
    if final:
        ms = jnp.mean(x2 * x2, axis=-1, keepdims=True)
        x2 = x2 * lax.rsqrt(ms + EPS) * fw_ref[...]
    return x2


N_TILE_REFS = 11


def _out_ffn_kernel(final, first, *refs):
    tile_refs, weights, (o_ref, u_ref) = refs[:N_TILE_REFS], refs[N_TILE_REFS:-2], refs[-2:]
    o_ref[0] = _ffn_tile(final, first, pl.program_id(1), pl.num_programs(1),
                         *tile_refs, *weights, u_ref)


def _out_ffn_call(xrows, ctx, oa, obcd, mod, w_out, norm_w, w_up, conv_w, conv_b, w_down, final_w,
                  final, first):
    nb, rows, _ = oa.shape
    nt = rows // TILE
    h8 = TILE // HALO_F32
    h16 = TILE // HALO_BF16

    def trio(w, halo, per_tile, tile_of, n_arr_tiles):
        return [
            pl.BlockSpec((1, halo, w),
                         lambda b, j: (b, jnp.maximum(tile_of(j) * per_tile - 1, 0), 0)),
            pl.BlockSpec((1, TILE, w), lambda b, j: (b, tile_of(j), 0)),
            pl.BlockSpec((1, halo, w), lambda b, j: (
                b, jnp.minimum((tile_of(j) + 1) * per_tile, n_arr_tiles * per_tile - 1), 0)),
        ]

    same = lambda j: j
    tile_specs = [*trio(D_MODEL, HALO_F32, h8, _latent_tile(first), xrows.shape[1] // TILE),
                  pl.BlockSpec((1, CTX_LEN, D_MODEL), lambda b, j: (b, 0, 0)),
                  *trio(256, HALO_BF16, h16, same, nt), *trio(768, HALO_BF16, h16, same, nt),
                  _mod_spec(nb)]
    weight_specs = [
        _resident((D_MODEL, D_MODEL)),
        _resident((1, D_MODEL)),
        _resident((D_MODEL, 2 * D_FF)),
        _resident((3, 2 * D_FF)),
        _resident((1, 2 * D_FF)),
        _resident((D_FF, D_MODEL)),
        _resident((1, D_MODEL)),
    ]
    if final:
        out_spec = pl.BlockSpec((1, TILE, D_MODEL), lambda b, j: (b, jnp.maximum(j - 1, 0), 0))
        out_shape = jax.ShapeDtypeStruct((nb, rows - CTX_LEN, D_MODEL), F32)
    else:
        out_spec = pl.BlockSpec((1, TILE, D_MODEL), lambda b, j: (b, j, 0))
        out_shape = jax.ShapeDtypeStruct((nb, rows, D_MODEL), F32)
    return pl.pallas_call(
        functools.partial(_out_ffn_kernel, final, first),
        grid=(nb, nt),
        in_specs=[*tile_specs, *weight_specs],
        out_specs=out_spec,
        out_shape=out_shape,
        scratch_shapes=[pltpu.VMEM((2 * D_FF // LANES, EXT, LANES), F32)],
        compiler_params=_params(2, [i in (N_TILE_REFS + 2, N_TILE_REFS + 5)
                                    for i in range(N_TILE_REFS + len(weight_specs))]),
        name="out_ffn_final" if final else "out_ffn",
    )(xrows, xrows, xrows, ctx, oa, oa, oa, obcd, obcd, obcd, mod, w_out, norm_w, w_up, conv_w,
      conv_b, w_down, final_w)


def _rope_tables(n_lat, dim):
    quarter = dim // 4
    t = np.arange(n_lat)
    pos = np.stack([t // GRID_W, t % GRID_W], axis=1).astype(np.float32)
    freqs = jnp.asarray(ROPE_THETA, F32) ** (-(jnp.arange(quarter, dtype=F32) / quarter))
    lane = np.arange(LANES) % dim
    axis = lane // (dim // 2)
    fidx = lane % quarter
    sign = np.where((lane % (dim // 2)) < quarter, -1.0, 1.0).astype(np.float32)
    ang = jnp.asarray(pos)[:, axis] * freqs[fidx][None, :]
    cos = jnp.concatenate([jnp.ones((CTX_LEN, LANES), F32), jnp.cos(ang)], axis=0)
    sin = jnp.concatenate([jnp.zeros((CTX_LEN, LANES), F32), jnp.sin(ang) * sign[None, :]], axis=0)
    return cos, sin


def _na_bias_table(rpb):
    rpt = TILE // GRID_W
    n_heads, n_dr, n_dc = rpb.shape
    cq = np.arange(GRID_W)[:, None]
    ck = np.arange(GRID_W)[None, :]
    dc = np.clip(ck - cq, -(NA_COLS - 1), NA_COLS - 1) + NA_COLS - 1
    col_start = np.clip(cq - NA_COLS // 2, 0, GRID_W - NA_COLS)
    col_ok = (ck >= col_start) & (ck < col_start + NA_COLS)
    pick = dc[None, :, :] == np.arange(n_dc)[:, None, None]
    e = jnp.sum(jnp.where(pick[None, None], rpb.astype(F32)[:, :, :, None, None], 0.0), axis=2)
    e = jnp.where(col_ok[None, None], e, NEG_INF)
    blocks = []
    for rq in range(rpt):
        lo = NA_ROWS - 1 - rpt - rq
        blocks.append(e[:, lo:lo + 3 * rpt].transpose(0, 1, 3, 2))
    local = jnp.stack(blocks, axis=3).reshape(n_heads, 3 * TILE, TILE)
    return jnp.concatenate([local, jnp.zeros((n_heads, CTX_LEN, TILE), F32)], axis=1)


def _window_masks(n_lat):
    nt = 1 + n_lat // TILE
    half = SWA_WINDOW
    rpt = TILE // GRID_W
    n_rows = n_lat // GRID_W
    rq = np.arange(TILE)[None, :]
    kb = np.arange(TILE + 2 * half + CTX_LEN)[:, None]
    kc = np.arange(3 * TILE + CTX_LEN)[:, None]
    masks_b, masks_c = [], []
    for j in _tile_variants(nt):
        kpos = (j - 1) * TILE - half + kb
        band = (np.abs(kb - half - rq) <= SWA_WINDOW) & (kpos >= 0) & (kpos < n_lat) & (j > 0)
        masks_b.append(band | (kb >= TILE + 2 * half))
        r = (j - 1) * rpt + rq // GRID_W
        kr = (j - 2) * rpt + kc // GRID_W
        rs = np.clip(r - NA_ROWS // 2, 0, n_rows - NA_ROWS)
        masks_c.append(((kr >= rs) & (kr < rs + NA_ROWS) & (j > 0)) | (kc >= 3 * TILE))
    to_add = lambda ms: jnp.where(jnp.asarray(np.stack(ms)), 0.0, NEG_INF).astype(F32)
    return to_add(masks_b), to_add(masks_c)


def _swa_head_order(w, axis, start):
    def blk(lo, hi):
        return lax.slice_in_dim(w, lo, hi, axis=axis)
    heads = [blk(start + h * HEAD_DIM, start + (h + 1) * HEAD_DIM) for h in (0, 2, 1, 3)]
    return jnp.concatenate(
        [blk(0, start), *heads, blk(start + SWA_HEADS * HEAD_DIM, w.shape[axis])], axis=axis)


def kernel(x, c, ctx, c_ctx, norm_mix_w, norm_ffn_w, w_mod, b_mod, w_in, w_out, diff_lambda_q1, diff_lambda_k1, diff_lambda_q2, diff_lambda_k2, diff_subln_w, swa_sink, na_rpb, sconv_w, ffn_w_up, ffn_conv_w, ffn_conv_b, ffn_w_down, final_norm_w):
    nb, n_lat, d = x.shape
    depth = w_in.shape[0]
    assert d == D_MODEL and ctx.shape[1] == CTX_LEN and n_lat % TILE == 0 and nb < 8
    assert n_lat // GRID_W >= NA_ROWS and n_lat % (max(2, BLOCKS_PER_STEP) * KV_BLOCK) == 0

    xrows = x
    cin = jnp.concatenate([c, c_ctx[None, :], jnp.zeros((8 - nb - 1, d), F32)], axis=0)
    mods = _mod_call(cin, w_mod, b_mod)

    tabs = (*_rope_tables(n_lat, DIFF_QK_DIM), *_rope_tables(n_lat, HEAD_DIM))
    mask_b, mask_c = _window_masks(n_lat)
    for l in range(depth):
        mod = mods[l].reshape(8, 1, 6 * D_MODEL)
        w_in_l = _swa_head_order(w_in[l], 1, 3 * GROUP_WIDTH).astype(BF16)
        w_out_l = _swa_head_order(w_out[l], 0, GROUP_WIDTH).astype(BF16)
        qa, ka, vat, bqt, bk, bvt, cqt, ck, cvt, hd = _in_proj_call(
            xrows, ctx, mod, norm_mix_w[l][None, :], w_in_l, tabs, first=(l == 0))

        lambda_init = 0.8 - 0.6 * math.exp(-0.3 * l)
        lam_init = jnp.asarray([lambda_init, 1.0 - lambda_init], F32)
        lam_vecs = jnp.stack([diff_lambda_q1[l], diff_lambda_k1[l],
                              diff_lambda_q2[l], diff_lambda_k2[l]])
        subw = jnp.broadcast_to(diff_subln_w[l][:, None], (HEAD_DIM, QA_TILE))
        oa = _diff_attn_call(lam_init, lam_vecs, qa, ka, vat, subw)

        ctab = (_na_bias_table(na_rpb[l])[None] + mask_c[:, None]) * LOG2E
        obcd = _local_mix_call(swa_sink[l], bqt, bk, bvt, cqt, ck, cvt, mask_b, ctab,
                               hd, sconv_w[l])

        xrows = _out_ffn_call(
            xrows, ctx, oa, obcd, mod, w_out_l, norm_ffn_w[l][None, :],
            ffn_w_up[l].astype(BF16), ffn_conv_w[l], ffn_conv_b[l][None, :],
            ffn_w_down[l].astype(BF16), final_norm_w[None, :],
            final=(l == depth - 1), first=(l == 0))
    return xrows
```
